```python
import jax, jax.numpy as jnp
from jax import lax
import numpy as np

D_MODEL = 2048
BATCH = 4
SEQ = 2048
DEPTH = 4
DEC_BATCH = 128
DEC_SEQ = 8
PAST_LEN = 16384
PAGE_SIZE = 128

N_MIXERS = 2
N_DELTA_LAYERS = (DEPTH + 1) // N_MIXERS
N_POOL_LAYERS = DEPTH // N_MIXERS
EXPAND = 2
D_INNER = EXPAND * D_MODEL
HEAD_DIM = 128
N_QK_HEADS = D_MODEL // HEAD_DIM
N_V_HEADS = D_INNER // HEAD_DIM
QK_DIM = N_QK_HEADS * HEAD_DIM
CONV_WIDTH = 4
CONV_DIM = 2 * QK_DIM + D_INNER
DELTA_IN_DIM = CONV_DIM + D_INNER + 2 * N_V_HEADS
CHUNK = 64
POOL_WINDOWS = (2, 4, 8, 16)
N_POOL_GROUPS = len(POOL_WINDOWS)
POOL_GROUP_DIM = D_INNER // N_POOL_GROUPS
POOL_BUF = max(POOL_WINDOWS) - 1
NORM_EPS = 1e-6

kernel_name = 'hybrid_gated_delta_pool_decoder_step'


def rmsnorm(x, w):
    xf = x.astype(jnp.float32)
    y = xf * lax.rsqrt(jnp.mean(xf * xf, axis=-1, keepdims=True) + NORM_EPS)
    return (y * w.astype(jnp.float32)).astype(x.dtype)


def l2norm(x):
    return x * lax.rsqrt(jnp.sum(x * x, axis=-1, keepdims=True) + NORM_EPS)


def causal_conv(x, buf, w):
    t = x.shape[1]
    xp = jnp.concatenate([buf.astype(x.dtype), x], axis=1)
    y = w[0] * xp[:, 0:t]
    for j in range(1, CONV_WIDTH):
        y = y + w[j] * xp[:, j:j + t]
    return jax.nn.silu(y), xp[:, -(CONV_WIDTH - 1):].astype(buf.dtype)


def _to_chunks(a, n, c):
    b, t = a.shape[:2]
    a = jnp.pad(a, [(0, 0), (0, n * c - t)] + [(0, 0)] * (a.ndim - 2))
    a = a.reshape((b, n, c) + a.shape[2:])
    return jnp.transpose(a, (1, 0, 3, 2) + tuple(range(4, a.ndim)))


def gated_delta_rule(q, k, v, g, beta, s0):
    b, t, h, _ = q.shape
    dv = v.shape[-1]
    c = min(CHUNK, t)
    n = -(-t // c)
    q, k, v = _to_chunks(q, n, c), _to_chunks(k, n, c), _to_chunks(v, n, c)
    g, beta = _to_chunks(g, n, c), _to_chunks(beta, n, c)
    gc = jnp.cumsum(g, axis=-1)
    diff = gc[..., :, None] - gc[..., None, :]
    incl = jnp.tril(jnp.ones((c, c), dtype=bool))
    strict = jnp.tril(jnp.ones((c, c), dtype=bool), -1)
    decay_incl = jnp.exp(jnp.where(incl, diff, -jnp.inf))
    decay_strict = jnp.where(strict, decay_incl, 0.0)
    kb = k * beta[..., None]
    lmat = jnp.einsum('nbhcd,nbhed->nbhce', kb, k) * decay_strict
    eye = jnp.eye(c, dtype=jnp.float32)
    tmat = lax.linalg.triangular_solve(lmat + eye, jnp.broadcast_to(eye, lmat.shape),
                                       left_side=True, lower=True, unit_diagonal=True)
    u_base = jnp.einsum('nbhce,nbhef->nbhcf', tmat, v * beta[..., None])
    w_dec = jnp.einsum('nbhce,nbhed->nbhcd', tmat, kb * jnp.exp(gc)[..., None])
    qk = jnp.einsum('nbhcd,nbhed->nbhce', q, k) * decay_incl
    qg = q * jnp.exp(gc)[..., None]
    g_last = gc[..., -1]
    kdec = k * jnp.exp(g_last[..., None] - gc)[..., None]

    def step(s, xs):
        u_b, w_c, qg_c, qk_c, kd_c, gl = xs
        u = u_b - jnp.einsum('bhcd,bhde->bhce', w_c, s)
        o = jnp.einsum('bhcd,bhde->bhce', qg_c, s) + jnp.einsum('bhce,bhef->bhcf', qk_c, u)
        s = s * jnp.exp(gl)[..., None, None] + jnp.einsum('bhcd,bhce->bhde', kd_c, u)
        return s, o

    s_final, o = lax.scan(step, s0, (u_base, w_dec, qg, qk, kdec, g_last))
    o = jnp.transpose(o, (1, 0, 3, 2, 4)).reshape(b, n * c, h, dv)[:, :t]
    return o, s_final


def delta_mixer(h, s0, conv_buf, w_in, conv_w, a_log, dt_bias, o_norm_w, w_out):
    b, t, _ = h.shape
    proj = h @ w_in
    qkv, z, beta_raw, a_raw = jnp.split(
        proj, [CONV_DIM, CONV_DIM + D_INNER, CONV_DIM + D_INNER + N_V_HEADS], axis=-1)
    qkv, conv_new = causal_conv(qkv, conv_buf, conv_w)
    q, k, v = jnp.split(qkv.astype(jnp.float32), [QK_DIM, 2 * QK_DIM], axis=-1)
    rep = N_V_HEADS // N_QK_HEADS
    q = jnp.repeat(l2norm(q.reshape(b, t, N_QK_HEADS, HEAD_DIM)) * (HEAD_DIM ** -0.5), rep, axis=2)
    k = jnp.repeat(l2norm(k.reshape(b, t, N_QK_HEADS, HEAD_DIM)), rep, axis=2)
    v = v.reshape(b, t, N_V_HEADS, HEAD_DIM)
    beta = jax.nn.sigmoid(beta_raw.astype(jnp.float32))
    g = -jnp.exp(a_log.astype(jnp.float32)) * jax.nn.softplus(
        a_raw.astype(jnp.float32) + dt_bias.astype(jnp.float32))
    o, s_new = gated_delta_rule(q, k, v, g, beta, s0.astype(jnp.float32))
    o = rmsnorm(o, o_norm_w) * jax.nn.silu(z.astype(jnp.float32).reshape(b, t, N_V_HEADS, HEAD_DIM))
    out = o.reshape(b, t, D_INNER).astype(h.dtype) @ w_out
    return out, s_new.astype(s0.dtype), conv_new


def pool_mixer(h, buf, start_pos, w_in, w_group, scale, w_out):
    b, t, _ = h.shape
    proj = h @ w_in
    u, z = jnp.split(proj, [D_INNER], axis=-1)
    xp = jnp.concatenate([buf.astype(u.dtype), u], axis=1).astype(jnp.float32)
    cs = jnp.pad(jnp.cumsum(xp, axis=1), ((0, 0), (1, 0), (0, 0)))
    end = cs[:, POOL_BUF + 1:POOL_BUF + 1 + t]
    pos = start_pos + jnp.arange(t, dtype=jnp.int32)
    means = []
    for gi, win in enumerate(POOL_WINDOWS):
        sl = slice(gi * POOL_GROUP_DIM, (gi + 1) * POOL_GROUP_DIM)
        start = cs[:, POOL_BUF + 1 - win:POOL_BUF + 1 - win + t, sl]
        cnt = jnp.minimum(pos + 1, win).astype(jnp.float32)[None, :, None]
        means.append((end[..., sl] - start) / cnt)
    pooled = jnp.concatenate(means, axis=-1) - xp[:, POOL_BUF:]
    mixed = jnp.einsum('btgc,gcd->btgd', pooled.reshape(b, t, N_POOL_GROUPS, POOL_GROUP_DIM),
                       w_group.astype(jnp.float32)).reshape(b, t, D_INNER)
    y = mixed * scale.astype(jnp.float32) * jax.nn.silu(z.astype(jnp.float32))
    out = y.astype(h.dtype) @ w_out
    return out, xp[:, -POOL_BUF:].astype(buf.dtype)


def setup_inputs(seed: int = 0) -> dict:
    key = jax.random.key(seed)
    ks = jax.random.split(key, 20)
    f32 = jnp.float32
    nd, npl = N_DELTA_LAYERS, N_POOL_LAYERS
    dt = jnp.exp(jax.random.uniform(ks[7], (nd, N_V_HEADS), f32, np.log(1e-3), np.log(1e-1)))
    return {
        'x_prompt': jax.random.normal(ks[0], (BATCH, SEQ, D_MODEL), f32),
        'x_sample': jax.random.normal(ks[1], (DEC_BATCH, DEC_SEQ, D_MODEL), f32),
        'state_delta': 0.1 * jax.random.normal(ks[2], (nd, DEC_BATCH, N_V_HEADS, HEAD_DIM, HEAD_DIM), f32),
        'cache_conv': jax.random.normal(ks[3], (nd, DEC_BATCH, CONV_WIDTH - 1, CONV_DIM), f32),
        'cache_pool': jax.random.normal(ks[4], (npl, DEC_BATCH, POOL_BUF, D_INNER), f32),
        'norm_w': 1.0 + 0.02 * jax.random.normal(ks[5], (DEPTH, D_MODEL), f32),
        'final_norm_w': 1.0 + 0.02 * jax.random.normal(ks[6], (D_MODEL,), f32),
        'delta_w_in': jax.random.normal(ks[8], (nd, D_MODEL, DELTA_IN_DIM), f32) * D_MODEL ** -0.5,
        'delta_conv_w': jax.random.normal(ks[9], (nd, CONV_WIDTH, CONV_DIM), f32) * CONV_WIDTH ** -0.5,
        'delta_a_log': jnp.log(jax.random.uniform(ks[10], (nd, N_V_HEADS), f32, 1.0, 16.0)),
        'delta_dt_bias': dt + jnp.log(-jnp.expm1(-dt)),
        'delta_o_norm_w': 1.0 + 0.02 * jax.random.normal(ks[11], (nd, HEAD_DIM), f32),
        'delta_w_out': jax.random.normal(ks[12], (nd, D_INNER, D_MODEL), f32) * D_INNER ** -0.5,
        'pool_w_in': jax.random.normal(ks[13], (npl, D_MODEL, 2 * D_INNER), f32) * D_MODEL ** -0.5,
        'pool_w_group': jax.random.normal(ks[14], (npl, N_POOL_GROUPS, POOL_GROUP_DIM, POOL_GROUP_DIM), f32) * POOL_GROUP_DIM ** -0.5,
        'pool_scale': 1.0 + 0.02 * jax.random.normal(ks[15], (npl, D_INNER), f32),
        'pool_w_out': jax.random.normal(ks[16], (npl, D_INNER, D_MODEL), f32) * D_INNER ** -0.5,
    }


def reference(x_prompt, x_sample, state_delta, cache_conv, cache_pool, norm_w, final_norm_w,
              delta_w_in, delta_conv_w, delta_a_log, delta_dt_bias, delta_o_norm_w, delta_w_out,
              pool_w_in, pool_w_group, pool_scale, pool_w_out):
    bp = x_prompt.shape[0]
    s_zero = jnp.zeros((bp,) + state_delta.shape[2:], state_delta.dtype)
    c_zero = jnp.zeros((bp,) + cache_conv.shape[2:], cache_conv.dtype)
    p_zero = jnp.zeros((bp,) + cache_pool.shape[2:], cache_pool.dtype)
    hp, hs = x_prompt, x_sample
    sd_p, cc_p, cp_p, sd_s, cc_s, cp_s = [], [], [], [], [], []
    for i in range(DEPTH):
        j = i // N_MIXERS
        np_in, ns_in = rmsnorm(hp, norm_w[i]), rmsnorm(hs, norm_w[i])
        if i % N_MIXERS == 0:
            ws = (delta_w_in[j], delta_conv_w[j], delta_a_log[j], delta_dt_bias[j],
                  delta_o_norm_w[j], delta_w_out[j])
            op, sp, cp = delta_mixer(np_in, s_zero, c_zero, *ws)
            os_, ss, cs = delta_mixer(ns_in, state_delta[j], cache_conv[j], *ws)
            sd_p.append(sp); cc_p.append(cp); sd_s.append(ss); cc_s.append(cs)
        else:
            ws = (pool_w_in[j], pool_w_group[j], pool_scale[j], pool_w_out[j])
            op, pp = pool_mixer(np_in, p_zero, 0, *ws)
            os_, ps = pool_mixer(ns_in, cache_pool[j], PAST_LEN, *ws)
            cp_p.append(pp); cp_s.append(ps)
        hp = hp + op
        hs = hs + os_
    y_prompt = rmsnorm(hp, final_norm_w)
    y_sample = rmsnorm(hs, final_norm_w)
    new_state_delta_p = jnp.stack(sd_p)
    new_cache_conv_p = jnp.stack(cc_p)
    new_cache_pool_p = jnp.stack(cp_p)
    new_state_delta_s = jnp.stack(sd_s)
    new_cache_conv_s = jnp.stack(cc_s)
    new_cache_pool_s = jnp.stack(cp_s)
    return (y_prompt, y_sample, new_state_delta_p, new_cache_conv_p, new_cache_pool_p,
            new_state_delta_s, new_cache_conv_s, new_cache_pool_s)
```

```python
import functools

import jax
import jax.numpy as jnp
from jax import lax
from jax.experimental import pallas as pl
from jax.experimental.pallas import tpu as pltpu

F32 = jnp.float32
BF16 = jnp.bfloat16

HEAD_DIM = 128
CONV_WIDTH = 4
CHUNK = 64
CHUNKS_PER_STEP = 4
POOL_WINDOWS = (2, 4, 8, 16)
POOL_BUF = max(POOL_WINDOWS) - 1
PAST_LEN = 16384
NORM_EPS = 1e-6
NEG_BIG = -1e30

LANES = 128
SUBLANES = 8
VMEM_LIMIT = 56 * 1024 * 1024


def _cparams(*sem):
    return pltpu.CompilerParams(dimension_semantics=sem, vmem_limit_bytes=VMEM_LIMIT)


def _silu(x):
    return x * jax.nn.sigmoid(x)


def _rmsnorm_kernel(x_ref, w_ref, o_ref):
    x = x_ref[...]
    y = x * lax.rsqrt(jnp.mean(x * x, axis=-1, keepdims=True) + NORM_EPS)
    o_ref[...] = (y * w_ref[...]).astype(o_ref.dtype)


def _rmsnorm(x, w, out_dtype, rows, row_off=0, tm=512):
    d = x.shape[1]
    off = row_off // tm
    return pl.pallas_call(
        _rmsnorm_kernel,
        grid=(rows // tm,),
        in_specs=[pl.BlockSpec((tm, d), lambda i: (i + off, 0)),
                  pl.BlockSpec((1, d), lambda i: (0, 0))],
        out_specs=pl.BlockSpec((tm, d), lambda i: (i, 0)),
        out_shape=jax.ShapeDtypeStruct((rows, d), out_dtype),
        compiler_params=_cparams("arbitrary"),
        name="rmsnorm",
    )(x, w.reshape(1, d))


def _matmul_kernel(x_ref, w_ref, o_ref):
    o_ref[...] = jnp.dot(x_ref[...], w_ref[...].astype(BF16), preferred_element_type=F32)


def _matmul(x, w, layer, n_cols, tm, tn):
    m, k = x.shape
    return pl.pallas_call(
        _matmul_kernel,
        grid=(m // tm, n_cols // tn),
        in_specs=[pl.BlockSpec((tm, k), lambda i, j: (i, 0)),
                  pl.BlockSpec((None, k, tn), lambda i, j: (layer, 0, j))],
        out_specs=pl.BlockSpec((tm, tn), lambda i, j: (i, j)),
        out_shape=jax.ShapeDtypeStruct((m, n_cols), F32),
        compiler_params=_cparams("arbitrary", "arbitrary"),
        name="in_proj",
    )(x, w)


def _matmul_res_kernel(y_ref, w_ref, h_ref, o_ref):
    o_ref[...] = h_ref[...] + jnp.dot(y_ref[...], w_ref[...].astype(BF16), preferred_element_type=F32)


def _matmul_residual(y, w, layer, h, row_off, tm=1024, tn=512):
    rows, k = y.shape
    m, n = h.shape
    off = row_off // tm
    return pl.pallas_call(
        _matmul_res_kernel,
        grid=(rows // tm, n // tn),
        in_specs=[pl.BlockSpec((tm, k), lambda i, j: (i, 0)),
                  pl.BlockSpec((None, k, tn), lambda i, j: (layer, 0, j)),
                  pl.BlockSpec((tm, tn), lambda i, j: (i + off, j))],
        out_specs=pl.BlockSpec((tm, tn), lambda i, j: (i + off, j)),
        out_shape=jax.ShapeDtypeStruct((m, n), F32),
        input_output_aliases={2: 0},
        compiler_params=_cparams("arbitrary", "arbitrary"),
        name="out_proj",
    )(y, w, h)


def _gates_kernel(t_ref, alog_ref, dt_ref, o_ref, *, n_prompt_blocks, n_heads, dec_seq):
    i = pl.program_id(0)
    x = t_ref[...]
    lane = lax.broadcasted_iota(jnp.int32, x.shape, 1)
    row = lax.broadcasted_iota(jnp.int32, x.shape, 0)
    beta = jax.nn.sigmoid(x)
    a = x + dt_ref[...]
    softplus = jnp.maximum(a, 0.0) + jnp.log(1.0 + jnp.exp(-jnp.abs(a)))
    g = -jnp.exp(alog_ref[...]) * softplus
    seg = jnp.where(i < n_prompt_blocks, CHUNK, dec_seq)
    pos = row & (seg - 1)
    cs = g
    s = 1
    while s < CHUNK:
        cs = cs + jnp.where(pos >= s, pltpu.roll(cs, s, 0), 0.0)
        s *= 2
    o_ref[...] = jnp.where(lane < n_heads, beta, jnp.where(lane < 2 * n_heads, cs, 0.0))


def _gates(tail, a_log, dt_bias, n_prompt_rows, n_heads, dec_seq, tm=1024):
    m = tail.shape[0]
    pad = lambda v: jnp.zeros((1, LANES), F32).at[0, n_heads:2 * n_heads].set(v.astype(F32))
    kern = functools.partial(_gates_kernel, n_prompt_blocks=n_prompt_rows // tm, n_heads=n_heads,
                             dec_seq=dec_seq)
    return pl.pallas_call(
        kern,
        grid=(m // tm,),
        in_specs=[pl.BlockSpec((tm, LANES), lambda i: (i, 0)),
                  pl.BlockSpec((1, LANES), lambda i: (0, 0)),
                  pl.BlockSpec((1, LANES), lambda i: (0, 0))],
        out_specs=pl.BlockSpec((tm, LANES), lambda i: (i, 0)),
        out_shape=jax.ShapeDtypeStruct((m, LANES), F32),
        compiler_params=_cparams("arbitrary"),
        name="delta_gates",
    )(tail, pad(a_log), pad(dt_bias))


def _blockdiag(y):
    c = y.shape[0]
    yb = y.astype(BF16)
    lane = lax.broadcasted_iota(jnp.int32, yb.shape, 1)
    zero = jnp.zeros_like(yb)
    return jnp.concatenate([jnp.where(lane < c, yb, zero), jnp.where(lane >= c, yb, zero)], axis=0)


def _blockdiag_wide(r):
    rb = r.astype(BF16)
    z = jnp.zeros_like(rb[:, :HEAD_DIM])
    return jnp.concatenate([jnp.concatenate([rb[:, :HEAD_DIM], z], axis=1),
                            jnp.concatenate([z, rb[:, HEAD_DIM:]], axis=1)], axis=0)


def _unit_lower_inverse_minus_identity(l_cat, seg):
    c = l_cat.shape[0]
    p = -l_cat
    q = jnp.dot(l_cat.astype(BF16), _blockdiag(l_cat), preferred_element_type=F32)
    n_steps = seg.bit_length() - 2
    for step in range(n_steps):
        if step < n_steps - 1:
            pq = jnp.dot(jnp.concatenate([p, q], axis=0).astype(BF16), _blockdiag(q),
                         preferred_element_type=F32)
            p = p + q + pq[:c]
            q = pq[c:]
        else:
            p = p + q + jnp.dot(p.astype(BF16), _blockdiag(q), preferred_element_type=F32)
    return p


def _pair_matrices(kn, qn, beta_cat, gc_cat, gcr_cat, seg):
    c = kn.shape[0]
    kq = jnp.concatenate([kn, qn], axis=0)
    kk = jnp.concatenate([kn, kn], axis=0)
    ab = lax.dot_general(kq, kk, (((1,), (1,)), ((), ())), preferred_element_type=F32)
    ii = lax.broadcasted_iota(jnp.int32, (c, 2 * c), 0)
    jj = lax.broadcasted_iota(jnp.int32, (c, 2 * c), 1) & (c - 1)
    same = (ii >= jj) if seg == c else ((ii >= jj) & ((ii // seg) == (jj // seg)))
    dec = jnp.exp(jnp.where(same, gc_cat - gcr_cat, NEG_BIG))
    l_cat = jnp.where(ii > jj, beta_cat * ab[:c] * dec, 0.0)
    return _unit_lower_inverse_minus_identity(l_cat, seg), ab[c:] * dec


def _head_cols(g_all, idx, width):
    c = g_all.shape[0]
    return jnp.concatenate([jnp.broadcast_to(g_all[:, idx:idx + 1], (c, width)),
                            jnp.broadcast_to(g_all[:, idx + 1:idx + 2], (c, width))], axis=1)


def _gated_out(o, z, onw):
    on = o * lax.rsqrt(jnp.mean(o * o, axis=-1, keepdims=True) + NORM_EPS) * onw
    return (on * _silu(z)).astype(BF16)


def _conv_taps(ext_ref, w_ref, start, n):
    lo = start - (CONV_WIDTH - 1)
    if len(ext_ref.shape) == 2:
        acc = w_ref[0:1, :] * ext_ref[lo:lo + n, :]
        for j in range(1, CONV_WIDTH):
            acc = acc + w_ref[j:j + 1, :] * ext_ref[lo + j:lo + j + n, :]
    else:
        acc = w_ref[0:1, :] * ext_ref[:, lo:lo + n, :]
        for j in range(1, CONV_WIDTH):
            acc = acc + w_ref[j:j + 1, :] * ext_ref[:, lo + j:lo + j + n, :]
    return _silu(acc)


def _l2norm(x):
    return x * lax.rsqrt(jnp.sum(x * x, axis=-1, keepdims=True) + NORM_EPS)


def _delta_prompt_kernel(q_ref, k_ref, v_ref, z_ref, gc_ref, gr_ref, wq_ref, wk_ref, wv_ref, onw_ref,
                         y_ref, s_ref, ccq_ref, cck_ref, ccv_ref,
                         eq_ref, ek_ref, ev_ref, qn_ref, kn_ref, vv_ref, gt_ref, p_ref, pi_ref,
                         *, n_pairs, n_chunks, n_heads):
    c = CHUNK
    rows = n_chunks * c
    pg = pl.program_id(1)
    cb = pl.program_id(2)
    hist = SUBLANES

    @pl.when(cb == 0)
    def _():
        eq_ref[0:hist, :] = jnp.zeros((hist, eq_ref.shape[1]), F32)
        ek_ref[0:hist, :] = jnp.zeros((hist, ek_ref.shape[1]), F32)
        ev_ref[0:hist, :] = jnp.zeros((hist, ev_ref.shape[1]), F32)
        s_ref[...] = jnp.zeros(s_ref.shape, F32)

    eq_ref[hist:hist + rows, :] = q_ref[...]
    ek_ref[hist:hist + rows, :] = k_ref[...]
    ev_ref[hist:hist + rows, :] = v_ref[...]
    xq = _conv_taps(eq_ref, wq_ref, hist, rows)
    xk = _conv_taps(ek_ref, wk_ref, hist, rows)
    vv_ref[...] = _conv_taps(ev_ref, wv_ref, hist, rows)
    for p in range(n_pairs):
        sl = slice(p * HEAD_DIM, (p + 1) * HEAD_DIM)
        qn_ref[:, sl] = (_l2norm(xq[:, sl]) * (HEAD_DIM ** -0.5)).astype(BF16)
        kn_ref[:, sl] = _l2norm(xk[:, sl]).astype(BF16)
    keep = CONV_WIDTH - 1
    ccq_ref[0] = eq_ref[hist + rows - keep:hist + rows, :]
    cck_ref[0] = ek_ref[hist + rows - keep:hist + rows, :]
    ccv_ref[0] = ev_ref[hist + rows - keep:hist + rows, :]
    eq_ref[0:hist, :] = eq_ref[rows:rows + hist, :]
    ek_ref[0:hist, :] = ek_ref[rows:rows + hist, :]
    ev_ref[0:hist, :] = ev_ref[rows:rows + hist, :]

    shift = lax.rem(LANES - 2 * n_pairs * pg, LANES)
    gt_ref[...] = pltpu.roll(gc_ref[...], shift, 1)

    lane_cat = lax.broadcasted_iota(jnp.int32, (c, 2 * c), 1)
    for ci in range(n_chunks):
        r0 = ci * c
        g_all = gt_ref[r0:r0 + c, :]
        for p in range(n_pairs):
            sl = slice(p * HEAD_DIM, (p + 1) * HEAD_DIM)
            beta_cat = jnp.where(lane_cat < c, g_all[:, 2 * p:2 * p + 1], g_all[:, 2 * p + 1:2 * p + 2])
            gc_cat = jnp.where(lane_cat < c, g_all[:, n_heads + 2 * p:n_heads + 2 * p + 1],
                               g_all[:, n_heads + 2 * p + 1:n_heads + 2 * p + 2])
            gcr_cat = gr_ref[0, 0, pl.ds(ci * (n_heads // 2) + n_pairs * pg + p, 1), :]
            pm, pi = _pair_matrices(kn_ref[r0:r0 + c, sl], qn_ref[r0:r0 + c, sl],
                                    beta_cat, gc_cat, gcr_cat, c)
            p_ref[ci, p] = pm.astype(BF16)
            pi_ref[ci, p] = pi.astype(BF16)

    lane_wide = lax.broadcasted_iota(jnp.int32, (1, 2 * HEAD_DIM), 1)

    def chunk_step(ci, carry):
        r0 = pl.multiple_of(ci * c, c)
        g_all = gt_ref[pl.ds(r0, c), :]
        for p in range(n_pairs):
            sl = slice(p * HEAD_DIM, (p + 1) * HEAD_DIM)
            sl2 = slice(2 * p * HEAD_DIM, (2 * p + 2) * HEAD_DIM)
            kn = kn_ref[pl.ds(r0, c), sl]
            qn = qn_ref[pl.ds(r0, c), sl]
            s_cat = jnp.concatenate([s_ref[0, 2 * p], s_ref[0, 2 * p + 1]], axis=1)
            kqs = jnp.dot(jnp.concatenate([kn, qn], axis=0), s_cat.astype(BF16),
                          preferred_element_type=F32)
            beta_w = _head_cols(g_all, 2 * p, HEAD_DIM)
            gc_w = _head_cols(g_all, n_heads + 2 * p, HEAD_DIM)
            egc_w = jnp.exp(gc_w)
            r = beta_w * (vv_ref[pl.ds(r0, c), sl2] - egc_w * kqs[:c])
            u = r + jnp.dot(p_ref[ci, p], _blockdiag_wide(r), preferred_element_type=F32)
            o = egc_w * kqs[c:] + jnp.dot(pi_ref[ci, p], _blockdiag_wide(u), preferred_element_type=F32)
            gcr_cat = gr_ref[0, 0, pl.ds(ci * (n_heads // 2) + n_pairs * pg + p, 1), :]
            gl_w = jnp.where(lane_wide < HEAD_DIM, gcr_cat[:, c - 1:c], gcr_cat[:, 2 * c - 1:2 * c])
            w = (jnp.exp(gl_w - gc_w) * u).astype(BF16)
            s_new = s_cat * jnp.exp(gl_w) + lax.dot_general(
                kn, w, (((0,), (0,)), ((), ())), preferred_element_type=F32)
            s_ref[0, 2 * p] = s_new[:, :HEAD_DIM]
            s_ref[0, 2 * p + 1] = s_new[:, HEAD_DIM:]
            for e in range(2):
                hs = slice((2 * p + e) * HEAD_DIM, (2 * p + e + 1) * HEAD_DIM)
                y_ref[pl.ds(r0, c), hs] = _gated_out(o[:, e * HEAD_DIM:(e + 1) * HEAD_DIM],
                                                     z_ref[pl.ds(r0, c), hs], onw_ref[...])
        return carry

    lax.fori_loop(0, n_chunks, chunk_step, 0)


def _delta_prompt(proj, gates, gates_rows, conv_w, o_norm_w, layer, batch, seq, n_heads, d_inner,
                  n_pairs=2, n_chunks=CHUNKS_PER_STEP):
    qk_dim = n_heads // 2 * HEAD_DIM
    rows = n_chunks * CHUNK
    nrb = seq // rows
    wq, wv = n_pairs * HEAD_DIM, 2 * n_pairs * HEAD_DIM
    n_pg = n_heads // (2 * n_pairs)
    kern = functools.partial(_delta_prompt_kernel, n_pairs=n_pairs, n_chunks=n_chunks, n_heads=n_heads)
    row_map = lambda col0: (lambda b, g, r: (b * nrb + r, col0 + g))
    keep = CONV_WIDTH - 1
    out_shapes = (
        jax.ShapeDtypeStruct((batch * seq, d_inner), BF16),
        jax.ShapeDtypeStruct((batch, n_heads, HEAD_DIM, HEAD_DIM), F32),
        jax.ShapeDtypeStruct((batch, keep, qk_dim), F32),
        jax.ShapeDtypeStruct((batch, keep, qk_dim), F32),
        jax.ShapeDtypeStruct((batch, keep, d_inner), F32),
    )
    return pl.pallas_call(
        kern,
        grid=(batch, n_pg, nrb),
        in_specs=[
            pl.BlockSpec((rows, wq), row_map(0)),
            pl.BlockSpec((rows, wq), row_map(qk_dim // wq)),
            pl.BlockSpec((rows, wv), row_map(2 * qk_dim // wv)),
            pl.BlockSpec((rows, wv), row_map((2 * qk_dim + d_inner) // wv)),
            pl.BlockSpec((rows, LANES), lambda b, g, r: (b * nrb + r, 0)),
            pl.BlockSpec((1, 1, n_chunks * (n_heads // 2), 2 * CHUNK), lambda b, g, r: (b, r, 0, 0)),
            pl.BlockSpec((None, CONV_WIDTH, wq), lambda b, g, r: (layer, 0, g)),
            pl.BlockSpec((None, CONV_WIDTH, wq), lambda b, g, r: (layer, 0, qk_dim // wq + g)),
            pl.BlockSpec((None, CONV_WIDTH, wv), lambda b, g, r: (layer, 0, 2 * qk_dim // wv + g)),
            pl.BlockSpec((None, 1, HEAD_DIM), lambda b, g, r: (layer, 0, 0)),
        ],
        out_specs=(
            pl.BlockSpec((rows, wv), lambda b, g, r: (b * nrb + r, g)),
            pl.BlockSpec((1, 2 * n_pairs, HEAD_DIM, HEAD_DIM), lambda b, g, r: (b, g, 0, 0)),
            pl.BlockSpec((1, keep, wq), lambda b, g, r: (b, 0, g)),
            pl.BlockSpec((1, keep, wq), lambda b, g, r: (b, 0, g)),
            pl.BlockSpec((1, keep, wv), lambda b, g, r: (b, 0, g)),
        ),
        out_shape=out_shapes,
        scratch_shapes=[
            pltpu.VMEM((SUBLANES + rows, wq), F32),
            pltpu.VMEM((SUBLANES + rows, wq), F32),
            pltpu.VMEM((SUBLANES + rows, wv), F32),
            pltpu.VMEM((rows, wq), BF16),
            pltpu.VMEM((rows, wq), BF16),
            pltpu.VMEM((rows, wv), F32),
            pltpu.VMEM((rows, LANES), F32),
            pltpu.VMEM((n_chunks, n_pairs, CHUNK, 2 * CHUNK), BF16),
            pltpu.VMEM((n_chunks, n_pairs, CHUNK, 2 * CHUNK), BF16),
        ],
        compiler_params=_cparams("arbitrary", "arbitrary", "arbitrary"),
        name="delta_prompt",
    )(proj, proj, proj, proj, gates, gates_rows, conv_w, conv_w, conv_w, o_norm_w)


def _delta_sample_kernel(*refs, n_pairs, n_seqs, dec_seq, n_heads, aliased):
    (q_ref, k_ref, v_ref, z_ref, cq_ref, ck_ref, cv_ref, gc_ref, gr_ref, wq_ref, wk_ref, wv_ref,
     onw_ref, s_in_ref) = refs[:14]
    refs = refs[15:] if aliased else refs[14:]
    y_ref, s_out_ref, ccq_ref, cck_ref, ccv_ref, eq_ref, ek_ref, ev_ref = refs
    c = n_seqs * dec_seq
    pg = pl.program_id(1)
    hist = SUBLANES

    def conv(x_ref, cache_ref, ext_ref, w_ref, cc_ref):
        cols = ext_ref.shape[2]
        ext_ref[:, 0:hist, :] = cache_ref[...]
        ext_ref[:, hist:hist + dec_seq, :] = x_ref[...].reshape(n_seqs, dec_seq, cols)
        keep = CONV_WIDTH - 1
        cc_ref[...] = ext_ref[:, hist + dec_seq - keep:hist + dec_seq, :]
        return _conv_taps(ext_ref, w_ref, hist, dec_seq).reshape(c, cols)

    xq = conv(q_ref, cq_ref, eq_ref, wq_ref, ccq_ref)
    xk = conv(k_ref, ck_ref, ek_ref, wk_ref, cck_ref)
    xv = conv(v_ref, cv_ref, ev_ref, wv_ref, ccv_ref)

    shift = lax.rem(LANES - 2 * n_pairs * pg, LANES)
    g_all = pltpu.roll(gc_ref[...], shift, 1)
    lane_cat = lax.broadcasted_iota(jnp.int32, (c, 2 * c), 1)
    lane_wide = lax.broadcasted_iota(jnp.int32, (1, 2 * HEAD_DIM), 1)
    row_seq = lax.broadcasted_iota(jnp.int32, (c, 2 * HEAD_DIM), 0) // dec_seq

    for p in range(n_pairs):
        sl = slice(p * HEAD_DIM, (p + 1) * HEAD_DIM)
        sl2 = slice(2 * p * HEAD_DIM, (2 * p + 2) * HEAD_DIM)
        qn = _l2norm(xq[:, sl]) * (HEAD_DIM ** -0.5)
        kn = _l2norm(xk[:, sl])
        knb = kn.astype(BF16)
        beta_cat = jnp.where(lane_cat < c, g_all[:, 2 * p:2 * p + 1], g_all[:, 2 * p + 1:2 * p + 2])
        gc_cat = jnp.where(lane_cat < c, g_all[:, n_heads + 2 * p:n_heads + 2 * p + 1],
                           g_all[:, n_heads + 2 * p + 1:n_heads + 2 * p + 2])
        gcr_cat = gr_ref[0, pl.ds(n_pairs * pg + p, 1), :]
        pm, pi = _pair_matrices(knb, qn.astype(BF16), beta_cat, gc_cat, gcr_cat, dec_seq)

        s_cats, ks, qs, gls = [], [], [], []
        for s in range(n_seqs):
            rs = slice(s * dec_seq, (s + 1) * dec_seq)
            s_cat = jnp.concatenate([s_in_ref[s, 2 * p], s_in_ref[s, 2 * p + 1]], axis=1)
            kq = jnp.concatenate([kn[rs], qn[rs]], axis=0).astype(BF16)
            kqs = jnp.dot(kq, s_cat.astype(BF16), preferred_element_type=F32)
            s_cats.append(s_cat)
            ks.append(kqs[:dec_seq])
            qs.append(kqs[dec_seq:])
            last = (s + 1) * dec_seq - 1
            gls.append(jnp.where(lane_wide < HEAD_DIM, gcr_cat[:, last:last + 1],
                                 gcr_cat[:, c + last:c + last + 1]))
        k_s = jnp.concatenate(ks, axis=0)
        q_s = jnp.concatenate(qs, axis=0)
        gl_rows = jnp.concatenate([jnp.broadcast_to(g, (dec_seq, 2 * HEAD_DIM)) for g in gls], axis=0)

        beta_w = _head_cols(g_all, 2 * p, HEAD_DIM)
        gc_w = _head_cols(g_all, n_heads + 2 * p, HEAD_DIM)
        egc_w = jnp.exp(gc_w)
        r = beta_w * (xv[:, sl2] - egc_w * k_s)
        u = r + jnp.dot(pm.astype(BF16), _blockdiag_wide(r), preferred_element_type=F32)
        o = egc_w * q_s + jnp.dot(pi.astype(BF16), _blockdiag_wide(u), preferred_element_type=F32)
        w = jnp.exp(gl_rows - gc_w) * u
        for s in range(n_seqs):
            w_s = jnp.where(row_seq == s, w, 0.0).astype(BF16)
            s_new = s_cats[s] * jnp.exp(gls[s]) + lax.dot_general(
                knb, w_s, (((0,), (0,)), ((), ())), preferred_element_type=F32)
            s_out_ref[s, 2 * p] = s_new[:, :HEAD_DIM]
            s_out_ref[s, 2 * p + 1] = s_new[:, HEAD_DIM:]
        for e in range(2):
            hs = slice((2 * p + e) * HEAD_DIM, (2 * p + e + 1) * HEAD_DIM)
            y_ref[:, hs] = _gated_out(o[:, e * HEAD_DIM:(e + 1) * HEAD_DIM], z_ref[:, hs], onw_ref[...])


def _delta_sample(proj, gates, gates_rows, conv_cache_pad, state, prev_state_out, conv_w, o_norm_w, layer,
                  row_off, batch, dec_seq, n_heads, d_inner, n_pairs=2, n_seqs=8):
    qk_dim = n_heads // 2 * HEAD_DIM
    c = n_seqs * dec_seq
    wq, wv = n_pairs * HEAD_DIM, 2 * n_pairs * HEAD_DIM
    n_pg = n_heads // (2 * n_pairs)
    rb0 = row_off // c
    aliased = prev_state_out is not None
    kern = functools.partial(_delta_sample_kernel, n_pairs=n_pairs, n_seqs=n_seqs, dec_seq=dec_seq,
                             n_heads=n_heads, aliased=aliased)
    row_map = lambda col0: (lambda i, g: (rb0 + i, col0 + g))
    keep = CONV_WIDTH - 1
    in_specs = [
        pl.BlockSpec((c, wq), row_map(0)),
        pl.BlockSpec((c, wq), row_map(qk_dim // wq)),
        pl.BlockSpec((c, wv), row_map(2 * qk_dim // wv)),
        pl.BlockSpec((c, wv), row_map((2 * qk_dim + d_inner) // wv)),
        pl.BlockSpec((n_seqs, SUBLANES, wq), lambda i, g: (i, 0, g)),
        pl.BlockSpec((n_seqs, SUBLANES, wq), lambda i, g: (i, 0, qk_dim // wq + g)),
        pl.BlockSpec((n_seqs, SUBLANES, wv), lambda i, g: (i, 0, 2 * qk_dim // wv + g)),
        pl.BlockSpec((c, LANES), lambda i, g: (rb0 + i, 0)),
        pl.BlockSpec((1, n_heads // 2, 2 * c), lambda i, g: (i, 0, 0)),
        pl.BlockSpec((None, CONV_WIDTH, wq), lambda i, g: (layer, 0, g)),
        pl.BlockSpec((None, CONV_WIDTH, wq), lambda i, g: (layer, 0, qk_dim // wq + g)),
        pl.BlockSpec((None, CONV_WIDTH, wv), lambda i, g: (layer, 0, 2 * qk_dim // wv + g)),
        pl.BlockSpec((None, 1, HEAD_DIM), lambda i, g: (layer, 0, 0)),
        pl.BlockSpec((None, n_seqs, 2 * n_pairs, HEAD_DIM, HEAD_DIM), lambda i, g: (layer, i, g, 0, 0)),
    ]
    args = [proj, proj, proj, proj, conv_cache_pad, conv_cache_pad, conv_cache_pad, gates, gates_rows,
            conv_w, conv_w, conv_w, o_norm_w, state]
    aliases = {}
    if aliased:
        in_specs.append(pl.BlockSpec(memory_space=pl.ANY))
        args.append(prev_state_out)
        aliases = {14: 1}
    out_shapes = (
        jax.ShapeDtypeStruct((batch * dec_seq, d_inner), BF16),
        jax.ShapeDtypeStruct(state.shape, F32),
        jax.ShapeDtypeStruct((batch, keep, qk_dim), F32),
        jax.ShapeDtypeStruct((batch, keep, qk_dim), F32),
        jax.ShapeDtypeStruct((batch, keep, d_inner), F32),
    )
    return pl.pallas_call(
        kern,
        grid=(batch // n_seqs, n_pg),
        in_specs=in_specs,
        out_specs=(
            pl.BlockSpec((c, wv), lambda i, g: (i, g)),
            pl.BlockSpec((None, n_seqs, 2 * n_pairs, HEAD_DIM, HEAD_DIM), lambda i, g: (layer, i, g, 0, 0)),
            pl.BlockSpec((n_seqs, keep, wq), lambda i, g: (i, 0, g)),
            pl.BlockSpec((n_seqs, keep, wq), lambda i, g: (i, 0, g)),
            pl.BlockSpec((n_seqs, keep, wv), lambda i, g: (i, 0, g)),
        ),
        out_shape=out_shapes,
        scratch_shapes=[
            pltpu.VMEM((n_seqs, SUBLANES + dec_seq, wq), F32),
            pltpu.VMEM((n_seqs, SUBLANES + dec_seq, wq), F32),
            pltpu.VMEM((n_seqs, SUBLANES + dec_seq, wv), F32),
        ],
        input_output_aliases=aliases,
        compiler_params=_cparams("arbitrary", "arbitrary"),
        name="delta_sample",
    )(*args)


def _pool_finish(pooled, z, wg_ref, sc_ref):
    mixed = jnp.dot(pooled.astype(BF16), wg_ref[...], preferred_element_type=F32)
    return (mixed * sc_ref[...] * _silu(z)).astype(BF16)


def _pool_prompt_kernel(u_ref, prev_ref, z_ref, wg_ref, sc_ref, y_ref, cache_ref, ext_ref, *, tt, nt):
    g = pl.program_id(0)
    t = pl.program_id(2)
    hist = POOL_BUF + 1

    @pl.when(t == 0)
    def _():
        ext_ref[0:hist, :] = jnp.zeros((hist, ext_ref.shape[1]), F32)

    @pl.when(t > 0)
    def _():
        ext_ref[0:hist, :] = prev_ref[...]

    ext_ref[hist:hist + tt, :] = u_ref[...]
    pos = t * tt + lax.broadcasted_iota(jnp.int32, (tt, 1), 0)
    for gi, win in enumerate(POOL_WINDOWS):
        @pl.when(g == gi)
        def _(win=win):
            x = ext_ref[hist:hist + tt, :]
            acc = x
            for d in range(1, win):
                acc = acc + ext_ref[hist - d:hist - d + tt, :]
            cnt = jnp.minimum(pos + 1, win).astype(F32)
            y_ref[...] = _pool_finish(acc / cnt - x, z_ref[...], wg_ref, sc_ref)

    @pl.when(t == nt - 1)
    def _():
        cache_ref[0] = ext_ref[hist + tt - POOL_BUF:hist + tt, :]


def _pool_prompt(proj, w_group, scale, layer, batch, seq, d_inner, tt=512):
    gd = d_inner // len(POOL_WINDOWS)
    nt = seq // tt
    hist = POOL_BUF + 1
    kern = functools.partial(_pool_prompt_kernel, tt=tt, nt=nt)
    n_g = len(POOL_WINDOWS)
    return pl.pallas_call(
        kern,
        grid=(n_g, batch, nt),
        in_specs=[
            pl.BlockSpec((tt, gd), lambda g, b, t: (b * nt + t, g)),
            pl.BlockSpec((hist, gd), lambda g, b, t: (jnp.maximum((b * nt + t) * (tt // hist) - 1, 0), g)),
            pl.BlockSpec((tt, gd), lambda g, b, t: (b * nt + t, n_g + g)),
            pl.BlockSpec((None, None, gd, gd), lambda g, b, t: (layer, g, 0, 0)),
            pl.BlockSpec((None, 1, gd), lambda g, b, t: (layer, 0, g)),
        ],
        out_specs=(
            pl.BlockSpec((tt, gd), lambda g, b, t: (b * nt + t, g)),
            pl.BlockSpec((1, POOL_BUF, gd), lambda g, b, t: (b, 0, g)),
        ),
        out_shape=(
            jax.ShapeDtypeStruct((batch * seq, d_inner), BF16),
            jax.ShapeDtypeStruct((batch, POOL_BUF, d_inner), F32),
        ),
        scratch_shapes=[pltpu.VMEM((hist + tt, gd), F32)],
        compiler_params=_cparams("arbitrary", "arbitrary", "arbitrary"),
        name="pool_prompt",
    )(proj, proj, proj, w_group, scale)


def _pool_sample_kernel(u_ref, c_ref, z_ref, wg_ref, sc_ref, y_ref, cache_ref, ext_ref, *, sb, dec_seq):
    g = pl.program_id(0)
    hist = POOL_BUF + 1
    gd = ext_ref.shape[2]
    ext_ref[:, hist - POOL_BUF:hist, :] = c_ref[...]
    ext_ref[:, hist:hist + dec_seq, :] = u_ref[...].reshape(sb, dec_seq, gd)
    pos = PAST_LEN + lax.broadcasted_iota(jnp.int32, (1, dec_seq, 1), 1)
    for gi, win in enumerate(POOL_WINDOWS):
        @pl.when(g == gi)
        def _(win=win):
            x = ext_ref[:, hist:hist + dec_seq, :]
            acc = x
            for d in range(1, win):
                acc = acc + ext_ref[:, hist - d:hist - d + dec_seq, :]
            cnt = jnp.minimum(pos + 1, win).astype(F32)
            pooled = (acc / cnt - x).reshape(sb * dec_seq, gd)
            y_ref[...] = _pool_finish(pooled, z_ref[...], wg_ref, sc_ref)

    cache_ref[...] = ext_ref[:, hist + dec_seq - POOL_BUF:hist + dec_seq, :]


def _pool_sample(proj, cache, w_group, scale, layer, row_off, batch, dec_seq, d_inner, sb=16):
    gd = d_inner // len(POOL_WINDOWS)
    n_g = len(POOL_WINDOWS)
    rows = sb * dec_seq
    rb0 = row_off // rows
    hist = POOL_BUF + 1
    kern = functools.partial(_pool_sample_kernel, sb=sb, dec_seq=dec_seq)
    return pl.pallas_call(
        kern,
        grid=(n_g, batch // sb),
        in_specs=[
            pl.BlockSpec((rows, gd), lambda g, i: (rb0 + i, g)),
            pl.BlockSpec((None, sb, POOL_BUF, gd), lambda g, i: (layer, i, 0, g)),
            pl.BlockSpec((rows, gd), lambda g, i: (rb0 + i, n_g + g)),
            pl.BlockSpec((None, None, gd, gd), lambda g, i: (layer, g, 0, 0)),
            pl.BlockSpec((None, 1, gd), lambda g, i: (layer, 0, g)),
        ],
        out_specs=(
            pl.BlockSpec((rows, gd), lambda g, i: (i, g)),
            pl.BlockSpec((sb, POOL_BUF, gd), lambda g, i: (i, 0, g)),
        ),
        out_shape=(
            jax.ShapeDtypeStruct((batch * dec_seq, d_inner), BF16),
            jax.ShapeDtypeStruct((batch, POOL_BUF, d_inner), F32),
        ),
        scratch_shapes=[pltpu.VMEM((sb, hist + dec_seq, gd), F32)],
        compiler_params=_cparams("arbitrary", "arbitrary"),
        name="pool_sample",
    )(proj, cache, proj, w_group, scale)


def kernel(x_prompt, x_sample, state_delta, cache_conv, cache_pool, norm_w, final_norm_w, delta_w_in,
           delta_conv_w, delta_a_log, delta_dt_bias, delta_o_norm_w, delta_w_out, pool_w_in, pool_w_group,
           pool_scale, pool_w_out):
    batch, seq, d_model = x_prompt.shape
    dec_batch, dec_seq, _ = x_sample.shape
    depth = norm_w.shape[0]
    n_heads = state_delta.shape[2]
    d_inner = n_heads * HEAD_DIM
    conv_dim = cache_conv.shape[3]
    qk_dim = (conv_dim - d_inner) // 2
    mp, ms = batch * seq, dec_batch * dec_seq
    m = mp + ms
    n_main = conv_dim + d_inner
    n_seqs = CHUNK // dec_seq

    h = jnp.concatenate([x_prompt.reshape(mp, d_model), x_sample.reshape(ms, d_model)], axis=0)

    delta_w_out_b = delta_w_out.astype(BF16)
    pool_w_out_b = pool_w_out.astype(BF16)
    pool_w_group_b = pool_w_group.astype(BF16)
    w_tail = jnp.pad(delta_w_in[:, :, n_main:], ((0, 0), (0, 0), (0, LANES - 2 * n_heads)))
    onw = delta_o_norm_w.reshape(-1, 1, HEAD_DIM)
    pool_scale3 = pool_scale.reshape(pool_scale.shape[0], 1, d_inner)
    conv_cache_pad = jnp.pad(cache_conv, ((0, 0), (0, 0), (SUBLANES - (CONV_WIDTH - 1), 0), (0, 0)))

    sd_p, cc_p, cp_p, cc_s, cp_s = [], [], [], [], []
    sd_s = None
    for i in range(depth):
        j = i // 2
        xn = _rmsnorm(h, norm_w[i], BF16, m)
        if i % 2 == 0:
            proj = _matmul(xn, delta_w_in, j, n_main, tm=3072, tn=512)
            tail = _matmul(xn, w_tail, j, LANES, tm=3072, tn=LANES)
            gates = _gates(tail, delta_a_log[j], delta_dt_bias[j], mp, n_heads, dec_seq)
            gcum = gates[:, n_heads:2 * n_heads]
            rows_p = gcum[:mp].reshape(batch, seq // CHUNK, CHUNK, n_heads // 2, 2)
            rows_p = rows_p.transpose(0, 1, 3, 4, 2).reshape(batch, seq // (CHUNKS_PER_STEP * CHUNK), CHUNKS_PER_STEP * (n_heads // 2), 2 * CHUNK)
            rows_s = gcum[mp:].reshape(ms // CHUNK, CHUNK, n_heads // 2, 2)
            rows_s = rows_s.transpose(0, 2, 3, 1).reshape(ms // CHUNK, n_heads // 2, 2 * CHUNK)
            y_p, sp, cq, ck, cv = _delta_prompt(proj, gates, rows_p, delta_conv_w, onw, j, batch, seq,
                                                n_heads, d_inner)
            y_s, sd_s, cqs, cks, cvs = _delta_sample(proj, gates, rows_s, conv_cache_pad[j], state_delta, sd_s,
                                                     delta_conv_w, onw, j, mp, dec_batch, dec_seq, n_heads,
                                                     d_inner, n_seqs=n_seqs)
            sd_p.append(sp)
            cc_p.append(jnp.concatenate([cq, ck, cv], axis=-1))
            cc_s.append(jnp.concatenate([cqs, cks, cvs], axis=-1))
            w_out, lw = delta_w_out_b, j
        else:
            proj = _matmul(xn, pool_w_in, j, 2 * d_inner, tm=3072, tn=512)
            y_p, pp = _pool_prompt(proj, pool_w_group_b, pool_scale3, j, batch, seq, d_inner)
            y_s, ps = _pool_sample(proj, cache_pool, pool_w_group_b, pool_scale3, j, mp, dec_batch, dec_seq,
                                   d_inner)
            cp_p.append(pp)
            cp_s.append(ps)
            w_out, lw = pool_w_out_b, j
        h = _matmul_residual(y_p, w_out, lw, h, 0)
        h = _matmul_residual(y_s, w_out, lw, h, mp)

    y_prompt = _rmsnorm(h, final_norm_w, F32, mp, 0).reshape(batch, seq, d_model)
    y_sample = _rmsnorm(h, final_norm_w, F32, ms, mp).reshape(dec_batch, dec_seq, d_model)
    return (y_prompt, y_sample, jnp.stack(sd_p), jnp.stack(cc_p), jnp.stack(cp_p),
            sd_s, jnp.stack(cc_s), jnp.stack(cp_s))
```

```python
import functools

import jax
import jax.numpy as jnp
from jax import lax
from jax.experimental import pallas as pl
from jax.experimental.pallas import tpu as pltpu

F32 = jnp.float32
BF16 = jnp.bfloat16

HEAD_DIM = 128
CONV_WIDTH = 4
CHUNK = 64
CHUNKS_PER_STEP = 4
POOL_WINDOWS = (2, 4, 8, 16)
POOL_BUF = max(POOL_WINDOWS) - 1
PAST_LEN = 16384
NORM_EPS = 1e-6
NEG_BIG = -1e30

LANES = 128
SUBLANES = 8
VMEM_LIMIT = 56 * 1024 * 1024


def _cparams(*sem):
    return pltpu.CompilerParams(dimension_semantics=sem, vmem_limit_bytes=VMEM_LIMIT)


def _silu(x):
    return x * jax.nn.sigmoid(x)


def _rmsnorm_kernel(x_ref, w_ref, o_ref):
    x = x_ref[...]
    y = x * lax.rsqrt(jnp.mean(x * x, axis=-1, keepdims=True) + NORM_EPS)
    o_ref[...] = (y * w_ref[...]).astype(o_ref.dtype)


def _rmsnorm(x, w, out_dtype, rows, row_off=0, tm=512):
    d = x.shape[1]
    off = row_off // tm
    return pl.pallas_call(
        _rmsnorm_kernel,
        grid=(rows // tm,),
        in_specs=[pl.BlockSpec((tm, d), lambda i: (i + off, 0)),
                  pl.BlockSpec((1, d), lambda i: (0, 0))],
        out_specs=pl.BlockSpec((tm, d), lambda i: (i, 0)),
        out_shape=jax.ShapeDtypeStruct((rows, d), out_dtype),
        compiler_params=_cparams("arbitrary"),
        name="rmsnorm",
    )(x, w.reshape(1, d))


def _matmul_kernel(x_ref, w_ref, o_ref):
    o_ref[...] = jnp.dot(x_ref[...], w_ref[...].astype(BF16), preferred_element_type=F32)


def _matmul(x, w, layer, n_cols, tm, tn):
    m, k = x.shape
    return pl.pallas_call(
        _matmul_kernel,
        grid=(m // tm, n_cols // tn),
        in_specs=[pl.BlockSpec((tm, k), lambda i, j: (i, 0)),
                  pl.BlockSpec((None, k, tn), lambda i, j: (layer, 0, j))],
        out_specs=pl.BlockSpec((tm, tn), lambda i, j: (i, j)),
        out_shape=jax.ShapeDtypeStruct((m, n_cols), F32),
        compiler_params=_cparams("arbitrary", "arbitrary"),
        name="in_proj",
    )(x, w)


def _matmul_res_kernel(y_ref, w_ref, h_ref, o_ref):
    o_ref[...] = h_ref[...] + jnp.dot(y_ref[...], w_ref[...].astype(BF16), preferred_element_type=F32)


def _matmul_residual(y, w, layer, h, row_off, tm=1024, tn=512):
    rows, k = y.shape
    m, n = h.shape
    off = row_off // tm
    return pl.pallas_call(
        _matmul_res_kernel,
        grid=(rows // tm, n // tn),
        in_specs=[pl.BlockSpec((tm, k), lambda i, j: (i, 0)),
                  pl.BlockSpec((None, k, tn), lambda i, j: (layer, 0, j)),
                  pl.BlockSpec((tm, tn), lambda i, j: (i + off, j))],
        out_specs=pl.BlockSpec((tm, tn), lambda i, j: (i + off, j)),
        out_shape=jax.ShapeDtypeStruct((m, n), F32),
        input_output_aliases={2: 0},
        compiler_params=_cparams("arbitrary", "arbitrary"),
        name="out_proj",
    )(y, w, h)


def _gates_kernel(t_ref, alog_ref, dt_ref, o_ref, *, n_prompt_blocks, n_heads, dec_seq):
    i = pl.program_id(0)
    x = t_ref[...]
    lane = lax.broadcasted_iota(jnp.int32, x.shape, 1)
    row = lax.broadcasted_iota(jnp.int32, x.shape, 0)
    beta = jax.nn.sigmoid(x)
    a = x + dt_ref[...]
    softplus = jnp.maximum(a, 0.0) + jnp.log(1.0 + jnp.exp(-jnp.abs(a)))
    g = -jnp.exp(alog_ref[...]) * softplus
    seg = jnp.where(i < n_prompt_blocks, CHUNK, dec_seq)
    pos = row & (seg - 1)
    cs = g
    s = 1
    while s < CHUNK:
        cs = cs + jnp.where(pos >= s, pltpu.roll(cs, s, 0), 0.0)
        s *= 2
    o_ref[...] = jnp.where(lane < n_heads, beta, jnp.where(lane < 2 * n_heads, cs, 0.0))


def _gates(tail, a_log, dt_bias, n_prompt_rows, n_heads, dec_seq, tm=1024):
    m = tail.shape[0]
    pad = lambda v: jnp.zeros((1, LANES), F32).at[0, n_heads:2 * n_heads].set(v.astype(F32))
    kern = functools.partial(_gates_kernel, n_prompt_blocks=n_prompt_rows // tm, n_heads=n_heads,
                             dec_seq=dec_seq)
    return pl.pallas_call(
        kern,
        grid=(m // tm,),
        in_specs=[pl.BlockSpec((tm, LANES), lambda i: (i, 0)),
                  pl.BlockSpec((1, LANES), lambda i: (0, 0)),
                  pl.BlockSpec((1, LANES), lambda i: (0, 0))],
        out_specs=pl.BlockSpec((tm, LANES), lambda i: (i, 0)),
        out_shape=jax.ShapeDtypeStruct((m, LANES), F32),
        compiler_params=_cparams("arbitrary"),
        name="delta_gates",
    )(tail, pad(a_log), pad(dt_bias))


QUAD = 4


def _blockdiag(y, masks):
    yb = y.astype(BF16)
    zero = jnp.zeros_like(yb)
    return jnp.concatenate([jnp.where(m, yb, zero) for m in masks], axis=0)


def _blockdiag_wide(r):
    rb = r.astype(BF16)
    z = jnp.zeros_like(rb[:, :HEAD_DIM])
    return jnp.concatenate([jnp.concatenate([rb[:, :HEAD_DIM], z], axis=1),
                            jnp.concatenate([z, rb[:, HEAD_DIM:]], axis=1)], axis=0)


def _unit_lower_inverses_minus_identity(l_cats, seg, masks):
    c = l_cats[0].shape[0]
    dot = lambda a, b: jnp.dot(a.astype(BF16), _blockdiag(b, masks), preferred_element_type=F32)
    ps = [-l for l in l_cats]
    qs = [dot(l, l) for l in l_cats]
    n_steps = seg.bit_length() - 2
    for step in range(n_steps):
        if step < n_steps - 1:
            pqs = [dot(jnp.concatenate([p, q], axis=0), q) for p, q in zip(ps, qs)]
            ps = [p + q + pq[:c] for p, q, pq in zip(ps, qs, pqs)]
            qs = [pq[c:] for pq in pqs]
        else:
            pqs = [dot(p, q) for p, q in zip(ps, qs)]
            ps = [p + q + pq for p, q, pq in zip(ps, qs, pqs)]
    return ps


def _gram(kn, qn):
    kq = jnp.concatenate([kn, qn], axis=0)
    kk = jnp.concatenate([kn, kn], axis=0)
    return lax.dot_general(kq, kk, (((1,), (1,)), ((), ())), preferred_element_type=F32)


def _quad_masks(c, seg):
    lane = lax.broadcasted_iota(jnp.int32, (c, QUAD * c), 1)
    ii = lax.broadcasted_iota(jnp.int32, (c, QUAD * c), 0)
    jj = lane & (c - 1)
    groups = [(lane // c) == g for g in range(QUAD)]
    incl = (ii >= jj) if seg == c else ((ii >= jj) & ((ii // seg) == (jj // seg)))
    return groups, incl, ii > jj


def _quad_select(g_all, base, groups):
    out = g_all[:, base + QUAD - 1:base + QUAD]
    for g in range(QUAD - 2, -1, -1):
        out = jnp.where(groups[g], g_all[:, base + g:base + g + 1], out)
    return out


def _head_cols(g_all, idx, width):
    c = g_all.shape[0]
    return jnp.concatenate([jnp.broadcast_to(g_all[:, idx:idx + 1], (c, width)),
                            jnp.broadcast_to(g_all[:, idx + 1:idx + 2], (c, width))], axis=1)


def _gated_out(o, z, onw):
    on = o * lax.rsqrt(jnp.mean(o * o, axis=-1, keepdims=True) + NORM_EPS) * onw
    return (on * _silu(z)).astype(BF16)


def _conv_taps(ext_ref, w_ref, start, n):
    lo = start - (CONV_WIDTH - 1)
    if len(ext_ref.shape) == 2:
        acc = w_ref[0:1, :] * ext_ref[lo:lo + n, :]
        for j in range(1, CONV_WIDTH):
            acc = acc + w_ref[j:j + 1, :] * ext_ref[lo + j:lo + j + n, :]
    else:
        acc = w_ref[0:1, :] * ext_ref[:, lo:lo + n, :]
        for j in range(1, CONV_WIDTH):
            acc = acc + w_ref[j:j + 1, :] * ext_ref[:, lo + j:lo + j + n, :]
    return _silu(acc)


def _l2norm(x):
    return x * lax.rsqrt(jnp.sum(x * x, axis=-1, keepdims=True) + NORM_EPS)


def _delta_prompt_kernel(q_ref, k_ref, v_ref, z_ref, gc_ref, gr_ref, wq_ref, wk_ref, wv_ref, onw_ref,
                         y_ref, s_ref, ccq_ref, cck_ref, ccv_ref,
                         eq_ref, ek_ref, ev_ref, qn_ref, kn_ref, vv_ref, gt_ref, p_ref, pi_ref,
                         *, n_pairs, n_chunks, n_heads, batch_chunks):
    c = CHUNK
    rows = n_chunks * c
    pg = pl.program_id(1)
    cb = pl.program_id(2)
    hist = SUBLANES

    @pl.when(cb == 0)
    def _():
        eq_ref[0:hist, :] = jnp.zeros((hist, eq_ref.shape[1]), F32)
        ek_ref[0:hist, :] = jnp.zeros((hist, ek_ref.shape[1]), F32)
        ev_ref[0:hist, :] = jnp.zeros((hist, ev_ref.shape[1]), F32)
        s_ref[...] = jnp.zeros(s_ref.shape, F32)

    eq_ref[hist:hist + rows, :] = q_ref[...]
    ek_ref[hist:hist + rows, :] = k_ref[...]
    ev_ref[hist:hist + rows, :] = v_ref[...]
    xq = _conv_taps(eq_ref, wq_ref, hist, rows)
    xk = _conv_taps(ek_ref, wk_ref, hist, rows)
    vv_ref[...] = _conv_taps(ev_ref, wv_ref, hist, rows)
    for p in range(n_pairs):
        sl = slice(p * HEAD_DIM, (p + 1) * HEAD_DIM)
        qn_ref[:, sl] = (_l2norm(xq[:, sl]) * (HEAD_DIM ** -0.5)).astype(BF16)
        kn_ref[:, sl] = _l2norm(xk[:, sl]).astype(BF16)
    keep = CONV_WIDTH - 1
    ccq_ref[0] = eq_ref[hist + rows - keep:hist + rows, :]
    cck_ref[0] = ek_ref[hist + rows - keep:hist + rows, :]
    ccv_ref[0] = ev_ref[hist + rows - keep:hist + rows, :]
    eq_ref[0:hist, :] = eq_ref[rows:rows + hist, :]
    ek_ref[0:hist, :] = ek_ref[rows:rows + hist, :]
    ev_ref[0:hist, :] = ev_ref[rows:rows + hist, :]

    shift = lax.rem(LANES - 2 * n_pairs * pg, LANES)
    gt_ref[...] = pltpu.roll(gc_ref[...], shift, 1)

    n_quads = n_pairs // 2
    groups, incl, strict = _quad_masks(c, c)
    pair_row = lambda ci, p: ci * (n_heads // 2) + n_pairs * pg + p
    for c0 in range(0, n_chunks, batch_chunks):
        l_cats, where = [], []
        for ci in range(c0, c0 + batch_chunks):
            r0 = ci * c
            g_all = gt_ref[r0:r0 + c, :]
            grams = [_gram(kn_ref[r0:r0 + c, p * HEAD_DIM:(p + 1) * HEAD_DIM],
                           qn_ref[r0:r0 + c, p * HEAD_DIM:(p + 1) * HEAD_DIM]) for p in range(n_pairs)]
            for qd in range(n_quads):
                ab = jnp.concatenate([grams[2 * qd], grams[2 * qd + 1]], axis=1)
                gcr = jnp.concatenate([gr_ref[0, 0, pl.ds(pair_row(ci, 2 * qd), 1), :],
                                       gr_ref[0, 0, pl.ds(pair_row(ci, 2 * qd + 1), 1), :]], axis=1)
                beta = _quad_select(g_all, QUAD * qd, groups)
                gc = _quad_select(g_all, n_heads + QUAD * qd, groups)
                dec = jnp.exp(jnp.where(incl, gc - gcr, NEG_BIG))
                pi_ref[ci, qd] = (ab[c:] * dec).astype(BF16)
                l_cats.append(jnp.where(strict, beta * ab[:c] * dec, 0.0))
                where.append((ci, qd))
        for (ci, qd), pm in zip(where, _unit_lower_inverses_minus_identity(l_cats, c, groups)):
            p_ref[ci, qd] = pm.astype(BF16)

    lane_wide = lax.broadcasted_iota(jnp.int32, (1, 2 * HEAD_DIM), 1)
    pairs = range(n_pairs)

    def chunk_step(ci, carry):
        r0 = pl.multiple_of(ci * c, c)
        g_all = gt_ref[pl.ds(r0, c), :]
        half = lambda ref, p: ref[ci, p // 2, :, (p % 2) * 2 * c:(p % 2 + 1) * 2 * c]
        kns = [kn_ref[pl.ds(r0, c), p * HEAD_DIM:(p + 1) * HEAD_DIM] for p in pairs]
        qns = [qn_ref[pl.ds(r0, c), p * HEAD_DIM:(p + 1) * HEAD_DIM] for p in pairs]
        s_cat = lambda p: jnp.concatenate([s_ref[0, 2 * p], s_ref[0, 2 * p + 1]], axis=1)
        kqs = [jnp.dot(jnp.concatenate([kns[p], qns[p]], axis=0), s_cat(p).astype(BF16),
                       preferred_element_type=F32) for p in pairs]
        beta_w = [_head_cols(g_all, 2 * p, HEAD_DIM) for p in pairs]
        gc_w = [_head_cols(g_all, n_heads + 2 * p, HEAD_DIM) for p in pairs]
        egc_w = [jnp.exp(g) for g in gc_w]
        rs = [beta_w[p] * (vv_ref[pl.ds(r0, c), 2 * p * HEAD_DIM:(2 * p + 2) * HEAD_DIM] - egc_w[p] * kqs[p][:c])
              for p in pairs]
        us = [rs[p] + jnp.dot(half(p_ref, p), _blockdiag_wide(rs[p]), preferred_element_type=F32)
              for p in pairs]
        os_ = [egc_w[p] * kqs[p][c:] + jnp.dot(half(pi_ref, p), _blockdiag_wide(us[p]),
                                               preferred_element_type=F32) for p in pairs]
        for p in pairs:
            gcr = gr_ref[0, 0, pl.ds(pair_row(ci, p), 1), :]
            gl_w = jnp.where(lane_wide < HEAD_DIM, gcr[:, c - 1:c], gcr[:, 2 * c - 1:2 * c])
            w = (jnp.exp(gl_w - gc_w[p]) * us[p]).astype(BF16)
            s_new = s_cat(p) * jnp.exp(gl_w) + lax.dot_general(
                kns[p], w, (((0,), (0,)), ((), ())), preferred_element_type=F32)
            s_ref[0, 2 * p] = s_new[:, :HEAD_DIM]
            s_ref[0, 2 * p + 1] = s_new[:, HEAD_DIM:]
        for p in pairs:
            for e in range(2):
                hs = slice((2 * p + e) * HEAD_DIM, (2 * p + e + 1) * HEAD_DIM)
                y_ref[pl.ds(r0, c), hs] = _gated_out(os_[p][:, e * HEAD_DIM:(e + 1) * HEAD_DIM],
                                                     z_ref[pl.ds(r0, c), hs], onw_ref[...])
        return carry

    lax.fori_loop(0, n_chunks, chunk_step, 0)


def _delta_prompt(proj, gates, gates_rows, conv_w, o_norm_w, layer, batch, seq, n_heads, d_inner,
                  n_pairs=8, n_chunks=CHUNKS_PER_STEP, batch_chunks=2):
    qk_dim = n_heads // 2 * HEAD_DIM
    rows = n_chunks * CHUNK
    nrb = seq // rows
    wq, wv = n_pairs * HEAD_DIM, 2 * n_pairs * HEAD_DIM
    n_pg = n_heads // (2 * n_pairs)
    kern = functools.partial(_delta_prompt_kernel, n_pairs=n_pairs, n_chunks=n_chunks, n_heads=n_heads,
                             batch_chunks=batch_chunks)
    row_map = lambda col0: (lambda b, g, r: (b * nrb + r, col0 + g))
    keep = CONV_WIDTH - 1
    out_shapes = (
        jax.ShapeDtypeStruct((batch * seq, d_inner), BF16),
        jax.ShapeDtypeStruct((batch, n_heads, HEAD_DIM, HEAD_DIM), F32),
        jax.ShapeDtypeStruct((batch, keep, qk_dim), F32),
        jax.ShapeDtypeStruct((batch, keep, qk_dim), F32),
        jax.ShapeDtypeStruct((batch, keep, d_inner), F32),
    )
    return pl.pallas_call(
        kern,
        grid=(batch, n_pg, nrb),
        in_specs=[
            pl.BlockSpec((rows, wq), row_map(0)),
            pl.BlockSpec((rows, wq), row_map(qk_dim // wq)),
            pl.BlockSpec((rows, wv), row_map(2 * qk_dim // wv)),
            pl.BlockSpec((rows, wv), row_map((2 * qk_dim + d_inner) // wv)),
            pl.BlockSpec((rows, LANES), lambda b, g, r: (b * nrb + r, 0)),
            pl.BlockSpec((1, 1, n_chunks * (n_heads // 2), 2 * CHUNK), lambda b, g, r: (b, r, 0, 0)),
            pl.BlockSpec((None, CONV_WIDTH, wq), lambda b, g, r: (layer, 0, g)),
            pl.BlockSpec((None, CONV_WIDTH, wq), lambda b, g, r: (layer, 0, qk_dim // wq + g)),
            pl.BlockSpec((None, CONV_WIDTH, wv), lambda b, g, r: (layer, 0, 2 * qk_dim // wv + g)),
            pl.BlockSpec((None, 1, HEAD_DIM), lambda b, g, r: (layer, 0, 0)),
        ],
        out_specs=(
            pl.BlockSpec((rows, wv), lambda b, g, r: (b * nrb + r, g)),
            pl.BlockSpec((1, 2 * n_pairs, HEAD_DIM, HEAD_DIM), lambda b, g, r: (b, g, 0, 0)),
            pl.BlockSpec((1, keep, wq), lambda b, g, r: (b, 0, g)),
            pl.BlockSpec((1, keep, wq), lambda b, g, r: (b, 0, g)),
            pl.BlockSpec((1, keep, wv), lambda b, g, r: (b, 0, g)),
        ),
        out_shape=out_shapes,
        scratch_shapes=[
            pltpu.VMEM((SUBLANES + rows, wq), F32),
            pltpu.VMEM((SUBLANES + rows, wq), F32),
            pltpu.VMEM((SUBLANES + rows, wv), F32),
            pltpu.VMEM((rows, wq), BF16),
            pltpu.VMEM((rows, wq), BF16),
            pltpu.VMEM((rows, wv), F32),
            pltpu.VMEM((rows, LANES), F32),
            pltpu.VMEM((n_chunks, n_pairs // 2, CHUNK, QUAD * CHUNK), BF16),
            pltpu.VMEM((n_chunks, n_pairs // 2, CHUNK, QUAD * CHUNK), BF16),
        ],
        compiler_params=_cparams("arbitrary", "arbitrary", "arbitrary"),
        name="delta_prompt",
    )(proj, proj, proj, proj, gates, gates_rows, conv_w, conv_w, conv_w, o_norm_w)


def _delta_sample_kernel(*refs, n_pairs, n_seqs, dec_seq, n_heads, aliased):
    (q_ref, k_ref, v_ref, z_ref, cq_ref, ck_ref, cv_ref, gc_ref, gr_ref, wq_ref, wk_ref, wv_ref,
     onw_ref, s_in_ref) = refs[:14]
    refs = refs[15:] if aliased else refs[14:]
    y_ref, s_out_ref, ccq_ref, cck_ref, ccv_ref, eq_ref, ek_ref, ev_ref = refs
    c = n_seqs * dec_seq
    pg = pl.program_id(1)
    hist = SUBLANES

    def conv(x_ref, cache_ref, ext_ref, w_ref, cc_ref):
        cols = ext_ref.shape[2]
        ext_ref[:, 0:hist, :] = cache_ref[...]
        ext_ref[:, hist:hist + dec_seq, :] = x_ref[...].reshape(n_seqs, dec_seq, cols)
        keep = CONV_WIDTH - 1
        cc_ref[...] = ext_ref[:, hist + dec_seq - keep:hist + dec_seq, :]
        return _conv_taps(ext_ref, w_ref, hist, dec_seq).reshape(c, cols)

    xq = conv(q_ref, cq_ref, eq_ref, wq_ref, ccq_ref)
    xk = conv(k_ref, ck_ref, ek_ref, wk_ref, cck_ref)
    xv = conv(v_ref, cv_ref, ev_ref, wv_ref, ccv_ref)

    shift = lax.rem(LANES - 2 * n_pairs * pg, LANES)
    g_all = pltpu.roll(gc_ref[...], shift, 1)
    lane_wide = lax.broadcasted_iota(jnp.int32, (1, 2 * HEAD_DIM), 1)
    row_seq = lax.broadcasted_iota(jnp.int32, (c, 2 * HEAD_DIM), 0) // dec_seq
    pairs = range(n_pairs)
    seqs = range(n_seqs)
    hd = lambda p: slice(p * HEAD_DIM, (p + 1) * HEAD_DIM)

    qns = [_l2norm(xq[:, hd(p)]) * (HEAD_DIM ** -0.5) for p in pairs]
    kns = [_l2norm(xk[:, hd(p)]) for p in pairs]
    knb = [k.astype(BF16) for k in kns]

    groups, incl, strict = _quad_masks(c, dec_seq)
    grams = [_gram(knb[p], qns[p].astype(BF16)) for p in pairs]
    gcrs = [gr_ref[0, pl.ds(n_pairs * pg + p, 1), :] for p in pairs]
    l_cats, pis = [], []
    for qd in range(n_pairs // 2):
        ab = jnp.concatenate([grams[2 * qd], grams[2 * qd + 1]], axis=1)
        gcr = jnp.concatenate([gcrs[2 * qd], gcrs[2 * qd + 1]], axis=1)
        beta = _quad_select(g_all, QUAD * qd, groups)
        gc = _quad_select(g_all, n_heads + QUAD * qd, groups)
        dec = jnp.exp(jnp.where(incl, gc - gcr, NEG_BIG))
        pis.append((ab[c:] * dec).astype(BF16))
        l_cats.append(jnp.where(strict, beta * ab[:c] * dec, 0.0))
    pms = [pm.astype(BF16) for pm in _unit_lower_inverses_minus_identity(l_cats, dec_seq, groups)]
    half = lambda mats, p: mats[p // 2][:, (p % 2) * 2 * c:(p % 2 + 1) * 2 * c]

    s_cat = lambda p, s: jnp.concatenate([s_in_ref[s, 2 * p], s_in_ref[s, 2 * p + 1]], axis=1)
    kqs = [[jnp.dot(jnp.concatenate([kns[p][s * dec_seq:(s + 1) * dec_seq],
                                     qns[p][s * dec_seq:(s + 1) * dec_seq]], axis=0).astype(BF16),
                    s_cat(p, s).astype(BF16), preferred_element_type=F32) for s in seqs] for p in pairs]
    k_s = [jnp.concatenate([kqs[p][s][:dec_seq] for s in seqs], axis=0) for p in pairs]
    q_s = [jnp.concatenate([kqs[p][s][dec_seq:] for s in seqs], axis=0) for p in pairs]

    def last_gc(p, s):
        last = (s + 1) * dec_seq - 1
        return jnp.where(lane_wide < HEAD_DIM, gcrs[p][:, last:last + 1], gcrs[p][:, c + last:c + last + 1])

    gls = [[last_gc(p, s) for s in seqs] for p in pairs]
    beta_w = [_head_cols(g_all, 2 * p, HEAD_DIM) for p in pairs]
    gc_w = [_head_cols(g_all, n_heads + 2 * p, HEAD_DIM) for p in pairs]
    egc_w = [jnp.exp(g) for g in gc_w]
    rs = [beta_w[p] * (xv[:, 2 * p * HEAD_DIM:(2 * p + 2) * HEAD_DIM] - egc_w[p] * k_s[p]) for p in pairs]
    us = [rs[p] + jnp.dot(half(pms, p), _blockdiag_wide(rs[p]), preferred_element_type=F32) for p in pairs]
    os_ = [egc_w[p] * q_s[p] + jnp.dot(half(pis, p), _blockdiag_wide(us[p]), preferred_element_type=F32)
           for p in pairs]
    for p in pairs:
        gl_rows = jnp.concatenate([jnp.broadcast_to(g, (dec_seq, 2 * HEAD_DIM)) for g in gls[p]], axis=0)
        w = jnp.exp(gl_rows - gc_w[p]) * us[p]
        for s in seqs:
            w_s = jnp.where(row_seq == s, w, 0.0).astype(BF16)
            s_new = s_cat(p, s) * jnp.exp(gls[p][s]) + lax.dot_general(
                knb[p], w_s, (((0,), (0,)), ((), ())), preferred_element_type=F32)
            s_out_ref[s, 2 * p] = s_new[:, :HEAD_DIM]
            s_out_ref[s, 2 * p + 1] = s_new[:, HEAD_DIM:]
    for p in pairs:
        for e in range(2):
            hs = slice((2 * p + e) * HEAD_DIM, (2 * p + e + 1) * HEAD_DIM)
            y_ref[:, hs] = _gated_out(os_[p][:, e * HEAD_DIM:(e + 1) * HEAD_DIM], z_ref[:, hs], onw_ref[...])


def _delta_sample(proj, gates, gates_rows, conv_cache_pad, state, prev_state_out, conv_w, o_norm_w, layer,
                  row_off, batch, dec_seq, n_heads, d_inner, n_pairs=4, n_seqs=8):
    qk_dim = n_heads // 2 * HEAD_DIM
    c = n_seqs * dec_seq
    wq, wv = n_pairs * HEAD_DIM, 2 * n_pairs * HEAD_DIM
    n_pg = n_heads // (2 * n_pairs)
    rb0 = row_off // c
    aliased = prev_state_out is not None
    kern = functools.partial(_delta_sample_kernel, n_pairs=n_pairs, n_seqs=n_seqs, dec_seq=dec_seq,
                             n_heads=n_heads, aliased=aliased)
    row_map = lambda col0: (lambda i, g: (rb0 + i, col0 + g))
    keep = CONV_WIDTH - 1
    in_specs = [
        pl.BlockSpec((c, wq), row_map(0)),
        pl.BlockSpec((c, wq), row_map(qk_dim // wq)),
        pl.BlockSpec((c, wv), row_map(2 * qk_dim // wv)),
        pl.BlockSpec((c, wv), row_map((2 * qk_dim + d_inner) // wv)),
        pl.BlockSpec((n_seqs, SUBLANES, wq), lambda i, g: (i, 0, g)),
        pl.BlockSpec((n_seqs, SUBLANES, wq), lambda i, g: (i, 0, qk_dim // wq + g)),
        pl.BlockSpec((n_seqs, SUBLANES, wv), lambda i, g: (i, 0, 2 * qk_dim // wv + g)),
        pl.BlockSpec((c, LANES), lambda i, g: (rb0 + i, 0)),
        pl.BlockSpec((1, n_heads // 2, 2 * c), lambda i, g: (i, 0, 0)),
        pl.BlockSpec((None, CONV_WIDTH, wq), lambda i, g: (layer, 0, g)),
        pl.BlockSpec((None, CONV_WIDTH, wq), lambda i, g: (layer, 0, qk_dim // wq + g)),
        pl.BlockSpec((None, CONV_WIDTH, wv), lambda i, g: (layer, 0, 2 * qk_dim // wv + g)),
        pl.BlockSpec((None, 1, HEAD_DIM), lambda i, g: (layer, 0, 0)),
        pl.BlockSpec((None, n_seqs, 2 * n_pairs, HEAD_DIM, HEAD_DIM), lambda i, g: (layer, i, g, 0, 0)),
    ]
    args = [proj, proj, proj, proj, conv_cache_pad, conv_cache_pad, conv_cache_pad, gates, gates_rows,
            conv_w, conv_w, conv_w, o_norm_w, state]
    aliases = {}
    if aliased:
        in_specs.append(pl.BlockSpec(memory_space=pl.ANY))
        args.append(prev_state_out)
        aliases = {14: 1}
    out_shapes = (
        jax.ShapeDtypeStruct((batch * dec_seq, d_inner), BF16),
        jax.ShapeDtypeStruct(state.shape, F32),
        jax.ShapeDtypeStruct((batch, keep, qk_dim), F32),
        jax.ShapeDtypeStruct((batch, keep, qk_dim), F32),
        jax.ShapeDtypeStruct((batch, keep, d_inner), F32),
    )
    return pl.pallas_call(
        kern,
        grid=(batch // n_seqs, n_pg),
        in_specs=in_specs,
        out_specs=(
            pl.BlockSpec((c, wv), lambda i, g: (i, g)),
            pl.BlockSpec((None, n_seqs, 2 * n_pairs, HEAD_DIM, HEAD_DIM), lambda i, g: (layer, i, g, 0, 0)),
            pl.BlockSpec((n_seqs, keep, wq), lambda i, g: (i, 0, g)),
            pl.BlockSpec((n_seqs, keep, wq), lambda i, g: (i, 0, g)),
            pl.BlockSpec((n_seqs, keep, wv), lambda i, g: (i, 0, g)),
        ),
        out_shape=out_shapes,
        scratch_shapes=[
            pltpu.VMEM((n_seqs, SUBLANES + dec_seq, wq), F32),
            pltpu.VMEM((n_seqs, SUBLANES + dec_seq, wq), F32),
            pltpu.VMEM((n_seqs, SUBLANES + dec_seq, wv), F32),
        ],
        input_output_aliases=aliases,
        compiler_params=_cparams("arbitrary", "arbitrary"),
        name="delta_sample",
    )(*args)


def _pool_finish(pooled, z, wg_ref, sc_ref):
    mixed = jnp.dot(pooled.astype(BF16), wg_ref[...], preferred_element_type=F32)
    return (mixed * sc_ref[...] * _silu(z)).astype(BF16)


def _pool_prompt_kernel(u_ref, prev_ref, z_ref, wg_ref, sc_ref, y_ref, cache_ref, ext_ref, *, tt, nt):
    g = pl.program_id(0)
    t = pl.program_id(2)
    hist = POOL_BUF + 1

    @pl.when(t == 0)
    def _():
        ext_ref[0:hist, :] = jnp.zeros((hist, ext_ref.shape[1]), F32)

    @pl.when(t > 0)
    def _():
        ext_ref[0:hist, :] = prev_ref[...]

    ext_ref[hist:hist + tt, :] = u_ref[...]
    pos = t * tt + lax.broadcasted_iota(jnp.int32, (tt, 1), 0)
    for gi, win in enumerate(POOL_WINDOWS):
        @pl.when(g == gi)
        def _(win=win):
            x = ext_ref[hist:hist + tt, :]
            acc = x
            for d in range(1, win):
                acc = acc + ext_ref[hist - d:hist - d + tt, :]
            cnt = jnp.minimum(pos + 1, win).astype(F32)
            y_ref[...] = _pool_finish(acc / cnt - x, z_ref[...], wg_ref, sc_ref)

    @pl.when(t == nt - 1)
    def _():
        cache_ref[0] = ext_ref[hist + tt - POOL_BUF:hist + tt, :]


def _pool_prompt(proj, w_group, scale, layer, batch, seq, d_inner, tt=512):
    gd = d_inner // len(POOL_WINDOWS)
    nt = seq // tt
    hist = POOL_BUF + 1
    kern = functools.partial(_pool_prompt_kernel, tt=tt, nt=nt)
    n_g = len(POOL_WINDOWS)
    return pl.pallas_call(
        kern,
        grid=(n_g, batch, nt),
        in_specs=[
            pl.BlockSpec((tt, gd), lambda g, b, t: (b * nt + t, g)),
            pl.BlockSpec((hist, gd), lambda g, b, t: (jnp.maximum((b * nt + t) * (tt // hist) - 1, 0), g)),
            pl.BlockSpec((tt, gd), lambda g, b, t: (b * nt + t, n_g + g)),
            pl.BlockSpec((None, None, gd, gd), lambda g, b, t: (layer, g, 0, 0)),
            pl.BlockSpec((None, 1, gd), lambda g, b, t: (layer, 0, g)),
        ],
        out_specs=(
            pl.BlockSpec((tt, gd), lambda g, b, t: (b * nt + t, g)),
            pl.BlockSpec((1, POOL_BUF, gd), lambda g, b, t: (b, 0, g)),
        ),
        out_shape=(
            jax.ShapeDtypeStruct((batch * seq, d_inner), BF16),
            jax.ShapeDtypeStruct((batch, POOL_BUF, d_inner), F32),
        ),
        scratch_shapes=[pltpu.VMEM((hist + tt, gd), F32)],
        compiler_params=_cparams("arbitrary", "arbitrary", "arbitrary"),
        name="pool_prompt",
    )(proj, proj, proj, w_group, scale)


def _pool_sample_kernel(u_ref, c_ref, z_ref, wg_ref, sc_ref, y_ref, cache_ref, ext_ref, *, sb, dec_seq):
    g = pl.program_id(0)
    hist = POOL_BUF + 1
    gd = ext_ref.shape[2]
    ext_ref[:, hist - POOL_BUF:hist, :] = c_ref[...]
    ext_ref[:, hist:hist + dec_seq, :] = u_ref[...].reshape(sb, dec_seq, gd)
    pos = PAST_LEN + lax.broadcasted_iota(jnp.int32, (1, dec_seq, 1), 1)
    for gi, win in enumerate(POOL_WINDOWS):
        @pl.when(g == gi)
        def _(win=win):
            x = ext_ref[:, hist:hist + dec_seq, :]
            acc = x
            for d in range(1, win):
                acc = acc + ext_ref[:, hist - d:hist - d + dec_seq, :]
            cnt = jnp.minimum(pos + 1, win).astype(F32)
            pooled = (acc / cnt - x).reshape(sb * dec_seq, gd)
            y_ref[...] = _pool_finish(pooled, z_ref[...], wg_ref, sc_ref)

    cache_ref[...] = ext_ref[:, hist + dec_seq - POOL_BUF:hist + dec_seq, :]


def _pool_sample(proj, cache, w_group, scale, layer, row_off, batch, dec_seq, d_inner, sb=16):
    gd = d_inner // len(POOL_WINDOWS)
    n_g = len(POOL_WINDOWS)
    rows = sb * dec_seq
    rb0 = row_off // rows
    hist = POOL_BUF + 1
    kern = functools.partial(_pool_sample_kernel, sb=sb, dec_seq=dec_seq)
    return pl.pallas_call(
        kern,
        grid=(n_g, batch // sb),
        in_specs=[
            pl.BlockSpec((rows, gd), lambda g, i: (rb0 + i, g)),
            pl.BlockSpec((None, sb, POOL_BUF, gd), lambda g, i: (layer, i, 0, g)),
            pl.BlockSpec((rows, gd), lambda g, i: (rb0 + i, n_g + g)),
            pl.BlockSpec((None, None, gd, gd), lambda g, i: (layer, g, 0, 0)),
            pl.BlockSpec((None, 1, gd), lambda g, i: (layer, 0, g)),
        ],
        out_specs=(
            pl.BlockSpec((rows, gd), lambda g, i: (i, g)),
            pl.BlockSpec((sb, POOL_BUF, gd), lambda g, i: (i, 0, g)),
        ),
        out_shape=(
            jax.ShapeDtypeStruct((batch * dec_seq, d_inner), BF16),
            jax.ShapeDtypeStruct((batch, POOL_BUF, d_inner), F32),
        ),
        scratch_shapes=[pltpu.VMEM((sb, hist + dec_seq, gd), F32)],
        compiler_params=_cparams("arbitrary", "arbitrary"),
        name="pool_sample",
    )(proj, cache, proj, w_group, scale)


def kernel(x_prompt, x_sample, state_delta, cache_conv, cache_pool, norm_w, final_norm_w, delta_w_in,
           delta_conv_w, delta_a_log, delta_dt_bias, delta_o_norm_w, delta_w_out, pool_w_in, pool_w_group,
           pool_scale, pool_w_out):
    batch, seq, d_model = x_prompt.shape
    dec_batch, dec_seq, _ = x_sample.shape
    depth = norm_w.shape[0]
    n_heads = state_delta.shape[2]
    d_inner = n_heads * HEAD_DIM
    conv_dim = cache_conv.shape[3]
    qk_dim = (conv_dim - d_inner) // 2
    mp, ms = batch * seq, dec_batch * dec_seq
    m = mp + ms
    n_main = conv_dim + d_inner
    n_seqs = CHUNK // dec_seq

    h = jnp.concatenate([x_prompt.reshape(mp, d_model), x_sample.reshape(ms, d_model)], axis=0)

    delta_w_out_b = delta_w_out.astype(BF16)
    pool_w_out_b = pool_w_out.astype(BF16)
    pool_w_group_b = pool_w_group.astype(BF16)
    w_tail = jnp.pad(delta_w_in[:, :, n_main:], ((0, 0), (0, 0), (0, LANES - 2 * n_heads)))
    onw = delta_o_norm_w.reshape(-1, 1, HEAD_DIM)
    pool_scale3 = pool_scale.reshape(pool_scale.shape[0], 1, d_inner)
    conv_cache_pad = jnp.pad(cache_conv, ((0, 0), (0, 0), (SUBLANES - (CONV_WIDTH - 1), 0), (0, 0)))

    sd_p, cc_p, cp_p, cc_s, cp_s = [], [], [], [], []
    sd_s = None
    for i in range(depth):
        j = i // 2
        xn = _rmsnorm(h, norm_w[i], BF16, m)
        if i % 2 == 0:
            proj = _matmul(xn, delta_w_in, j, n_main, tm=3072, tn=512)
            tail = _matmul(xn, w_tail, j, LANES, tm=3072, tn=LANES)
            gates = _gates(tail, delta_a_log[j], delta_dt_bias[j], mp, n_heads, dec_seq)
            gcum = gates[:, n_heads:2 * n_heads]
            rows_p = gcum[:mp].reshape(batch, seq // CHUNK, CHUNK, n_heads // 2, 2)
            rows_p = rows_p.transpose(0, 1, 3, 4, 2).reshape(batch, seq // (CHUNKS_PER_STEP * CHUNK), CHUNKS_PER_STEP * (n_heads // 2), 2 * CHUNK)
            rows_s = gcum[mp:].reshape(ms // CHUNK, CHUNK, n_heads // 2, 2)
            rows_s = rows_s.transpose(0, 2, 3, 1).reshape(ms // CHUNK, n_heads // 2, 2 * CHUNK)
            y_p, sp, cq, ck, cv = _delta_prompt(proj, gates, rows_p, delta_conv_w, onw, j, batch, seq,
                                                n_heads, d_inner)
            y_s, sd_s, cqs, cks, cvs = _delta_sample(proj, gates, rows_s, conv_cache_pad[j], state_delta, sd_s,
                                                     delta_conv_w, onw, j, mp, dec_batch, dec_seq, n_heads,
                                                     d_inner, n_seqs=n_seqs)
            sd_p.append(sp)
            cc_p.append(jnp.concatenate([cq, ck, cv], axis=-1))
            cc_s.append(jnp.concatenate([cqs, cks, cvs], axis=-1))
            w_out, lw = delta_w_out_b, j
        else:
            proj = _matmul(xn, pool_w_in, j, 2 * d_inner, tm=3072, tn=512)
            y_p, pp = _pool_prompt(proj, pool_w_group_b, pool_scale3, j, batch, seq, d_inner)
            y_s, ps = _pool_sample(proj, cache_pool, pool_w_group_b, pool_scale3, j, mp, dec_batch, dec_seq,
                                   d_inner)
            cp_p.append(pp)
            cp_s.append(ps)
            w_out, lw = pool_w_out_b, j
        h = _matmul_residual(y_p, w_out, lw, h, 0)
        h = _matmul_residual(y_s, w_out, lw, h, mp)

    y_prompt = _rmsnorm(h, final_norm_w, F32, mp, 0).reshape(batch, seq, d_model)
    y_sample = _rmsnorm(h, final_norm_w, F32, ms, mp).reshape(dec_batch, dec_seq, d_model)
    return (y_prompt, y_sample, jnp.stack(sd_p), jnp.stack(cc_p), jnp.stack(cp_p),
            sd_s, jnp.stack(cc_s), jnp.stack(cp_s))
```

```python
import functools

import jax
import jax.numpy as jnp
from jax import lax
from jax.experimental import pallas as pl
from jax.experimental.pallas import tpu as pltpu

F32 = jnp.float32
BF16 = jnp.bfloat16

HEAD_DIM = 128
CONV_WIDTH = 4
CHUNK = 64
CHUNKS_PER_STEP = 4
POOL_WINDOWS = (2, 4, 8, 16)
POOL_BUF = max(POOL_WINDOWS) - 1
PAST_LEN = 16384
NORM_EPS = 1e-6
NEG_BIG = -1e30

LANES = 128
SUBLANES = 8
VMEM_LIMIT = 56 * 1024 * 1024


def _cparams(*sem):
    return pltpu.CompilerParams(dimension_semantics=sem, vmem_limit_bytes=VMEM_LIMIT)


def _silu(x):
    return x * jax.nn.sigmoid(x)


def _rmsnorm_kernel(x_ref, w_ref, o_ref):
    x = x_ref[...]
    y = x * lax.rsqrt(jnp.mean(x * x, axis=-1, keepdims=True) + NORM_EPS)
    o_ref[...] = (y * w_ref[...]).astype(o_ref.dtype)


def _rmsnorm(x, w, out_dtype, rows, row_off=0, tm=512):
    d = x.shape[1]
    off = row_off // tm
    return pl.pallas_call(
        _rmsnorm_kernel,
        grid=(rows // tm,),
        in_specs=[pl.BlockSpec((tm, d), lambda i: (i + off, 0)),
                  pl.BlockSpec((1, d), lambda i: (0, 0))],
        out_specs=pl.BlockSpec((tm, d), lambda i: (i, 0)),
        out_shape=jax.ShapeDtypeStruct((rows, d), out_dtype),
        compiler_params=_cparams("arbitrary"),
        name="rmsnorm",
    )(x, w.reshape(1, d))


def _matmul_kernel(x_ref, w_ref, o_ref, *, w_transposed):
    w = w_ref[...].astype(BF16)
    contract = (((1,), (1 if w_transposed else 0,)), ((), ()))
    o_ref[...] = lax.dot_general(x_ref[...], w, contract, preferred_element_type=F32)


def _matmul(x, w, layer, n_cols, tm, tn, w_transposed=False):
    m, k = x.shape
    if w_transposed:
        w_spec = pl.BlockSpec((None, tn, k), lambda i, j: (layer, j, 0))
    else:
        w_spec = pl.BlockSpec((None, k, tn), lambda i, j: (layer, 0, j))
    return pl.pallas_call(
        functools.partial(_matmul_kernel, w_transposed=w_transposed),
        grid=(m // tm, n_cols // tn),
        in_specs=[pl.BlockSpec((tm, k), lambda i, j: (i, 0)), w_spec],
        out_specs=pl.BlockSpec((tm, tn), lambda i, j: (i, j)),
        out_shape=jax.ShapeDtypeStruct((m, n_cols), F32),
        compiler_params=_cparams("arbitrary", "arbitrary"),
        name="in_proj",
    )(x, w)


def _matmul_res_kernel(yp_ref, ys_ref, w_ref, h_ref, o_ref, *, n_prompt_blocks):
    i = pl.program_id(0)
    w = w_ref[...]

    @pl.when(i < n_prompt_blocks)
    def _():
        o_ref[...] = h_ref[...] + jnp.dot(yp_ref[...], w, preferred_element_type=F32)

    @pl.when(i >= n_prompt_blocks)
    def _():
        o_ref[...] = h_ref[...] + jnp.dot(ys_ref[...], w, preferred_element_type=F32)


def _matmul_residual(y_p, y_s, w, layer, h, tm=1024, tn=512):
    rows_p, k = y_p.shape
    m, n = h.shape
    npb = rows_p // tm
    return pl.pallas_call(
        functools.partial(_matmul_res_kernel, n_prompt_blocks=npb),
        grid=(m // tm, n // tn),
        in_specs=[pl.BlockSpec((tm, k), lambda i, j: (jnp.minimum(i, npb - 1), 0)),
                  pl.BlockSpec((tm, k), lambda i, j: (jnp.maximum(i - npb, 0), 0)),
                  pl.BlockSpec((None, k, tn), lambda i, j: (layer, 0, j)),
                  pl.BlockSpec((tm, tn), lambda i, j: (i, j))],
        out_specs=pl.BlockSpec((tm, tn), lambda i, j: (i, j)),
        out_shape=jax.ShapeDtypeStruct((m, n), F32),
        input_output_aliases={3: 0},
        compiler_params=_cparams("arbitrary", "arbitrary"),
        name="out_proj",
    )(y_p, y_s, w, h)


def _gates_kernel(t_ref, alog_ref, dt_ref, o_ref, *, n_prompt_blocks, n_heads, dec_seq):
    i = pl.program_id(0)
    x = t_ref[...]
    lane = lax.broadcasted_iota(jnp.int32, x.shape, 1)
    row = lax.broadcasted_iota(jnp.int32, x.shape, 0)
    beta = jax.nn.sigmoid(x)
    a = x + dt_ref[...]
    softplus = jnp.maximum(a, 0.0) + jnp.log(1.0 + jnp.exp(-jnp.abs(a)))
    g = -jnp.exp(alog_ref[...]) * softplus
    seg = jnp.where(i < n_prompt_blocks, CHUNK, dec_seq)
    pos = row & (seg - 1)
    cs = g
    s = 1
    while s < CHUNK:
        cs = cs + jnp.where(pos >= s, pltpu.roll(cs, s, 0), 0.0)
        s *= 2
    o_ref[...] = jnp.where(lane < n_heads, beta, jnp.where(lane < 2 * n_heads, cs, 0.0))


def _gates(tail, a_log, dt_bias, n_prompt_rows, n_heads, dec_seq, tm=1024):
    m = tail.shape[0]
    pad = lambda v: jnp.zeros((1, LANES), F32).at[0, n_heads:2 * n_heads].set(v.astype(F32))
    kern = functools.partial(_gates_kernel, n_prompt_blocks=n_prompt_rows // tm, n_heads=n_heads,
                             dec_seq=dec_seq)
    return pl.pallas_call(
        kern,
        grid=(m // tm,),
        in_specs=[pl.BlockSpec((tm, LANES), lambda i: (i, 0)),
                  pl.BlockSpec((1, LANES), lambda i: (0, 0)),
                  pl.BlockSpec((1, LANES), lambda i: (0, 0))],
        out_specs=pl.BlockSpec((tm, LANES), lambda i: (i, 0)),
        out_shape=jax.ShapeDtypeStruct((m, LANES), F32),
        compiler_params=_cparams("arbitrary"),
        name="delta_gates",
    )(tail, pad(a_log), pad(dt_bias))


QUAD = 4


def _blockdiag(y, masks):
    yb = y.astype(BF16)
    zero = jnp.zeros_like(yb)
    return jnp.concatenate([jnp.where(m, yb, zero) for m in masks], axis=0)


def _blockdiag_wide(r):
    rb = r.astype(BF16)
    z = jnp.zeros_like(rb[:, :HEAD_DIM])
    return jnp.concatenate([jnp.concatenate([rb[:, :HEAD_DIM], z], axis=1),
                            jnp.concatenate([z, rb[:, HEAD_DIM:]], axis=1)], axis=0)


def _unit_lower_inverses_minus_identity(l_cats, seg, masks):
    c = l_cats[0].shape[0]
    dot = lambda a, b: jnp.dot(a.astype(BF16), _blockdiag(b, masks), preferred_element_type=F32)
    ps = [-l for l in l_cats]
    qs = [dot(l, l) for l in l_cats]
    n_steps = seg.bit_length() - 2
    for step in range(n_steps):
        if step < n_steps - 1:
            pqs = [dot(jnp.concatenate([p, q], axis=0), q) for p, q in zip(ps, qs)]
            ps = [p + q + pq[:c] for p, q, pq in zip(ps, qs, pqs)]
            qs = [pq[c:] for pq in pqs]
        else:
            pqs = [dot(p, q) for p, q in zip(ps, qs)]
            ps = [p + q + pq for p, q, pq in zip(ps, qs, pqs)]
    return ps


def _gram_quad(kn_a, qn_a, kn_b, qn_b):
    zero = jnp.zeros_like(kn_a)
    lhs = jnp.concatenate([jnp.concatenate([kn_a, kn_b], axis=1),
                           jnp.concatenate([qn_a, qn_b], axis=1)], axis=0)
    ka = jnp.concatenate([kn_a, zero], axis=1)
    kb = jnp.concatenate([zero, kn_b], axis=1)
    rhs = jnp.concatenate([ka, ka, kb, kb], axis=0)
    return lax.dot_general(lhs, rhs, (((1,), (1,)), ((), ())), preferred_element_type=F32)


def _quad_masks(c, seg):
    lane = lax.broadcasted_iota(jnp.int32, (c, QUAD * c), 1)
    ii = lax.broadcasted_iota(jnp.int32, (c, QUAD * c), 0)
    jj = lane & (c - 1)
    groups = [(lane // c) == g for g in range(QUAD)]
    incl = (ii >= jj) if seg == c else ((ii >= jj) & ((ii // seg) == (jj // seg)))
    return groups, incl, ii > jj


def _quad_select(g_all, base, groups):
    out = g_all[:, base + QUAD - 1:base + QUAD]
    for g in range(QUAD - 2, -1, -1):
        out = jnp.where(groups[g], g_all[:, base + g:base + g + 1], out)
    return out


def _head_cols(g_all, idx, width):
    c = g_all.shape[0]
    return jnp.concatenate([jnp.broadcast_to(g_all[:, idx:idx + 1], (c, width)),
                            jnp.broadcast_to(g_all[:, idx + 1:idx + 2], (c, width))], axis=1)


def _gated_out(o, z, onw):
    on = o * lax.rsqrt(jnp.mean(o * o, axis=-1, keepdims=True) + NORM_EPS) * onw
    return (on * _silu(z)).astype(BF16)


def _conv_taps(ext_ref, w_ref, start, n):
    lo = start - (CONV_WIDTH - 1)
    if len(ext_ref.shape) == 2:
        acc = w_ref[0:1, :] * ext_ref[lo:lo + n, :]
        for j in range(1, CONV_WIDTH):
            acc = acc + w_ref[j:j + 1, :] * ext_ref[lo + j:lo + j + n, :]
    else:
        acc = w_ref[0:1, :] * ext_ref[:, lo:lo + n, :]
        for j in range(1, CONV_WIDTH):
            acc = acc + w_ref[j:j + 1, :] * ext_ref[:, lo + j:lo + j + n, :]
    return _silu(acc)


def _l2norm(x):
    return x * lax.rsqrt(jnp.sum(x * x, axis=-1, keepdims=True) + NORM_EPS)


def _delta_prompt_kernel(q_ref, k_ref, v_ref, z_ref, gc_ref, gr_ref, wq_ref, wk_ref, wv_ref, onw_ref,
                         y_ref, s_ref, ccq_ref, cck_ref, ccv_ref,
                         eq_ref, ek_ref, ev_ref, qn_ref, kn_ref, vv_ref, gt_ref, p_ref, pi_ref,
                         *, n_pairs, n_chunks, n_heads, batch_chunks):
    c = CHUNK
    rows = n_chunks * c
    pg = pl.program_id(1)
    cb = pl.program_id(2)
    hist = SUBLANES

    @pl.when(cb == 0)
    def _():
        eq_ref[0:hist, :] = jnp.zeros((hist, eq_ref.shape[1]), F32)
        ek_ref[0:hist, :] = jnp.zeros((hist, ek_ref.shape[1]), F32)
        ev_ref[0:hist, :] = jnp.zeros((hist, ev_ref.shape[1]), F32)
        s_ref[...] = jnp.zeros(s_ref.shape, F32)

    eq_ref[hist:hist + rows, :] = q_ref[...]
    ek_ref[hist:hist + rows, :] = k_ref[...]
    ev_ref[hist:hist + rows, :] = v_ref[...]
    xq = _conv_taps(eq_ref, wq_ref, hist, rows)
    xk = _conv_taps(ek_ref, wk_ref, hist, rows)
    vv_ref[...] = _conv_taps(ev_ref, wv_ref, hist, rows)
    for p in range(n_pairs):
        sl = slice(p * HEAD_DIM, (p + 1) * HEAD_DIM)
        qn_ref[:, sl] = (_l2norm(xq[:, sl]) * (HEAD_DIM ** -0.5)).astype(BF16)
        kn_ref[:, sl] = _l2norm(xk[:, sl]).astype(BF16)
    keep = CONV_WIDTH - 1
    ccq_ref[0] = eq_ref[hist + rows - keep:hist + rows, :]
    cck_ref[0] = ek_ref[hist + rows - keep:hist + rows, :]
    ccv_ref[0] = ev_ref[hist + rows - keep:hist + rows, :]
    eq_ref[0:hist, :] = eq_ref[rows:rows + hist, :]
    ek_ref[0:hist, :] = ek_ref[rows:rows + hist, :]
    ev_ref[0:hist, :] = ev_ref[rows:rows + hist, :]

    shift = lax.rem(LANES - 2 * n_pairs * pg, LANES)
    gt_ref[...] = pltpu.roll(gc_ref[...], shift, 1)

    n_quads = n_pairs // 2
    groups, incl, strict = _quad_masks(c, c)
    pair_row = lambda ci, p: ci * (n_heads // 2) + n_pairs * pg + p
    for c0 in range(0, n_chunks, batch_chunks):
        l_cats, where = [], []
        for ci in range(c0, c0 + batch_chunks):
            r0 = ci * c
            g_all = gt_ref[r0:r0 + c, :]
            hd = lambda ref, p: ref[r0:r0 + c, p * HEAD_DIM:(p + 1) * HEAD_DIM]
            grams = [_gram_quad(hd(kn_ref, 2 * qd), hd(qn_ref, 2 * qd), hd(kn_ref, 2 * qd + 1),
                                hd(qn_ref, 2 * qd + 1)) for qd in range(n_quads)]
            for qd in range(n_quads):
                ab = grams[qd]
                gcr =jnp.concatenate([gr_ref[0, 0, pl.ds(pair_row(ci, 2 * qd), 1), :],
                                       gr_ref[0, 0, pl.ds(pair_row(ci, 2 * qd + 1), 1), :]], axis=1)
                beta = _quad_select(g_all, QUAD * qd, groups)
                gc = _quad_select(g_all, n_heads + QUAD * qd, groups)
                dec = jnp.exp(jnp.where(incl, gc - gcr, NEG_BIG))
                pi_ref[ci, qd] = (ab[c:] * dec).astype(BF16)
                l_cats.append(jnp.where(strict, beta * ab[:c] * dec, 0.0))
                where.append((ci, qd))
        for (ci, qd), pm in zip(where, _unit_lower_inverses_minus_identity(l_cats, c, groups)):
            p_ref[ci, qd] = pm.astype(BF16)

    lane_wide = lax.broadcasted_iota(jnp.int32, (1, 2 * HEAD_DIM), 1)
    pairs = range(n_pairs)

    def chunk_step(ci, carry):
        r0 = pl.multiple_of(ci * c, c)
        g_all = gt_ref[pl.ds(r0, c), :]
        half = lambda ref, p: ref[ci, p // 2, :, (p % 2) * 2 * c:(p % 2 + 1) * 2 * c]
        kns = [kn_ref[pl.ds(r0, c), p * HEAD_DIM:(p + 1) * HEAD_DIM] for p in pairs]
        qns = [qn_ref[pl.ds(r0, c), p * HEAD_DIM:(p + 1) * HEAD_DIM] for p in pairs]
        s_cat = lambda p: jnp.concatenate([s_ref[0, 2 * p], s_ref[0, 2 * p + 1]], axis=1)
        kqs = [jnp.dot(jnp.concatenate([kns[p], qns[p]], axis=0), s_cat(p).astype(BF16),
                       preferred_element_type=F32) for p in pairs]
        beta_w = [_head_cols(g_all, 2 * p, HEAD_DIM) for p in pairs]
        gc_w = [_head_cols(g_all, n_heads + 2 * p, HEAD_DIM) for p in pairs]
        egc_w = [jnp.exp(g) for g in gc_w]
        rs = [beta_w[p] * (vv_ref[pl.ds(r0, c), 2 * p * HEAD_DIM:(2 * p + 2) * HEAD_DIM] - egc_w[p] * kqs[p][:c])
              for p in pairs]
        us = [rs[p] + jnp.dot(half(p_ref, p), _blockdiag_wide(rs[p]), preferred_element_type=F32)
              for p in pairs]
        os_ = [egc_w[p] * kqs[p][c:] + jnp.dot(half(pi_ref, p), _blockdiag_wide(us[p]),
                                               preferred_element_type=F32) for p in pairs]
        for p in pairs:
            gcr = gr_ref[0, 0, pl.ds(pair_row(ci, p), 1), :]
            gl_w = jnp.where(lane_wide < HEAD_DIM, gcr[:, c - 1:c], gcr[:, 2 * c - 1:2 * c])
            w = (jnp.exp(gl_w - gc_w[p]) * us[p]).astype(BF16)
            s_new = s_cat(p) * jnp.exp(gl_w) + lax.dot_general(
                kns[p], w, (((0,), (0,)), ((), ())), preferred_element_type=F32)
            s_ref[0, 2 * p] = s_new[:, :HEAD_DIM]
            s_ref[0, 2 * p + 1] = s_new[:, HEAD_DIM:]
        for p in pairs:
            for e in range(2):
                hs = slice((2 * p + e) * HEAD_DIM, (2 * p + e + 1) * HEAD_DIM)
                y_ref[pl.ds(r0, c), hs] = _gated_out(os_[p][:, e * HEAD_DIM:(e + 1) * HEAD_DIM],
                                                     z_ref[pl.ds(r0, c), hs], onw_ref[...])
        return carry

    lax.fori_loop(0, n_chunks, chunk_step, 0)


def _delta_prompt(proj, gates, gates_rows, conv_w, o_norm_w, layer, batch, seq, n_heads, d_inner,
                  n_pairs=8, n_chunks=CHUNKS_PER_STEP, batch_chunks=2):
    qk_dim = n_heads // 2 * HEAD_DIM
    rows = n_chunks * CHUNK
    nrb = seq // rows
    wq, wv = n_pairs * HEAD_DIM, 2 * n_pairs * HEAD_DIM
    n_pg = n_heads // (2 * n_pairs)
    kern = functools.partial(_delta_prompt_kernel, n_pairs=n_pairs, n_chunks=n_chunks, n_heads=n_heads,
                             batch_chunks=batch_chunks)
    row_map = lambda col0: (lambda b, g, r: (b * nrb + r, col0 + g))
    keep = CONV_WIDTH - 1
    out_shapes = (
        jax.ShapeDtypeStruct((batch * seq, d_inner), BF16),
        jax.ShapeDtypeStruct((batch, n_heads, HEAD_DIM, HEAD_DIM), F32),
        jax.ShapeDtypeStruct((batch, keep, qk_dim), F32),
        jax.ShapeDtypeStruct((batch, keep, qk_dim), F32),
        jax.ShapeDtypeStruct((batch, keep, d_inner), F32),
    )
    return pl.pallas_call(
        kern,
        grid=(batch, n_pg, nrb),
        in_specs=[
            pl.BlockSpec((rows, wq), row_map(0)),
            pl.BlockSpec((rows, wq), row_map(qk_dim // wq)),
            pl.BlockSpec((rows, wv), row_map(2 * qk_dim // wv)),
            pl.BlockSpec((rows, wv), row_map((2 * qk_dim + d_inner) // wv)),
            pl.BlockSpec((rows, LANES), lambda b, g, r: (b * nrb + r, 0)),
            pl.BlockSpec((1, 1, n_chunks * (n_heads // 2), 2 * CHUNK), lambda b, g, r: (b, r, 0, 0)),
            pl.BlockSpec((None, CONV_WIDTH, wq), lambda b, g, r: (layer, 0, g)),
            pl.BlockSpec((None, CONV_WIDTH, wq), lambda b, g, r: (layer, 0, qk_dim // wq + g)),
            pl.BlockSpec((None, CONV_WIDTH, wv), lambda b, g, r: (layer, 0, 2 * qk_dim // wv + g)),
            pl.BlockSpec((None, 1, HEAD_DIM), lambda b, g, r: (layer, 0, 0)),
        ],
        out_specs=(
            pl.BlockSpec((rows, wv), lambda b, g, r: (b * nrb + r, g)),
            pl.BlockSpec((1, 2 * n_pairs, HEAD_DIM, HEAD_DIM), lambda b, g, r: (b, g, 0, 0)),
            pl.BlockSpec((1, keep, wq), lambda b, g, r: (b, 0, g)),
            pl.BlockSpec((1, keep, wq), lambda b, g, r: (b, 0, g)),
            pl.BlockSpec((1, keep, wv), lambda b, g, r: (b, 0, g)),
        ),
        out_shape=out_shapes,
        scratch_shapes=[
            pltpu.VMEM((SUBLANES + rows, wq), F32),
            pltpu.VMEM((SUBLANES + rows, wq), F32),
            pltpu.VMEM((SUBLANES + rows, wv), F32),
            pltpu.VMEM((rows, wq), BF16),
            pltpu.VMEM((rows, wq), BF16),
            pltpu.VMEM((rows, wv), F32),
            pltpu.VMEM((rows, LANES), F32),
            pltpu.VMEM((n_chunks, n_pairs // 2, CHUNK, QUAD * CHUNK), BF16),
            pltpu.VMEM((n_chunks, n_pairs // 2, CHUNK, QUAD * CHUNK), BF16),
        ],
        compiler_params=_cparams("arbitrary", "arbitrary", "arbitrary"),
        name="delta_prompt",
    )(proj, proj, proj, proj, gates, gates_rows, conv_w, conv_w, conv_w, o_norm_w)


N_SAMPLE_INPUTS = 14


def _delta_sample_kernel(*refs, aliased, **static):
    s_out_ref = refs[N_SAMPLE_INPUTS + (1 if aliased else 0) + 1]
    fill = pl.program_id(2)

    @pl.when(fill == 0)
    def _():
        _delta_sample_compute(*refs, aliased=aliased, **static)

    @pl.when(fill > 0)
    def _():
        s_out_ref[...] = jnp.zeros(s_out_ref.shape, F32)


def _delta_sample_compute(*refs, n_pairs, n_seqs, dec_seq, n_heads, aliased):
    (q_ref, k_ref, v_ref, z_ref, cq_ref, ck_ref, cv_ref, gc_ref, gr_ref, wq_ref, wk_ref, wv_ref,
     onw_ref, s_in_ref) = refs[:N_SAMPLE_INPUTS]
    refs = refs[N_SAMPLE_INPUTS + 1:] if aliased else refs[N_SAMPLE_INPUTS:]
    y_ref, s_out_ref, ccq_ref, cck_ref, ccv_ref, eq_ref, ek_ref, ev_ref = refs
    c = n_seqs * dec_seq
    pg = pl.program_id(1)
    hist = SUBLANES

    def conv(x_ref, cache_ref, ext_ref, w_ref, cc_ref):
        cols = ext_ref.shape[2]
        ext_ref[:, 0:hist, :] = cache_ref[...]
        ext_ref[:, hist:hist + dec_seq, :] = x_ref[...].reshape(n_seqs, dec_seq, cols)
        keep = CONV_WIDTH - 1
        cc_ref[...] = ext_ref[:, hist + dec_seq - keep:hist + dec_seq, :]
        return _conv_taps(ext_ref, w_ref, hist, dec_seq).reshape(c, cols)

    xq = conv(q_ref, cq_ref, eq_ref, wq_ref, ccq_ref)
    xk = conv(k_ref, ck_ref, ek_ref, wk_ref, cck_ref)
    xv = conv(v_ref, cv_ref, ev_ref, wv_ref, ccv_ref)

    shift = lax.rem(LANES - 2 * n_pairs * pg, LANES)
    g_all = pltpu.roll(gc_ref[...], shift, 1)
    lane_wide = lax.broadcasted_iota(jnp.int32, (1, 2 * HEAD_DIM), 1)
    row_seq = lax.broadcasted_iota(jnp.int32, (c, 2 * HEAD_DIM), 0) // dec_seq
    pairs = range(n_pairs)
    seqs = range(n_seqs)
    hd = lambda p: slice(p * HEAD_DIM, (p + 1) * HEAD_DIM)

    qns = [_l2norm(xq[:, hd(p)]) * (HEAD_DIM ** -0.5) for p in pairs]
    kns = [_l2norm(xk[:, hd(p)]) for p in pairs]
    knb = [k.astype(BF16) for k in kns]

    groups, incl, strict = _quad_masks(c, dec_seq)
    qnb = [q.astype(BF16) for q in qns]
    grams = [_gram_quad(knb[2 * qd], qnb[2 * qd], knb[2 * qd + 1], qnb[2 * qd + 1])
             for qd in range(n_pairs // 2)]
    gcrs = [gr_ref[0, pl.ds(n_pairs * pg + p, 1), :] for p in pairs]
    l_cats, pis = [], []
    for qd in range(n_pairs // 2):
        ab = grams[qd]
        gcr = jnp.concatenate([gcrs[2 * qd], gcrs[2 * qd + 1]], axis=1)
        beta = _quad_select(g_all, QUAD * qd, groups)
        gc = _quad_select(g_all, n_heads + QUAD * qd, groups)
        dec = jnp.exp(jnp.where(incl, gc - gcr, NEG_BIG))
        pis.append((ab[c:] * dec).astype(BF16))
        l_cats.append(jnp.where(strict, beta * ab[:c] * dec, 0.0))
    pms = [pm.astype(BF16) for pm in _unit_lower_inverses_minus_identity(l_cats, dec_seq, groups)]
    half = lambda mats, p: mats[p // 2][:, (p % 2) * 2 * c:(p % 2 + 1) * 2 * c]

    s_cat = lambda p, s: jnp.concatenate([s_in_ref[s, 2 * p], s_in_ref[s, 2 * p + 1]], axis=1)
    kqs = [[jnp.dot(jnp.concatenate([kns[p][s * dec_seq:(s + 1) * dec_seq],
                                     qns[p][s * dec_seq:(s + 1) * dec_seq]], axis=0).astype(BF16),
                    s_cat(p, s).astype(BF16), preferred_element_type=F32) for s in seqs] for p in pairs]
    k_s = [jnp.concatenate([kqs[p][s][:dec_seq] for s in seqs], axis=0) for p in pairs]
    q_s = [jnp.concatenate([kqs[p][s][dec_seq:] for s in seqs], axis=0) for p in pairs]

    def last_gc(p, s):
        last = (s + 1) * dec_seq - 1
        return jnp.where(lane_wide < HEAD_DIM, gcrs[p][:, last:last + 1], gcrs[p][:, c + last:c + last + 1])

    gls = [[last_gc(p, s) for s in seqs] for p in pairs]
    beta_w = [_head_cols(g_all, 2 * p, HEAD_DIM) for p in pairs]
    gc_w = [_head_cols(g_all, n_heads + 2 * p, HEAD_DIM) for p in pairs]
    egc_w = [jnp.exp(g) for g in gc_w]
    rs = [beta_w[p] * (xv[:, 2 * p * HEAD_DIM:(2 * p + 2) * HEAD_DIM] - egc_w[p] * k_s[p]) for p in pairs]
    us = [rs[p] + jnp.dot(half(pms, p), _blockdiag_wide(rs[p]), preferred_element_type=F32) for p in pairs]
    os_ = [egc_w[p] * q_s[p] + jnp.dot(half(pis, p), _blockdiag_wide(us[p]), preferred_element_type=F32)
           for p in pairs]
    for p in pairs:
        gl_rows = jnp.concatenate([jnp.broadcast_to(g, (dec_seq, 2 * HEAD_DIM)) for g in gls[p]], axis=0)
        w = jnp.exp(gl_rows - gc_w[p]) * us[p]
        for s in seqs:
            w_s = jnp.where(row_seq == s, w, 0.0).astype(BF16)
            s_new = s_cat(p, s) * jnp.exp(gls[p][s]) + lax.dot_general(
                knb[p], w_s, (((0,), (0,)), ((), ())), preferred_element_type=F32)
            s_out_ref[s, 2 * p] = s_new[:, :HEAD_DIM]
            s_out_ref[s, 2 * p + 1] = s_new[:, HEAD_DIM:]
    for p in pairs:
        for e in range(2):
            hs = slice((2 * p + e) * HEAD_DIM, (2 * p + e + 1) * HEAD_DIM)
            y_ref[:, hs] = _gated_out(os_[p][:, e * HEAD_DIM:(e + 1) * HEAD_DIM], z_ref[:, hs], onw_ref[...])


def _delta_sample(proj, gates, gates_rows, conv_cache_pad, state, prev_state_out, conv_w, o_norm_w, layer,
                  row_off, batch, dec_seq, n_heads, d_inner, n_pairs=4, n_seqs=8):
    qk_dim = n_heads // 2 * HEAD_DIM
    c = n_seqs * dec_seq
    wq, wv = n_pairs * HEAD_DIM, 2 * n_pairs * HEAD_DIM
    n_pg = n_heads // (2 * n_pairs)
    rb0 = row_off // c
    aliased = prev_state_out is not None
    kern = functools.partial(_delta_sample_kernel, n_pairs=n_pairs, n_seqs=n_seqs, dec_seq=dec_seq,
                             n_heads=n_heads, aliased=aliased)
    row_map = lambda col0: (lambda i, g, f: (rb0 + i, col0 + g))
    keep = CONV_WIDTH - 1
    in_specs = [
        pl.BlockSpec((c, wq), row_map(0)),
        pl.BlockSpec((c, wq), row_map(qk_dim // wq)),
        pl.BlockSpec((c, wv), row_map(2 * qk_dim // wv)),
        pl.BlockSpec((c, wv), row_map((2 * qk_dim + d_inner) // wv)),
        pl.BlockSpec((n_seqs, SUBLANES, wq), lambda i, g, f: (i, 0, g)),
        pl.BlockSpec((n_seqs, SUBLANES, wq), lambda i, g, f: (i, 0, qk_dim // wq + g)),
        pl.BlockSpec((n_seqs, SUBLANES, wv), lambda i, g, f: (i, 0, 2 * qk_dim // wv + g)),
        pl.BlockSpec((c, LANES), lambda i, g, f: (rb0 + i, 0)),
        pl.BlockSpec((1, n_heads // 2, 2 * c), lambda i, g, f: (i, 0, 0)),
        pl.BlockSpec((None, CONV_WIDTH, wq), lambda i, g, f: (layer, 0, g)),
        pl.BlockSpec((None, CONV_WIDTH, wq), lambda i, g, f: (layer, 0, qk_dim // wq + g)),
        pl.BlockSpec((None, CONV_WIDTH, wv), lambda i, g, f: (layer, 0, 2 * qk_dim // wv + g)),
        pl.BlockSpec((None, 1, HEAD_DIM), lambda i, g, f: (layer, 0, 0)),
        pl.BlockSpec((None, n_seqs, 2 * n_pairs, HEAD_DIM, HEAD_DIM), lambda i, g, f: (layer, i, g, 0, 0)),
    ]
    args = [proj, proj, proj, proj, conv_cache_pad, conv_cache_pad, conv_cache_pad, gates, gates_rows,
            conv_w, conv_w, conv_w, o_norm_w, state]
    aliases = {}
    if aliased:
        in_specs.append(pl.BlockSpec(memory_space=pl.ANY))
        args.append(prev_state_out)
        aliases = {N_SAMPLE_INPUTS: 1}
    n_layers = state.shape[0]
    n_fill = 1 if aliased else n_layers
    out_shapes = (
        jax.ShapeDtypeStruct((batch * dec_seq, d_inner), BF16),
        jax.ShapeDtypeStruct(state.shape, F32),
        jax.ShapeDtypeStruct((batch, keep, qk_dim), F32),
        jax.ShapeDtypeStruct((batch, keep, qk_dim), F32),
        jax.ShapeDtypeStruct((batch, keep, d_inner), F32),
    )
    return pl.pallas_call(
        kern,
        grid=(batch // n_seqs, n_pg, n_fill),
        in_specs=in_specs,
        out_specs=(
            pl.BlockSpec((c, wv), lambda i, g, f: (i, g)),
            pl.BlockSpec((None, n_seqs, 2 * n_pairs, HEAD_DIM, HEAD_DIM),
                         lambda i, g, f: ((layer + f) % n_layers, i, g, 0, 0)),
            pl.BlockSpec((n_seqs, keep, wq), lambda i, g, f: (i, 0, g)),
            pl.BlockSpec((n_seqs, keep, wq), lambda i, g, f: (i, 0, g)),
            pl.BlockSpec((n_seqs, keep, wv), lambda i, g, f: (i, 0, g)),
        ),
        out_shape=out_shapes,
        scratch_shapes=[
            pltpu.VMEM((n_seqs, SUBLANES + dec_seq, wq), F32),
            pltpu.VMEM((n_seqs, SUBLANES + dec_seq, wq), F32),
            pltpu.VMEM((n_seqs, SUBLANES + dec_seq, wv), F32),
        ],
        input_output_aliases=aliases,
        compiler_params=_cparams("arbitrary", "arbitrary", "arbitrary"),
        name="delta_sample",
    )(*args)


def _window_sums(x, win, axis):
    s = x
    w = 1
    while w < win:
        s = s + pltpu.roll(s, w, axis)
        w *= 2
    return s


def _pool_finish(pooled, z, wg_ref, sc_ref):
    mixed = jnp.dot(pooled.astype(BF16), wg_ref[...], preferred_element_type=F32)
    return (mixed * sc_ref[...] * _silu(z)).astype(BF16)


def _pool_prompt_kernel(u_ref, prev_ref, z_ref, wg_ref, sc_ref, y_ref, cache_ref, ext_ref, *, tt, nt):
    g = pl.program_id(0)
    t = pl.program_id(2)
    hist = POOL_BUF + 1

    @pl.when(t == 0)
    def _():
        ext_ref[0:hist, :] = jnp.zeros((hist, ext_ref.shape[1]), F32)

    @pl.when(t > 0)
    def _():
        ext_ref[0:hist, :] = prev_ref[...]

    ext_ref[hist:hist + tt, :] = u_ref[...]
    pos = t * tt + lax.broadcasted_iota(jnp.int32, (tt, 1), 0)
    for gi, win in enumerate(POOL_WINDOWS):
        @pl.when(g == gi)
        def _(win=win):
            acc = _window_sums(ext_ref[...], win, 0)[hist:]
            cnt = jnp.minimum(pos + 1, win).astype(F32)
            y_ref[...] = _pool_finish(acc / cnt - u_ref[...], z_ref[...], wg_ref, sc_ref)

    @pl.when(t == nt - 1)
    def _():
        cache_ref[0] = ext_ref[hist + tt - POOL_BUF:hist + tt, :]


def _pool_prompt(proj, w_group, scale, layer, batch, seq, d_inner, tt=512):
    gd = d_inner // len(POOL_WINDOWS)
    nt = seq // tt
    hist = POOL_BUF + 1
    kern = functools.partial(_pool_prompt_kernel, tt=tt, nt=nt)
    n_g = len(POOL_WINDOWS)
    return pl.pallas_call(
        kern,
        grid=(n_g, batch, nt),
        in_specs=[
            pl.BlockSpec((tt, gd), lambda g, b, t: (b * nt + t, g)),
            pl.BlockSpec((hist, gd), lambda g, b, t: (jnp.maximum((b * nt + t) * (tt // hist) - 1, 0), g)),
            pl.BlockSpec((tt, gd), lambda g, b, t: (b * nt + t, n_g + g)),
            pl.BlockSpec((None, None, gd, gd), lambda g, b, t: (layer, g, 0, 0)),
            pl.BlockSpec((None, 1, gd), lambda g, b, t: (layer, 0, g)),
        ],
        out_specs=(
            pl.BlockSpec((tt, gd), lambda g, b, t: (b * nt + t, g)),
            pl.BlockSpec((1, POOL_BUF, gd), lambda g, b, t: (b, 0, g)),
        ),
        out_shape=(
            jax.ShapeDtypeStruct((batch * seq, d_inner), BF16),
            jax.ShapeDtypeStruct((batch, POOL_BUF, d_inner), F32),
        ),
        scratch_shapes=[pltpu.VMEM((hist + tt, gd), F32)],
        compiler_params=_cparams("arbitrary", "arbitrary", "arbitrary"),
        name="pool_prompt",
    )(proj, proj, proj, w_group, scale)


def _pool_sample_kernel(u_ref, c_ref, z_ref, wg_ref, sc_ref, y_ref, cache_ref, ext_ref, *, sb, dec_seq):
    g = pl.program_id(0)
    hist = POOL_BUF + 1
    gd = ext_ref.shape[2]
    ext_ref[:, 0:SUBLANES, :] = jnp.zeros((sb, SUBLANES, gd), F32)
    ext_ref[:, hist - POOL_BUF:hist, :] = c_ref[...]
    ext_ref[:, hist:hist + dec_seq, :] = u_ref[...].reshape(sb, dec_seq, gd)
    pos = PAST_LEN + lax.broadcasted_iota(jnp.int32, (1, dec_seq, 1), 1)
    for gi, win in enumerate(POOL_WINDOWS):
        @pl.when(g == gi)
        def _(win=win):
            x = ext_ref[:, hist:hist + dec_seq, :]
            acc = _window_sums(ext_ref[...], win, 1)[:, hist:, :]
            cnt = jnp.minimum(pos + 1, win).astype(F32)
            pooled = (acc / cnt - x).reshape(sb * dec_seq, gd)
            y_ref[...] = _pool_finish(pooled, z_ref[...], wg_ref, sc_ref)

    cache_ref[...] = ext_ref[:, hist + dec_seq - POOL_BUF:hist + dec_seq, :]


def _pool_sample(proj, cache, w_group, scale, layer, row_off, batch, dec_seq, d_inner, sb=16):
    gd = d_inner // len(POOL_WINDOWS)
    n_g = len(POOL_WINDOWS)
    rows = sb * dec_seq
    rb0 = row_off // rows
    hist = POOL_BUF + 1
    kern = functools.partial(_pool_sample_kernel, sb=sb, dec_seq=dec_seq)
    return pl.pallas_call(
        kern,
        grid=(n_g, batch // sb),
        in_specs=[
            pl.BlockSpec((rows, gd), lambda g, i: (rb0 + i, g)),
            pl.BlockSpec((None, sb, POOL_BUF, gd), lambda g, i: (layer, i, 0, g)),
            pl.BlockSpec((rows, gd), lambda g, i: (rb0 + i, n_g + g)),
            pl.BlockSpec((None, None, gd, gd), lambda g, i: (layer, g, 0, 0)),
            pl.BlockSpec((None, 1, gd), lambda g, i: (layer, 0, g)),
        ],
        out_specs=(
            pl.BlockSpec((rows, gd), lambda g, i: (i, g)),
            pl.BlockSpec((sb, POOL_BUF, gd), lambda g, i: (i, 0, g)),
        ),
        out_shape=(
            jax.ShapeDtypeStruct((batch * dec_seq, d_inner), BF16),
            jax.ShapeDtypeStruct((batch, POOL_BUF, d_inner), F32),
        ),
        scratch_shapes=[pltpu.VMEM((sb, hist + dec_seq, gd), F32)],
        compiler_params=_cparams("arbitrary", "arbitrary"),
        name="pool_sample",
    )(proj, cache, proj, w_group, scale)


def kernel(x_prompt, x_sample, state_delta, cache_conv, cache_pool, norm_w, final_norm_w, delta_w_in,
           delta_conv_w, delta_a_log, delta_dt_bias, delta_o_norm_w, delta_w_out, pool_w_in, pool_w_group,
           pool_scale, pool_w_out):
    batch, seq, d_model = x_prompt.shape
    dec_batch, dec_seq, _ = x_sample.shape
    depth = norm_w.shape[0]
    n_heads = state_delta.shape[2]
    d_inner = n_heads * HEAD_DIM
    conv_dim = cache_conv.shape[3]
    qk_dim = (conv_dim - d_inner) // 2
    mp, ms = batch * seq, dec_batch * dec_seq
    m = mp + ms
    n_main = conv_dim + d_inner
    n_seqs = CHUNK // dec_seq

    h = jnp.concatenate([x_prompt.reshape(mp, d_model), x_sample.reshape(ms, d_model)], axis=0)

    delta_w_out_b = delta_w_out.astype(BF16)
    pool_w_out_b = pool_w_out.astype(BF16)
    pool_w_group_b = pool_w_group.astype(BF16)
    delta_w_in_t = jnp.swapaxes(delta_w_in, 1, 2)
    w_tail_t = jnp.pad(delta_w_in_t[:, n_main:, :], ((0, 0), (0, LANES - 2 * n_heads), (0, 0)))
    onw = delta_o_norm_w.reshape(-1, 1, HEAD_DIM)
    pool_scale3 = pool_scale.reshape(pool_scale.shape[0], 1, d_inner)
    conv_cache_pad = jnp.pad(cache_conv, ((0, 0), (0, 0), (SUBLANES - (CONV_WIDTH - 1), 0), (0, 0)))

    sd_p, cc_p, cp_p, cc_s, cp_s = [], [], [], [], []
    sd_s = None
    for i in range(depth):
        j = i // 2
        xn = _rmsnorm(h, norm_w[i], BF16, m)
        if i % 2 == 0:
            proj = _matmul(xn, delta_w_in_t, j, n_main, tm=3072, tn=512, w_transposed=True)
            tail = _matmul(xn, w_tail_t, j, LANES, tm=3072, tn=LANES, w_transposed=True)
            gates = _gates(tail, delta_a_log[j], delta_dt_bias[j], mp, n_heads, dec_seq)
            gcum = gates[:, n_heads:2 * n_heads]
            rows_p = gcum[:mp].reshape(batch, seq // CHUNK, CHUNK, n_heads // 2, 2)
            rows_p = rows_p.transpose(0, 1, 3, 4, 2).reshape(batch, seq // (CHUNKS_PER_STEP * CHUNK), CHUNKS_PER_STEP * (n_heads // 2), 2 * CHUNK)
            rows_s = gcum[mp:].reshape(ms // CHUNK, CHUNK, n_heads // 2, 2)
            rows_s = rows_s.transpose(0, 2, 3, 1).reshape(ms // CHUNK, n_heads // 2, 2 * CHUNK)
            y_p, sp, cq, ck, cv = _delta_prompt(proj, gates, rows_p, delta_conv_w, onw, j, batch, seq,
                                                n_heads, d_inner)
            y_s, sd_s, cqs, cks, cvs = _delta_sample(proj, gates, rows_s, conv_cache_pad[j], state_delta, sd_s,
                                                     delta_conv_w, onw, j, mp, dec_batch, dec_seq, n_heads,
                                                     d_inner, n_seqs=n_seqs)
            sd_p.append(sp)
            cc_p.append(jnp.concatenate([cq, ck, cv], axis=-1))
            cc_s.append(jnp.concatenate([cqs, cks, cvs], axis=-1))
            w_out, lw = delta_w_out_b, j
        else:
            proj = _matmul(xn, pool_w_in, j, 2 * d_inner, tm=3072, tn=512)
            y_p, pp = _pool_prompt(proj, pool_w_group_b, pool_scale3, j, batch, seq, d_inner)
            y_s, ps = _pool_sample(proj, cache_pool, pool_w_group_b, pool_scale3, j, mp, dec_batch, dec_seq,
                                   d_inner)
            cp_p.append(pp)
            cp_s.append(ps)
            w_out, lw = pool_w_out_b, j
        h = _matmul_residual(y_p, y_s, w_out, lw, h)

    y_prompt = _rmsnorm(h, final_norm_w, F32, mp, 0).reshape(batch, seq, d_model)
    y_sample = _rmsnorm(h, final_norm_w, F32, ms, mp).reshape(dec_batch, dec_seq, d_model)
    return (y_prompt, y_sample, jnp.stack(sd_p), jnp.stack(cc_p), jnp.stack(cp_p),
            sd_s, jnp.stack(cc_s), jnp.stack(cp_s))
```

```python
import functools

import jax
import jax.numpy as jnp
from jax import lax
from jax.experimental import pallas as pl
from jax.experimental.pallas import tpu as pltpu

F32 = jnp.float32
BF16 = jnp.bfloat16

HEAD_DIM = 128
CONV_WIDTH = 4
CHUNK = 64
CHUNKS_PER_STEP = 4
POOL_WINDOWS = (2, 4, 8, 16)
POOL_BUF = max(POOL_WINDOWS) - 1
PAST_LEN = 16384
NORM_EPS = 1e-6
NEG_BIG = -1e30

LANES = 128
SUBLANES = 8
VMEM_LIMIT = 56 * 1024 * 1024


def _cparams(*sem):
    return pltpu.CompilerParams(dimension_semantics=sem, vmem_limit_bytes=VMEM_LIMIT)


def _silu(x):
    return x * jax.nn.sigmoid(x)


def _rmsnorm_kernel(x_ref, w_ref, o_ref):
    x = x_ref[...]
    y = x * lax.rsqrt(jnp.mean(x * x, axis=-1, keepdims=True) + NORM_EPS)
    o_ref[...] = (y * w_ref[...]).astype(o_ref.dtype)


def _rmsnorm_pair_kernel(xp_ref, xs_ref, w_ref, o_ref, *, n_prompt_blocks):
    i = pl.program_id(0)

    @pl.when(i < n_prompt_blocks)
    def _():
        _rmsnorm_kernel(xp_ref, w_ref, o_ref)

    @pl.when(i >= n_prompt_blocks)
    def _():
        _rmsnorm_kernel(xs_ref, w_ref, o_ref)


def _rmsnorm_pair(x_p, x_s, w, out_dtype, tm=512):
    d = x_p.shape[1]
    npb = x_p.shape[0] // tm
    m = x_p.shape[0] + x_s.shape[0]
    return pl.pallas_call(
        functools.partial(_rmsnorm_pair_kernel, n_prompt_blocks=npb),
        grid=(m // tm,),
        in_specs=[pl.BlockSpec((tm, d), lambda i: (jnp.minimum(i, npb - 1), 0)),
                  pl.BlockSpec((tm, d), lambda i: (jnp.maximum(i - npb, 0), 0)),
                  pl.BlockSpec((1, d), lambda i: (0, 0))],
        out_specs=pl.BlockSpec((tm, d), lambda i: (i, 0)),
        out_shape=jax.ShapeDtypeStruct((m, d), out_dtype),
        compiler_params=_cparams("arbitrary"),
        name="rmsnorm",
    )(x_p, x_s, w.reshape(1, d))


def _rmsnorm(x, w, out_dtype, rows, row_off=0, tm=512):
    d = x.shape[1]
    off = row_off // tm
    return pl.pallas_call(
        _rmsnorm_kernel,
        grid=(rows // tm,),
        in_specs=[pl.BlockSpec((tm, d), lambda i: (i + off, 0)),
                  pl.BlockSpec((1, d), lambda i: (0, 0))],
        out_specs=pl.BlockSpec((tm, d), lambda i: (i, 0)),
        out_shape=jax.ShapeDtypeStruct((rows, d), out_dtype),
        compiler_params=_cparams("arbitrary"),
        name="rmsnorm",
    )(x, w.reshape(1, d))


def _matmul_kernel(x_ref, w_ref, o_ref, *, w_transposed):
    w = w_ref[...].astype(BF16)
    contract = (((1,), (1 if w_transposed else 0,)), ((), ()))
    o_ref[...] = lax.dot_general(x_ref[...], w, contract, preferred_element_type=F32)


def _matmul(x, w, layer, n_cols, tm, tn, w_transposed=False):
    m, k = x.shape
    if w_transposed:
        w_spec = pl.BlockSpec((None, tn, k), lambda i, j: (layer, j, 0))
    else:
        w_spec = pl.BlockSpec((None, k, tn), lambda i, j: (layer, 0, j))
    return pl.pallas_call(
        functools.partial(_matmul_kernel, w_transposed=w_transposed),
        grid=(m // tm, n_cols // tn),
        in_specs=[pl.BlockSpec((tm, k), lambda i, j: (i, 0)), w_spec],
        out_specs=pl.BlockSpec((tm, tn), lambda i, j: (i, j)),
        out_shape=jax.ShapeDtypeStruct((m, n_cols), F32),
        compiler_params=_cparams("arbitrary", "arbitrary"),
        name="in_proj",
    )(x, w)


def _matmul_res_kernel(yp_ref, ys_ref, w_ref, *refs, n_prompt_blocks):
    hp_ref, hs_ref, o_ref = refs if len(refs) == 3 else (refs[0], refs[0], refs[1])
    i = pl.program_id(0)
    w = w_ref[...]

    @pl.when(i < n_prompt_blocks)
    def _():
        o_ref[...] = hp_ref[...] + jnp.dot(yp_ref[...], w, preferred_element_type=F32)

    @pl.when(i >= n_prompt_blocks)
    def _():
        o_ref[...] = hs_ref[...] + jnp.dot(ys_ref[...], w, preferred_element_type=F32)


def _matmul_residual(y_p, y_s, w, layer, h, tm=1024, tn=512):
    rows_p, k = y_p.shape
    npb = rows_p // tm
    prompt_rows = lambda i, j: (jnp.minimum(i, npb - 1), j)
    sample_rows = lambda i, j: (jnp.maximum(i - npb, 0), j)
    if isinstance(h, tuple):
        m, n = h[0].shape[0] + h[1].shape[0], h[0].shape[1]
        h_specs = [pl.BlockSpec((tm, tn), prompt_rows), pl.BlockSpec((tm, tn), sample_rows)]
        aliases = {}
    else:
        m, n = h.shape
        h, h_specs, aliases = (h,), [pl.BlockSpec((tm, tn), lambda i, j: (i, j))], {3: 0}
    return pl.pallas_call(
        functools.partial(_matmul_res_kernel, n_prompt_blocks=npb),
        grid=(m // tm, n // tn),
        in_specs=[pl.BlockSpec((tm, k), lambda i, j: (jnp.minimum(i, npb - 1), 0)),
                  pl.BlockSpec((tm, k), lambda i, j: (jnp.maximum(i - npb, 0), 0)),
                  pl.BlockSpec((None, k, tn), lambda i, j: (layer, 0, j))] + h_specs,
        out_specs=pl.BlockSpec((tm, tn), lambda i, j: (i, j)),
        out_shape=jax.ShapeDtypeStruct((m, n), F32),
        input_output_aliases=aliases,
        compiler_params=_cparams("arbitrary", "arbitrary"),
        name="out_proj",
    )(y_p, y_s, w, *h)


def _gates_kernel(t_ref, alog_ref, dt_ref, o_ref, *, n_prompt_blocks, n_heads, dec_seq):
    i = pl.program_id(0)
    x = t_ref[...]
    lane = lax.broadcasted_iota(jnp.int32, x.shape, 1)
    row = lax.broadcasted_iota(jnp.int32, x.shape, 0)
    beta = jax.nn.sigmoid(x)
    a = x + dt_ref[...]
    softplus = jnp.maximum(a, 0.0) + jnp.log(1.0 + jnp.exp(-jnp.abs(a)))
    g = -jnp.exp(alog_ref[...]) * softplus
    seg = jnp.where(i < n_prompt_blocks, CHUNK, dec_seq)
    pos = row & (seg - 1)
    cs = g
    s = 1
    while s < CHUNK:
        cs = cs + jnp.where(pos >= s, pltpu.roll(cs, s, 0), 0.0)
        s *= 2
    o_ref[...] = jnp.where(lane < n_heads, beta, jnp.where(lane < 2 * n_heads, cs, 0.0))


def _gates(tail, a_log, dt_bias, n_prompt_rows, n_heads, dec_seq, tm=1024):
    m = tail.shape[0]
    pad = lambda v: jnp.zeros((1, LANES), F32).at[0, n_heads:2 * n_heads].set(v.astype(F32))
    kern = functools.partial(_gates_kernel, n_prompt_blocks=n_prompt_rows // tm, n_heads=n_heads,
                             dec_seq=dec_seq)
    return pl.pallas_call(
        kern,
        grid=(m // tm,),
        in_specs=[pl.BlockSpec((tm, LANES), lambda i: (i, 0)),
                  pl.BlockSpec((1, LANES), lambda i: (0, 0)),
                  pl.BlockSpec((1, LANES), lambda i: (0, 0))],
        out_specs=pl.BlockSpec((tm, LANES), lambda i: (i, 0)),
        out_shape=jax.ShapeDtypeStruct((m, LANES), F32),
        compiler_params=_cparams("arbitrary"),
        name="delta_gates",
    )(tail, pad(a_log), pad(dt_bias))


QUAD = 4


def _blockdiag(y, masks):
    yb = y.astype(BF16)
    zero = jnp.zeros_like(yb)
    return jnp.concatenate([jnp.where(m, yb, zero) for m in masks], axis=0)


def _blockdiag_wide(r):
    rb = r.astype(BF16)
    z = jnp.zeros_like(rb[:, :HEAD_DIM])
    return jnp.concatenate([jnp.concatenate([rb[:, :HEAD_DIM], z], axis=1),
                            jnp.concatenate([z, rb[:, HEAD_DIM:]], axis=1)], axis=0)


def _unit_lower_inverses_minus_identity(l_cats, seg, masks):
    c = l_cats[0].shape[0]
    dot = lambda a, b: jnp.dot(a.astype(BF16), _blockdiag(b, masks), preferred_element_type=F32)
    ps = [-l for l in l_cats]
    qs = [dot(l, l) for l in l_cats]
    n_steps = seg.bit_length() - 2
    for step in range(n_steps):
        if step < n_steps - 1:
            pqs = [dot(jnp.concatenate([p, q], axis=0), q) for p, q in zip(ps, qs)]
            ps = [p + q + pq[:c] for p, q, pq in zip(ps, qs, pqs)]
            qs = [pq[c:] for pq in pqs]
        else:
            pqs = [dot(p, q) for p, q in zip(ps, qs)]
            ps = [p + q + pq for p, q, pq in zip(ps, qs, pqs)]
    return ps


def _gram_quad(kn_a, qn_a, kn_b, qn_b):
    zero = jnp.zeros_like(kn_a)
    lhs = jnp.concatenate([jnp.concatenate([kn_a, kn_b], axis=1),
                           jnp.concatenate([qn_a, qn_b], axis=1)], axis=0)
    ka = jnp.concatenate([kn_a, zero], axis=1)
    kb = jnp.concatenate([zero, kn_b], axis=1)
    rhs = jnp.concatenate([ka, ka, kb, kb], axis=0)
    return lax.dot_general(lhs, rhs, (((1,), (1,)), ((), ())), preferred_element_type=F32)


def _quad_masks(c, seg):
    lane = lax.broadcasted_iota(jnp.int32, (c, QUAD * c), 1)
    ii = lax.broadcasted_iota(jnp.int32, (c, QUAD * c), 0)
    jj = lane & (c - 1)
    groups = [(lane // c) == g for g in range(QUAD)]
    incl = (ii >= jj) if seg == c else ((ii >= jj) & ((ii // seg) == (jj // seg)))
    return groups, incl, ii > jj


def _quad_select(g_all, base, groups):
    out = g_all[:, base + QUAD - 1:base + QUAD]
    for g in range(QUAD - 2, -1, -1):
        out = jnp.where(groups[g], g_all[:, base + g:base + g + 1], out)
    return out


def _head_cols(g_all, idx, width):
    c = g_all.shape[0]
    return jnp.concatenate([jnp.broadcast_to(g_all[:, idx:idx + 1], (c, width)),
                            jnp.broadcast_to(g_all[:, idx + 1:idx + 2], (c, width))], axis=1)


def _gated_out(o, z, onw):
    on = o * lax.rsqrt(jnp.mean(o * o, axis=-1, keepdims=True) + NORM_EPS) * onw
    return (on * _silu(z)).astype(BF16)


def _conv_taps(ext_ref, w_ref, start):
    assert CONV_WIDTH == 4 and start >= CONV_WIDTH - 1 and start % SUBLANES == 0
    axis = len(ext_ref.shape) - 2
    x = ext_ref[...]
    x1 = pltpu.roll(x, 1, axis)
    a = w_ref[3:4, :] * x + w_ref[2:3, :] * x1
    b = w_ref[1:2, :] * x + w_ref[0:1, :] * x1
    y = a + pltpu.roll(b, 2, axis)
    y = y[start:] if axis == 0 else y[:, start:]
    return _silu(y)


def _l2norm(x):
    return x * lax.rsqrt(jnp.sum(x * x, axis=-1, keepdims=True) + NORM_EPS)


def _delta_prompt_kernel(q_ref, k_ref, v_ref, z_ref, gc_ref, gr_ref, wq_ref, wk_ref, wv_ref, onw_ref,
                         y_ref, s_ref, ccq_ref, cck_ref, ccv_ref,
                         eq_ref, ek_ref, ev_ref, qn_ref, kn_ref, vv_ref, gt_ref, p_ref, pi_ref,
                         *, n_pairs, n_chunks, n_heads, batch_chunks):
    c = CHUNK
    rows = n_chunks * c
    pg = pl.program_id(1)
    cb = pl.program_id(2)
    hist = SUBLANES

    @pl.when(cb == 0)
    def _():
        eq_ref[0:hist, :] = jnp.zeros((hist, eq_ref.shape[1]), F32)
        ek_ref[0:hist, :] = jnp.zeros((hist, ek_ref.shape[1]), F32)
        ev_ref[0:hist, :] = jnp.zeros((hist, ev_ref.shape[1]), F32)
        s_ref[...] = jnp.zeros(s_ref.shape, F32)

    eq_ref[hist:hist + rows, :] = q_ref[...]
    ek_ref[hist:hist + rows, :] = k_ref[...]
    ev_ref[hist:hist + rows, :] = v_ref[...]
    xq = _conv_taps(eq_ref, wq_ref, hist)
    xk = _conv_taps(ek_ref, wk_ref, hist)
    vv_ref[...] = _conv_taps(ev_ref, wv_ref, hist)
    for p in range(n_pairs):
        sl = slice(p * HEAD_DIM, (p + 1) * HEAD_DIM)
        qn_ref[:, sl] = (_l2norm(xq[:, sl]) * (HEAD_DIM ** -0.5)).astype(BF16)
        kn_ref[:, sl] = _l2norm(xk[:, sl]).astype(BF16)
    keep = CONV_WIDTH - 1
    ccq_ref[0] = eq_ref[hist + rows - keep:hist + rows, :]
    cck_ref[0] = ek_ref[hist + rows - keep:hist + rows, :]
    ccv_ref[0] = ev_ref[hist + rows - keep:hist + rows, :]
    eq_ref[0:hist, :] = eq_ref[rows:rows + hist, :]
    ek_ref[0:hist, :] = ek_ref[rows:rows + hist, :]
    ev_ref[0:hist, :] = ev_ref[rows:rows + hist, :]

    shift = lax.rem(LANES - 2 * n_pairs * pg, LANES)
    gt_ref[...] = pltpu.roll(gc_ref[...], shift, 1)

    n_quads = n_pairs // 2
    groups, incl, strict = _quad_masks(c, c)
    pair_row = lambda ci, p: ci * (n_heads // 2) + n_pairs * pg + p
    for c0 in range(0, n_chunks, batch_chunks):
        l_cats, where = [], []
        for ci in range(c0, c0 + batch_chunks):
            r0 = ci * c
            g_all = gt_ref[r0:r0 + c, :]
            hd = lambda ref, p: ref[r0:r0 + c, p * HEAD_DIM:(p + 1) * HEAD_DIM]
            grams = [_gram_quad(hd(kn_ref, 2 * qd), hd(qn_ref, 2 * qd), hd(kn_ref, 2 * qd + 1),
                                hd(qn_ref, 2 * qd + 1)) for qd in range(n_quads)]
            for qd in range(n_quads):
                ab = grams[qd]
                gcr =jnp.concatenate([gr_ref[0, 0, pl.ds(pair_row(ci, 2 * qd), 1), :],
                                       gr_ref[0, 0, pl.ds(pair_row(ci, 2 * qd + 1), 1), :]], axis=1)
                beta = _quad_select(g_all, QUAD * qd, groups)
                gc = _quad_select(g_all, n_heads + QUAD * qd, groups)
                dec = jnp.exp(jnp.where(incl, gc - gcr, NEG_BIG))
                pi_ref[ci, qd] = (ab[c:] * dec).astype(BF16)
                l_cats.append(jnp.where(strict, beta * ab[:c] * dec, 0.0))
                where.append((ci, qd))
        for (ci, qd), pm in zip(where, _unit_lower_inverses_minus_identity(l_cats, c, groups)):
            p_ref[ci, qd] = pm.astype(BF16)

    lane_wide = lax.broadcasted_iota(jnp.int32, (1, 2 * HEAD_DIM), 1)
    pairs = range(n_pairs)

    def chunk_step(ci, carry):
        r0 = pl.multiple_of(ci * c, c)
        g_all = gt_ref[pl.ds(r0, c), :]
        half = lambda ref, p: ref[ci, p // 2, :, (p % 2) * 2 * c:(p % 2 + 1) * 2 * c]
        kns = [kn_ref[pl.ds(r0, c), p * HEAD_DIM:(p + 1) * HEAD_DIM] for p in pairs]
        qns = [qn_ref[pl.ds(r0, c), p * HEAD_DIM:(p + 1) * HEAD_DIM] for p in pairs]
        s_cat = lambda p: jnp.concatenate([s_ref[0, 2 * p], s_ref[0, 2 * p + 1]], axis=1)
        kqs = [jnp.dot(jnp.concatenate([kns[p], qns[p]], axis=0), s_cat(p).astype(BF16),
                       preferred_element_type=F32) for p in pairs]
        beta_w = [_head_cols(g_all, 2 * p, HEAD_DIM) for p in pairs]
        gc_w = [_head_cols(g_all, n_heads + 2 * p, HEAD_DIM) for p in pairs]
        egc_w = [jnp.exp(g) for g in gc_w]
        rs = [beta_w[p] * (vv_ref[pl.ds(r0, c), 2 * p * HEAD_DIM:(2 * p + 2) * HEAD_DIM] - egc_w[p] * kqs[p][:c])
              for p in pairs]
        us = [rs[p] + jnp.dot(half(p_ref, p), _blockdiag_wide(rs[p]), preferred_element_type=F32)
              for p in pairs]
        os_ = [egc_w[p] * kqs[p][c:] + jnp.dot(half(pi_ref, p), _blockdiag_wide(us[p]),
                                               preferred_element_type=F32) for p in pairs]
        for p in pairs:
            gcr = gr_ref[0, 0, pl.ds(pair_row(ci, p), 1), :]
            gl_w = jnp.where(lane_wide < HEAD_DIM, gcr[:, c - 1:c], gcr[:, 2 * c - 1:2 * c])
            w = (jnp.exp(gl_w - gc_w[p]) * us[p]).astype(BF16)
            s_new = s_cat(p) * jnp.exp(gl_w) + lax.dot_general(
                kns[p], w, (((0,), (0,)), ((), ())), preferred_element_type=F32)
            s_ref[0, 2 * p] = s_new[:, :HEAD_DIM]
            s_ref[0, 2 * p + 1] = s_new[:, HEAD_DIM:]
        for p in pairs:
            for e in range(2):
                hs = slice((2 * p + e) * HEAD_DIM, (2 * p + e + 1) * HEAD_DIM)
                y_ref[pl.ds(r0, c), hs] = _gated_out(os_[p][:, e * HEAD_DIM:(e + 1) * HEAD_DIM],
                                                     z_ref[pl.ds(r0, c), hs], onw_ref[...])
        return carry

    lax.fori_loop(0, n_chunks, chunk_step, 0)


def _delta_prompt(proj, gates, gates_rows, conv_w, o_norm_w, layer, batch, seq, n_heads, d_inner,
                  n_pairs=8, n_chunks=CHUNKS_PER_STEP, batch_chunks=2):
    qk_dim = n_heads // 2 * HEAD_DIM
    rows = n_chunks * CHUNK
    nrb = seq // rows
    wq, wv = n_pairs * HEAD_DIM, 2 * n_pairs * HEAD_DIM
    n_pg = n_heads // (2 * n_pairs)
    kern = functools.partial(_delta_prompt_kernel, n_pairs=n_pairs, n_chunks=n_chunks, n_heads=n_heads,
                             batch_chunks=batch_chunks)
    row_map = lambda col0: (lambda b, g, r: (b * nrb + r, col0 + g))
    keep = CONV_WIDTH - 1
    out_shapes = (
        jax.ShapeDtypeStruct((batch * seq, d_inner), BF16),
        jax.ShapeDtypeStruct((batch, n_heads, HEAD_DIM, HEAD_DIM), F32),
        jax.ShapeDtypeStruct((batch, keep, qk_dim), F32),
        jax.ShapeDtypeStruct((batch, keep, qk_dim), F32),
        jax.ShapeDtypeStruct((batch, keep, d_inner), F32),
    )
    return pl.pallas_call(
        kern,
        grid=(batch, n_pg, nrb),
        in_specs=[
            pl.BlockSpec((rows, wq), row_map(0)),
            pl.BlockSpec((rows, wq), row_map(qk_dim // wq)),
            pl.BlockSpec((rows, wv), row_map(2 * qk_dim // wv)),
            pl.BlockSpec((rows, wv), row_map((2 * qk_dim + d_inner) // wv)),
            pl.BlockSpec((rows, LANES), lambda b, g, r: (b * nrb + r, 0)),
            pl.BlockSpec((1, 1, n_chunks * (n_heads // 2), 2 * CHUNK), lambda b, g, r: (b, r, 0, 0)),
            pl.BlockSpec((None, CONV_WIDTH, wq), lambda b, g, r: (layer, 0, g)),
            pl.BlockSpec((None, CONV_WIDTH, wq), lambda b, g, r: (layer, 0, qk_dim // wq + g)),
            pl.BlockSpec((None, CONV_WIDTH, wv), lambda b, g, r: (layer, 0, 2 * qk_dim // wv + g)),
            pl.BlockSpec((None, 1, HEAD_DIM), lambda b, g, r: (layer, 0, 0)),
        ],
        out_specs=(
            pl.BlockSpec((rows, wv), lambda b, g, r: (b * nrb + r, g)),
            pl.BlockSpec((1, 2 * n_pairs, HEAD_DIM, HEAD_DIM), lambda b, g, r: (b, g, 0, 0)),
            pl.BlockSpec((1, keep, wq), lambda b, g, r: (b, 0, g)),
            pl.BlockSpec((1, keep, wq), lambda b, g, r: (b, 0, g)),
            pl.BlockSpec((1, keep, wv), lambda b, g, r: (b, 0, g)),
        ),
        out_shape=out_shapes,
        scratch_shapes=[
            pltpu.VMEM((SUBLANES + rows, wq), F32),
            pltpu.VMEM((SUBLANES + rows, wq), F32),
            pltpu.VMEM((SUBLANES + rows, wv), F32),
            pltpu.VMEM((rows, wq), BF16),
            pltpu.VMEM((rows, wq), BF16),
            pltpu.VMEM((rows, wv), F32),
            pltpu.VMEM((rows, LANES), F32),
            pltpu.VMEM((n_chunks, n_pairs // 2, CHUNK, QUAD * CHUNK), BF16),
            pltpu.VMEM((n_chunks, n_pairs // 2, CHUNK, QUAD * CHUNK), BF16),
        ],
        compiler_params=_cparams("arbitrary", "arbitrary", "arbitrary"),
        name="delta_prompt",
    )(proj, proj, proj, proj, gates, gates_rows, conv_w, conv_w, conv_w, o_norm_w)


N_SAMPLE_INPUTS = 14


def _delta_sample_kernel(*refs, n_pairs, n_seqs, dec_seq, n_heads, aliased, layer):
    (q_ref, k_ref, v_ref, z_ref, cq_ref, ck_ref, cv_ref, gc_ref, gr_ref, wq_ref, wk_ref, wv_ref,
     onw_ref, s_in_ref) = refs[:N_SAMPLE_INPUTS]
    refs = refs[N_SAMPLE_INPUTS + 1:] if aliased else refs[N_SAMPLE_INPUTS:]
    y_ref, s_all_ref, ccq_ref, cck_ref, ccv_ref, eq_ref, ek_ref, ev_ref = refs
    if aliased:
        s_out_ref = s_all_ref
    else:
        s_out_ref = s_all_ref.at[layer]
        for other in range(s_all_ref.shape[0]):
            if other != layer:
                s_all_ref[other] = jnp.zeros(s_all_ref.shape[1:], F32)
    c = n_seqs * dec_seq
    pg = pl.program_id(1)
    hist = SUBLANES

    def conv(x_ref, cache_ref, ext_ref, w_ref, cc_ref):
        cols = ext_ref.shape[2]
        keep = CONV_WIDTH - 1
        ext_ref[:, 0:hist, :] = jnp.zeros((n_seqs, hist, cols), F32)
        ext_ref[:, hist - keep:hist, :] = cache_ref[...]
        ext_ref[:, hist:hist + dec_seq, :] = x_ref[...].reshape(n_seqs, dec_seq, cols)
        cc_ref[...] = ext_ref[:, hist + dec_seq - keep:hist + dec_seq, :]
        return _conv_taps(ext_ref, w_ref, hist).reshape(c, cols)

    xq = conv(q_ref, cq_ref, eq_ref, wq_ref, ccq_ref)
    xk = conv(k_ref, ck_ref, ek_ref, wk_ref, cck_ref)
    xv = conv(v_ref, cv_ref, ev_ref, wv_ref, ccv_ref)

    shift = lax.rem(LANES - 2 * n_pairs * pg, LANES)
    g_all = pltpu.roll(gc_ref[...], shift, 1)
    lane_wide = lax.broadcasted_iota(jnp.int32, (1, 2 * HEAD_DIM), 1)
    row_seq = lax.broadcasted_iota(jnp.int32, (c, 2 * HEAD_DIM), 0) // dec_seq
    pairs = range(n_pairs)
    seqs = range(n_seqs)
    hd = lambda p: slice(p * HEAD_DIM, (p + 1) * HEAD_DIM)

    qns = [_l2norm(xq[:, hd(p)]) * (HEAD_DIM ** -0.5) for p in pairs]
    kns = [_l2norm(xk[:, hd(p)]) for p in pairs]
    knb = [k.astype(BF16) for k in kns]

    groups, incl, strict = _quad_masks(c, dec_seq)
    qnb = [q.astype(BF16) for q in qns]
    grams = [_gram_quad(knb[2 * qd], qnb[2 * qd], knb[2 * qd + 1], qnb[2 * qd + 1])
             for qd in range(n_pairs // 2)]
    gcrs = [gr_ref[0, pl.ds(n_pairs * pg + p, 1), :] for p in pairs]
    l_cats, pis = [], []
    for qd in range(n_pairs // 2):
        ab = grams[qd]
        gcr = jnp.concatenate([gcrs[2 * qd], gcrs[2 * qd + 1]], axis=1)
        beta = _quad_select(g_all, QUAD * qd, groups)
        gc = _quad_select(g_all, n_heads + QUAD * qd, groups)
        dec = jnp.exp(jnp.where(incl, gc - gcr, NEG_BIG))
        pis.append((ab[c:] * dec).astype(BF16))
        l_cats.append(jnp.where(strict, beta * ab[:c] * dec, 0.0))
    pms = [pm.astype(BF16) for pm in _unit_lower_inverses_minus_identity(l_cats, dec_seq, groups)]
    half = lambda mats, p: mats[p // 2][:, (p % 2) * 2 * c:(p % 2 + 1) * 2 * c]

    s_cat = lambda p, s: jnp.concatenate([s_in_ref[s, 2 * p], s_in_ref[s, 2 * p + 1]], axis=1)
    kqs = [[jnp.dot(jnp.concatenate([kns[p][s * dec_seq:(s + 1) * dec_seq],
                                     qns[p][s * dec_seq:(s + 1) * dec_seq]], axis=0).astype(BF16),
                    s_cat(p, s).astype(BF16), preferred_element_type=F32) for s in seqs] for p in pairs]
    k_s = [jnp.concatenate([kqs[p][s][:dec_seq] for s in seqs], axis=0) for p in pairs]
    q_s = [jnp.concatenate([kqs[p][s][dec_seq:] for s in seqs], axis=0) for p in pairs]

    def last_gc(p, s):
        last = (s + 1) * dec_seq - 1
        return jnp.where(lane_wide < HEAD_DIM, gcrs[p][:, last:last + 1], gcrs[p][:, c + last:c + last + 1])

    gls = [[last_gc(p, s) for s in seqs] for p in pairs]
    beta_w = [_head_cols(g_all, 2 * p, HEAD_DIM) for p in pairs]
    gc_w = [_head_cols(g_all, n_heads + 2 * p, HEAD_DIM) for p in pairs]
    egc_w = [jnp.exp(g) for g in gc_w]
    rs = [beta_w[p] * (xv[:, 2 * p * HEAD_DIM:(2 * p + 2) * HEAD_DIM] - egc_w[p] * k_s[p]) for p in pairs]
    us = [rs[p] + jnp.dot(half(pms, p), _blockdiag_wide(rs[p]), preferred_element_type=F32) for p in pairs]
    os_ = [egc_w[p] * q_s[p] + jnp.dot(half(pis, p), _blockdiag_wide(us[p]), preferred_element_type=F32)
           for p in pairs]
    for p in pairs:
        gl_rows = jnp.concatenate([jnp.broadcast_to(g, (dec_seq, 2 * HEAD_DIM)) for g in gls[p]], axis=0)
        w = jnp.exp(gl_rows - gc_w[p]) * us[p]
        for s in seqs:
            w_s = jnp.where(row_seq == s, w, 0.0).astype(BF16)
            s_new = s_cat(p, s) * jnp.exp(gls[p][s]) + lax.dot_general(
                knb[p], w_s, (((0,), (0,)), ((), ())), preferred_element_type=F32)
            s_out_ref[s, 2 * p] = s_new[:, :HEAD_DIM]
            s_out_ref[s, 2 * p + 1] = s_new[:, HEAD_DIM:]
    for p in pairs:
        for e in range(2):
            hs = slice((2 * p + e) * HEAD_DIM, (2 * p + e + 1) * HEAD_DIM)
            y_ref[:, hs] = _gated_out(os_[p][:, e * HEAD_DIM:(e + 1) * HEAD_DIM], z_ref[:, hs], onw_ref[...])


def _delta_sample(proj, gates, gates_rows, conv_cache, state, prev_state_out, conv_w, o_norm_w, layer,
                  row_off, batch, dec_seq, n_heads, d_inner, n_pairs=4, n_seqs=8):
    qk_dim = n_heads // 2 * HEAD_DIM
    c = n_seqs * dec_seq
    wq, wv = n_pairs * HEAD_DIM, 2 * n_pairs * HEAD_DIM
    n_pg = n_heads // (2 * n_pairs)
    rb0 = row_off // c
    aliased = prev_state_out is not None
    kern = functools.partial(_delta_sample_kernel, n_pairs=n_pairs, n_seqs=n_seqs, dec_seq=dec_seq,
                             n_heads=n_heads, aliased=aliased, layer=layer)
    row_map = lambda col0: (lambda i, g: (rb0 + i, col0 + g))
    keep = CONV_WIDTH - 1
    in_specs = [
        pl.BlockSpec((c, wq), row_map(0)),
        pl.BlockSpec((c, wq), row_map(qk_dim // wq)),
        pl.BlockSpec((c, wv), row_map(2 * qk_dim // wv)),
        pl.BlockSpec((c, wv), row_map((2 * qk_dim + d_inner) // wv)),
        pl.BlockSpec((None, n_seqs, keep, wq), lambda i, g: (layer, i, 0, g)),
        pl.BlockSpec((None, n_seqs, keep, wq), lambda i, g: (layer, i, 0, qk_dim // wq + g)),
        pl.BlockSpec((None, n_seqs, keep, wv), lambda i, g: (layer, i, 0, 2 * qk_dim // wv + g)),
        pl.BlockSpec((c, LANES), lambda i, g: (rb0 + i, 0)),
        pl.BlockSpec((1, n_heads // 2, 2 * c), lambda i, g: (i, 0, 0)),
        pl.BlockSpec((None, CONV_WIDTH, wq), lambda i, g: (layer, 0, g)),
        pl.BlockSpec((None, CONV_WIDTH, wq), lambda i, g: (layer, 0, qk_dim // wq + g)),
        pl.BlockSpec((None, CONV_WIDTH, wv), lambda i, g: (layer, 0, 2 * qk_dim // wv + g)),
        pl.BlockSpec((None, 1, HEAD_DIM), lambda i, g: (layer, 0, 0)),
        pl.BlockSpec((None, n_seqs, 2 * n_pairs, HEAD_DIM, HEAD_DIM), lambda i, g: (layer, i, g, 0, 0)),
    ]
    args = [proj, proj, proj, proj, conv_cache, conv_cache, conv_cache, gates, gates_rows,
            conv_w, conv_w, conv_w, o_norm_w, state]
    aliases = {}
    if aliased:
        in_specs.append(pl.BlockSpec(memory_space=pl.ANY))
        args.append(prev_state_out)
        aliases = {N_SAMPLE_INPUTS: 1}
    state_blk = (n_seqs, 2 * n_pairs, HEAD_DIM, HEAD_DIM)
    if aliased:
        state_spec = pl.BlockSpec((None,) + state_blk, lambda i, g: (layer, i, g, 0, 0))
    else:
        state_spec = pl.BlockSpec((state.shape[0],) + state_blk, lambda i, g: (0, i, g, 0, 0))
    out_shapes = (
        jax.ShapeDtypeStruct((batch * dec_seq, d_inner), BF16),
        jax.ShapeDtypeStruct(state.shape, F32),
        jax.ShapeDtypeStruct((batch, keep, qk_dim), F32),
        jax.ShapeDtypeStruct((batch, keep, qk_dim), F32),
        jax.ShapeDtypeStruct((batch, keep, d_inner), F32),
    )
    return pl.pallas_call(
        kern,
        grid=(batch // n_seqs, n_pg),
        in_specs=in_specs,
        out_specs=(
            pl.BlockSpec((c, wv), lambda i, g: (i, g)),
            state_spec,
            pl.BlockSpec((n_seqs, keep, wq), lambda i, g: (i, 0, g)),
            pl.BlockSpec((n_seqs, keep, wq), lambda i, g: (i, 0, g)),
            pl.BlockSpec((n_seqs, keep, wv), lambda i, g: (i, 0, g)),
        ),
        out_shape=out_shapes,
        scratch_shapes=[
            pltpu.VMEM((n_seqs, SUBLANES + dec_seq, wq), F32),
            pltpu.VMEM((n_seqs, SUBLANES + dec_seq, wq), F32),
            pltpu.VMEM((n_seqs, SUBLANES + dec_seq, wv), F32),
        ],
        input_output_aliases=aliases,
        compiler_params=_cparams("arbitrary", "arbitrary"),
        name="delta_sample",
    )(*args)


def _window_sums(x, win, axis):
    s = x
    w = 1
    while w < win:
        s = s + pltpu.roll(s, w, axis)
        w *= 2
    return s


def _pool_finish(pooled, z, wg_ref, sc_ref):
    mixed = jnp.dot(pooled.astype(BF16), wg_ref[...], preferred_element_type=F32)
    return (mixed * sc_ref[...] * _silu(z)).astype(BF16)


def _pool_prompt_kernel(u_ref, prev_ref, z_ref, wg_ref, sc_ref, y_ref, cache_ref, ext_ref, *, tt, nt):
    g = pl.program_id(0)
    t = pl.program_id(2)
    hist = POOL_BUF + 1

    @pl.when(t == 0)
    def _():
        ext_ref[0:hist, :] = jnp.zeros((hist, ext_ref.shape[1]), F32)

    @pl.when(t > 0)
    def _():
        ext_ref[0:hist, :] = prev_ref[...]

    ext_ref[hist:hist + tt, :] = u_ref[...]
    pos = t * tt + lax.broadcasted_iota(jnp.int32, (tt, 1), 0)
    for gi, win in enumerate(POOL_WINDOWS):
        @pl.when(g == gi)
        def _(win=win):
            acc = _window_sums(ext_ref[...], win, 0)[hist:]
            cnt = jnp.minimum(pos + 1, win).astype(F32)
            y_ref[...] = _pool_finish(acc / cnt - u_ref[...], z_ref[...], wg_ref, sc_ref)

    @pl.when(t == nt - 1)
    def _():
        cache_ref[0] = ext_ref[hist + tt - POOL_BUF:hist + tt, :]


def _pool_prompt(proj, w_group, scale, layer, batch, seq, d_inner, tt=512):
    gd = d_inner // len(POOL_WINDOWS)
    nt = seq // tt
    hist = POOL_BUF + 1
    kern = functools.partial(_pool_prompt_kernel, tt=tt, nt=nt)
    n_g = len(POOL_WINDOWS)
    return pl.pallas_call(
        kern,
        grid=(n_g, batch, nt),
        in_specs=[
            pl.BlockSpec((tt, gd), lambda g, b, t: (b * nt + t, g)),
            pl.BlockSpec((hist, gd), lambda g, b, t: (jnp.maximum((b * nt + t) * (tt // hist) - 1, 0), g)),
            pl.BlockSpec((tt, gd), lambda g, b, t: (b * nt + t, n_g + g)),
            pl.BlockSpec((None, None, gd, gd), lambda g, b, t: (layer, g, 0, 0)),
            pl.BlockSpec((None, 1, gd), lambda g, b, t: (layer, 0, g)),
        ],
        out_specs=(
            pl.BlockSpec((tt, gd), lambda g, b, t: (b * nt + t, g)),
            pl.BlockSpec((1, POOL_BUF, gd), lambda g, b, t: (b, 0, g)),
        ),
        out_shape=(
            jax.ShapeDtypeStruct((batch * seq, d_inner), BF16),
            jax.ShapeDtypeStruct((batch, POOL_BUF, d_inner), F32),
        ),
        scratch_shapes=[pltpu.VMEM((hist + tt, gd), F32)],
        compiler_params=_cparams("arbitrary", "arbitrary", "arbitrary"),
        name="pool_prompt",
    )(proj, proj, proj, w_group, scale)


def _pool_sample_kernel(u_ref, c_ref, z_ref, wg_ref, sc_ref, y_ref, cache_ref, ext_ref, *, sb, dec_seq):
    g = pl.program_id(0)
    hist = POOL_BUF + 1
    gd = ext_ref.shape[2]
    ext_ref[:, 0:SUBLANES, :] = jnp.zeros((sb, SUBLANES, gd), F32)
    ext_ref[:, hist - POOL_BUF:hist, :] = c_ref[...]
    ext_ref[:, hist:hist + dec_seq, :] = u_ref[...].reshape(sb, dec_seq, gd)
    pos = PAST_LEN + lax.broadcasted_iota(jnp.int32, (1, dec_seq, 1), 1)
    for gi, win in enumerate(POOL_WINDOWS):
        @pl.when(g == gi)
        def _(win=win):
            x = ext_ref[:, hist:hist + dec_seq, :]
            acc = _window_sums(ext_ref[...], win, 1)[:, hist:, :]
            cnt = jnp.minimum(pos + 1, win).astype(F32)
            pooled = (acc / cnt - x).reshape(sb * dec_seq, gd)
            y_ref[...] = _pool_finish(pooled, z_ref[...], wg_ref, sc_ref)

    cache_ref[...] = ext_ref[:, hist + dec_seq - POOL_BUF:hist + dec_seq, :]


def _pool_sample(proj, cache, w_group, scale, layer, row_off, batch, dec_seq, d_inner, sb=16):
    gd = d_inner // len(POOL_WINDOWS)
    n_g = len(POOL_WINDOWS)
    rows = sb * dec_seq
    rb0 = row_off // rows
    hist = POOL_BUF + 1
    kern = functools.partial(_pool_sample_kernel, sb=sb, dec_seq=dec_seq)
    return pl.pallas_call(
        kern,
        grid=(n_g, batch // sb),
        in_specs=[
            pl.BlockSpec((rows, gd), lambda g, i: (rb0 + i, g)),
            pl.BlockSpec((None, sb, POOL_BUF, gd), lambda g, i: (layer, i, 0, g)),
            pl.BlockSpec((rows, gd), lambda g, i: (rb0 + i, n_g + g)),
            pl.BlockSpec((None, None, gd, gd), lambda g, i: (layer, g, 0, 0)),
            pl.BlockSpec((None, 1, gd), lambda g, i: (layer, 0, g)),
        ],
        out_specs=(
            pl.BlockSpec((rows, gd), lambda g, i: (i, g)),
            pl.BlockSpec((sb, POOL_BUF, gd), lambda g, i: (i, 0, g)),
        ),
        out_shape=(
            jax.ShapeDtypeStruct((batch * dec_seq, d_inner), BF16),
            jax.ShapeDtypeStruct((batch, POOL_BUF, d_inner), F32),
        ),
        scratch_shapes=[pltpu.VMEM((sb, hist + dec_seq, gd), F32)],
        compiler_params=_cparams("arbitrary", "arbitrary"),
        name="pool_sample",
    )(proj, cache, proj, w_group, scale)


def kernel(x_prompt, x_sample, state_delta, cache_conv, cache_pool, norm_w, final_norm_w, delta_w_in,
           delta_conv_w, delta_a_log, delta_dt_bias, delta_o_norm_w, delta_w_out, pool_w_in, pool_w_group,
           pool_scale, pool_w_out):
    batch, seq, d_model = x_prompt.shape
    dec_batch, dec_seq, _ = x_sample.shape
    depth = norm_w.shape[0]
    n_heads = state_delta.shape[2]
    d_inner = n_heads * HEAD_DIM
    conv_dim = cache_conv.shape[3]
    qk_dim = (conv_dim - d_inner) // 2
    mp, ms = batch * seq, dec_batch * dec_seq
    m = mp + ms
    n_main = conv_dim + d_inner
    n_seqs = CHUNK // dec_seq

    h = (x_prompt.reshape(mp, d_model), x_sample.reshape(ms, d_model))

    delta_w_out_b = delta_w_out.astype(BF16)
    pool_w_out_b = pool_w_out.astype(BF16)
    pool_w_group_b = pool_w_group.astype(BF16)
    delta_w_in_t = jnp.swapaxes(delta_w_in, 1, 2)
    w_tail_t = jnp.pad(delta_w_in_t[:, n_main:, :], ((0, 0), (0, LANES - 2 * n_heads), (0, 0)))
    onw = delta_o_norm_w.reshape(-1, 1, HEAD_DIM)
    pool_scale3 = pool_scale.reshape(pool_scale.shape[0], 1, d_inner)

    sd_p, cc_p, cp_p, cc_s, cp_s = [], [], [], [], []
    sd_s = None
    for i in range(depth):
        j = i // 2
        if isinstance(h, tuple):
            xn = _rmsnorm_pair(h[0], h[1], norm_w[i], BF16)
        else:
            xn = _rmsnorm(h, norm_w[i], BF16, m)
        if i % 2 == 0:
            proj = _matmul(xn, delta_w_in_t, j, n_main, tm=3072, tn=512, w_transposed=True)
            tail = _matmul(xn, w_tail_t, j, LANES, tm=3072, tn=LANES, w_transposed=True)
            gates = _gates(tail, delta_a_log[j], delta_dt_bias[j], mp, n_heads, dec_seq)
            gcum = gates[:, n_heads:2 * n_heads]
            rows_p = gcum[:mp].reshape(batch, seq // CHUNK, CHUNK, n_heads // 2, 2)
            rows_p = rows_p.transpose(0, 1, 3, 4, 2).reshape(batch, seq // (CHUNKS_PER_STEP * CHUNK), CHUNKS_PER_STEP * (n_heads // 2), 2 * CHUNK)
            rows_s = gcum[mp:].reshape(ms // CHUNK, CHUNK, n_heads // 2, 2)
            rows_s = rows_s.transpose(0, 2, 3, 1).reshape(ms // CHUNK, n_heads // 2, 2 * CHUNK)
            y_p, sp, cq, ck, cv = _delta_prompt(proj, gates, rows_p, delta_conv_w, onw, j, batch, seq,
                                                n_heads, d_inner)
            y_s, sd_s, cqs, cks, cvs = _delta_sample(proj, gates, rows_s, cache_conv, state_delta, sd_s,
                                                     delta_conv_w, onw, j, mp, dec_batch, dec_seq, n_heads,
                                                     d_inner, n_seqs=n_seqs)
            sd_p.append(sp)
            cc_p.append(jnp.concatenate([cq, ck, cv], axis=-1))
            cc_s.append(jnp.concatenate([cqs, cks, cvs], axis=-1))
            w_out, lw = delta_w_out_b, j
        else:
            proj = _matmul(xn, pool_w_in, j, 2 * d_inner, tm=3072, tn=512)
            y_p, pp = _pool_prompt(proj, pool_w_group_b, pool_scale3, j, batch, seq, d_inner)
            y_s, ps = _pool_sample(proj, cache_pool, pool_w_group_b, pool_scale3, j, mp, dec_batch, dec_seq,
                                   d_inner)
            cp_p.append(pp)
            cp_s.append(ps)
            w_out, lw = pool_w_out_b, j
        h = _matmul_residual(y_p, y_s, w_out, lw, h, tn=256 if isinstance(h, tuple) else 512)

    y_prompt = _rmsnorm(h, final_norm_w, F32, mp, 0).reshape(batch, seq, d_model)
    y_sample = _rmsnorm(h, final_norm_w, F32, ms, mp).reshape(dec_batch, dec_seq, d_model)
    return (y_prompt, y_sample, jnp.stack(sd_p), jnp.stack(cc_p), jnp.stack(cp_p),
            sd_s, jnp.stack(cc_s), jnp.stack(cp_s))
```

```python
import functools

import jax
import jax.numpy as jnp
from jax import lax
from jax.experimental import pallas as pl
from jax.experimental.pallas import tpu as pltpu

F32 = jnp.float32
BF16 = jnp.bfloat16

HEAD_DIM = 128
CONV_WIDTH = 4
CHUNK = 64
CHUNKS_PER_STEP = 4
POOL_WINDOWS = (2, 4, 8, 16)
POOL_BUF = max(POOL_WINDOWS) - 1
PAST_LEN = 16384
NORM_EPS = 1e-6
NEG_BIG = -1e30

LANES = 128
SUBLANES = 8
VMEM_LIMIT = 56 * 1024 * 1024


def _cparams(*sem):
    return pltpu.CompilerParams(dimension_semantics=sem, vmem_limit_bytes=VMEM_LIMIT)


def _silu(x):
    return x * jax.nn.sigmoid(x)


def _rmsnorm_kernel(x_ref, w_ref, o_ref):
    x = x_ref[...]
    y = x * lax.rsqrt(jnp.mean(x * x, axis=-1, keepdims=True) + NORM_EPS)
    o_ref[...] = (y * w_ref[...]).astype(o_ref.dtype)


def _rmsnorm_pair_kernel(xp_ref, xs_ref, w_ref, o_ref, *, n_prompt_blocks):
    i = pl.program_id(0)

    @pl.when(i < n_prompt_blocks)
    def _():
        _rmsnorm_kernel(xp_ref, w_ref, o_ref)

    @pl.when(i >= n_prompt_blocks)
    def _():
        _rmsnorm_kernel(xs_ref, w_ref, o_ref)


def _rmsnorm_pair(x_p, x_s, w, out_dtype, tm=512):
    d = x_p.shape[1]
    npb = x_p.shape[0] // tm
    m = x_p.shape[0] + x_s.shape[0]
    return pl.pallas_call(
        functools.partial(_rmsnorm_pair_kernel, n_prompt_blocks=npb),
        grid=(m // tm,),
        in_specs=[pl.BlockSpec((tm, d), lambda i: (jnp.minimum(i, npb - 1), 0)),
                  pl.BlockSpec((tm, d), lambda i: (jnp.maximum(i - npb, 0), 0)),
                  pl.BlockSpec((1, d), lambda i: (0, 0))],
        out_specs=pl.BlockSpec((tm, d), lambda i: (i, 0)),
        out_shape=jax.ShapeDtypeStruct((m, d), out_dtype),
        compiler_params=_cparams("arbitrary"),
        name="rmsnorm",
    )(x_p, x_s, w.reshape(1, d))


def _rmsnorm(x, w, out_dtype, rows, row_off=0, tm=512):
    d = x.shape[1]
    off = row_off // tm
    return pl.pallas_call(
        _rmsnorm_kernel,
        grid=(rows // tm,),
        in_specs=[pl.BlockSpec((tm, d), lambda i: (i + off, 0)),
                  pl.BlockSpec((1, d), lambda i: (0, 0))],
        out_specs=pl.BlockSpec((tm, d), lambda i: (i, 0)),
        out_shape=jax.ShapeDtypeStruct((rows, d), out_dtype),
        compiler_params=_cparams("arbitrary"),
        name="rmsnorm",
    )(x, w.reshape(1, d))


def _matmul_kernel(x_ref, w_ref, o_ref, *, w_transposed):
    w = w_ref[...].astype(BF16)
    contract = (((1,), (1 if w_transposed else 0,)), ((), ()))
    o_ref[...] = lax.dot_general(x_ref[...], w, contract, preferred_element_type=F32)


def _matmul(x, w, layer, n_cols, tm, tn, w_transposed=False):
    m, k = x.shape
    if w_transposed:
        w_spec = pl.BlockSpec((None, tn, k), lambda i, j: (layer, j, 0))
    else:
        w_spec = pl.BlockSpec((None, k, tn), lambda i, j: (layer, 0, j))
    return pl.pallas_call(
        functools.partial(_matmul_kernel, w_transposed=w_transposed),
        grid=(m // tm, n_cols // tn),
        in_specs=[pl.BlockSpec((tm, k), lambda i, j: (i, 0)), w_spec],
        out_specs=pl.BlockSpec((tm, tn), lambda i, j: (i, j)),
        out_shape=jax.ShapeDtypeStruct((m, n_cols), F32),
        compiler_params=_cparams("arbitrary", "arbitrary"),
        name="in_proj",
    )(x, w)


def _matmul_res_kernel(yp_ref, ys_ref, w_ref, *refs, n_prompt_blocks):
    hp_ref, hs_ref, o_ref = refs if len(refs) == 3 else (refs[0], refs[0], refs[1])
    i = pl.program_id(0)
    w = w_ref[...]

    @pl.when(i < n_prompt_blocks)
    def _():
        o_ref[...] = hp_ref[...] + jnp.dot(yp_ref[...], w, preferred_element_type=F32)

    @pl.when(i >= n_prompt_blocks)
    def _():
        o_ref[...] = hs_ref[...] + jnp.dot(ys_ref[...], w, preferred_element_type=F32)


def _matmul_residual(y_p, y_s, w, layer, h, tm=1024, tn=512):
    rows_p, k = y_p.shape
    npb = rows_p // tm
    prompt_rows = lambda i, j: (jnp.minimum(i, npb - 1), j)
    sample_rows = lambda i, j: (jnp.maximum(i - npb, 0), j)
    if isinstance(h, tuple):
        m, n = h[0].shape[0] + h[1].shape[0], h[0].shape[1]
        h_specs = [pl.BlockSpec((tm, tn), prompt_rows), pl.BlockSpec((tm, tn), sample_rows)]
        aliases = {}
    else:
        m, n = h.shape
        h, h_specs, aliases = (h,), [pl.BlockSpec((tm, tn), lambda i, j: (i, j))], {3: 0}
    return pl.pallas_call(
        functools.partial(_matmul_res_kernel, n_prompt_blocks=npb),
        grid=(m // tm, n // tn),
        in_specs=[pl.BlockSpec((tm, k), lambda i, j: (jnp.minimum(i, npb - 1), 0)),
                  pl.BlockSpec((tm, k), lambda i, j: (jnp.maximum(i - npb, 0), 0)),
                  pl.BlockSpec((None, k, tn), lambda i, j: (layer, 0, j))] + h_specs,
        out_specs=pl.BlockSpec((tm, tn), lambda i, j: (i, j)),
        out_shape=jax.ShapeDtypeStruct((m, n), F32),
        input_output_aliases=aliases,
        compiler_params=_cparams("arbitrary", "arbitrary"),
        name="out_proj",
    )(y_p, y_s, w, *h)


def _gates_kernel(t_ref, alog_ref, dt_ref, o_ref, *, n_prompt_blocks, n_heads, dec_seq):
    i = pl.program_id(0)
    x = t_ref[...]
    lane = lax.broadcasted_iota(jnp.int32, x.shape, 1)
    row = lax.broadcasted_iota(jnp.int32, x.shape, 0)
    beta = jax.nn.sigmoid(x)
    a = x + dt_ref[...]
    softplus = jnp.maximum(a, 0.0) + jnp.log(1.0 + jnp.exp(-jnp.abs(a)))
    g = -jnp.exp(alog_ref[...]) * softplus
    seg = jnp.where(i < n_prompt_blocks, CHUNK, dec_seq)
    pos = row & (seg - 1)
    cs = g
    s = 1
    while s < CHUNK:
        cs = cs + jnp.where(pos >= s, pltpu.roll(cs, s, 0), 0.0)
        s *= 2
    o_ref[...] = jnp.where(lane < n_heads, beta, jnp.where(lane < 2 * n_heads, cs, 0.0))


def _gates(tail, a_log, dt_bias, n_prompt_rows, n_heads, dec_seq, tm=1024):
    m = tail.shape[0]
    pad = lambda v: jnp.zeros((1, LANES), F32).at[0, n_heads:2 * n_heads].set(v.astype(F32))
    kern = functools.partial(_gates_kernel, n_prompt_blocks=n_prompt_rows // tm, n_heads=n_heads,
                             dec_seq=dec_seq)
    return pl.pallas_call(
        kern,
        grid=(m // tm,),
        in_specs=[pl.BlockSpec((tm, LANES), lambda i: (i, 0)),
                  pl.BlockSpec((1, LANES), lambda i: (0, 0)),
                  pl.BlockSpec((1, LANES), lambda i: (0, 0))],
        out_specs=pl.BlockSpec((tm, LANES), lambda i: (i, 0)),
        out_shape=jax.ShapeDtypeStruct((m, LANES), F32),
        compiler_params=_cparams("arbitrary"),
        name="delta_gates",
    )(tail, pad(a_log), pad(dt_bias))


QUAD = 4


def _blockdiag(y, masks):
    yb = y.astype(BF16)
    zero = jnp.zeros_like(yb)
    return jnp.concatenate([jnp.where(m, yb, zero) for m in masks], axis=0)


def _blockdiag_wide(r):
    rb = r.astype(BF16)
    z = jnp.zeros_like(rb[:, :HEAD_DIM])
    return jnp.concatenate([jnp.concatenate([rb[:, :HEAD_DIM], z], axis=1),
                            jnp.concatenate([z, rb[:, HEAD_DIM:]], axis=1)], axis=0)


def _inverse_chain_stages(l_cats, seg, masks, out):
    c = l_cats[0].shape[0]
    dot = lambda a, b: jnp.dot(a, _blockdiag(b, masks), preferred_element_type=F32)
    ps = [-l for l in l_cats]
    lbs = [l.astype(BF16) for l in l_cats]
    qs = [dot(lb, lb) for lb in lbs]
    yield
    n_steps = seg.bit_length() - 2
    for step in range(n_steps):
        qbs = [q.astype(BF16) for q in qs]
        if step < n_steps - 1:
            pqs = [dot(jnp.concatenate([p.astype(BF16), qb], axis=0), qb) for p, qb in zip(ps, qbs)]
            ps = [p + q + pq[:c] for p, q, pq in zip(ps, qs, pqs)]
            qs = [pq[c:] for pq in pqs]
        else:
            pqs = [dot(p.astype(BF16), qb) for p, qb in zip(ps, qbs)]
            ps = [p + q + pq for p, q, pq in zip(ps, qs, pqs)]
        yield
    out.extend(ps)


def _unit_lower_inverses_minus_identity(l_cats, seg, masks):
    out = []
    for _ in _inverse_chain_stages(l_cats, seg, masks, out):
        pass
    return out


def _interleave(*stage_generators):
    live = list(stage_generators)
    while live:
        for g in list(live):
            try:
                next(g)
            except StopIteration:
                live.remove(g)


def _gram_quad(kn_a, qn_a, kn_b, qn_b):
    zero = jnp.zeros_like(kn_a)
    lhs = jnp.concatenate([jnp.concatenate([kn_a, kn_b], axis=1),
                           jnp.concatenate([qn_a, qn_b], axis=1)], axis=0)
    ka = jnp.concatenate([kn_a, zero], axis=1)
    kb = jnp.concatenate([zero, kn_b], axis=1)
    rhs = jnp.concatenate([ka, ka, kb, kb], axis=0)
    return lax.dot_general(lhs, rhs, (((1,), (1,)), ((), ())), preferred_element_type=F32)


def _quad_masks(c, seg):
    lane = lax.broadcasted_iota(jnp.int32, (c, QUAD * c), 1)
    ii = lax.broadcasted_iota(jnp.int32, (c, QUAD * c), 0)
    jj = lane & (c - 1)
    groups = [(lane // c) == g for g in range(QUAD)]
    incl = (ii >= jj) if seg == c else ((ii >= jj) & ((ii // seg) == (jj // seg)))
    return groups, incl, ii > jj


def _quad_select(g_all, base, groups):
    out = g_all[:, base + QUAD - 1:base + QUAD]
    for g in range(QUAD - 2, -1, -1):
        out = jnp.where(groups[g], g_all[:, base + g:base + g + 1], out)
    return out


def _head_cols(g_all, idx, width):
    c = g_all.shape[0]
    return jnp.concatenate([jnp.broadcast_to(g_all[:, idx:idx + 1], (c, width)),
                            jnp.broadcast_to(g_all[:, idx + 1:idx + 2], (c, width))], axis=1)


def _gated_out(o, z, onw):
    on = o * lax.rsqrt(jnp.mean(o * o, axis=-1, keepdims=True) + NORM_EPS) * onw
    return (on * _silu(z)).astype(BF16)


def _conv_taps(ext_ref, w_ref, start):
    assert CONV_WIDTH == 4 and start >= CONV_WIDTH - 1 and start % SUBLANES == 0
    axis = len(ext_ref.shape) - 2
    x = ext_ref[...]
    x1 = pltpu.roll(x, 1, axis)
    a = w_ref[3:4, :] * x + w_ref[2:3, :] * x1
    b = w_ref[1:2, :] * x + w_ref[0:1, :] * x1
    y = a + pltpu.roll(b, 2, axis)
    y = y[start:] if axis == 0 else y[:, start:]
    return _silu(y)


def _l2norm(x):
    return x * lax.rsqrt(jnp.sum(x * x, axis=-1, keepdims=True) + NORM_EPS)


def _delta_prompt_kernel(q_ref, k_ref, v_ref, z_ref, gc_ref, gr_ref, wq_ref, wk_ref, wv_ref, onw_ref,
                         y_ref, s_ref, ccq_ref, cck_ref, ccv_ref,
                         eq_ref, ek_ref, ev_ref, qn_ref, kn_ref, vv_ref, gt_ref, p_ref, pi_ref,
                         *, n_pairs, n_chunks, n_heads, batch_chunks):
    c = CHUNK
    rows = n_chunks * c
    pg = pl.program_id(1)
    cb = pl.program_id(2)
    hist = SUBLANES

    @pl.when(cb == 0)
    def _():
        eq_ref[0:hist, :] = jnp.zeros((hist, eq_ref.shape[1]), F32)
        ek_ref[0:hist, :] = jnp.zeros((hist, ek_ref.shape[1]), F32)
        ev_ref[0:hist, :] = jnp.zeros((hist, ev_ref.shape[1]), F32)
        s_ref[...] = jnp.zeros(s_ref.shape, F32)

    eq_ref[hist:hist + rows, :] = q_ref[...]
    ek_ref[hist:hist + rows, :] = k_ref[...]
    ev_ref[hist:hist + rows, :] = v_ref[...]
    xq = _conv_taps(eq_ref, wq_ref, hist)
    xk = _conv_taps(ek_ref, wk_ref, hist)
    vv_ref[...] = _conv_taps(ev_ref, wv_ref, hist)
    for p in range(n_pairs):
        sl = slice(p * HEAD_DIM, (p + 1) * HEAD_DIM)
        qn_ref[:, sl] = (_l2norm(xq[:, sl]) * (HEAD_DIM ** -0.5)).astype(BF16)
        kn_ref[:, sl] = _l2norm(xk[:, sl]).astype(BF16)
    keep = CONV_WIDTH - 1
    ccq_ref[0] = eq_ref[hist + rows - keep:hist + rows, :]
    cck_ref[0] = ek_ref[hist + rows - keep:hist + rows, :]
    ccv_ref[0] = ev_ref[hist + rows - keep:hist + rows, :]
    eq_ref[0:hist, :] = eq_ref[rows:rows + hist, :]
    ek_ref[0:hist, :] = ek_ref[rows:rows + hist, :]
    ev_ref[0:hist, :] = ev_ref[rows:rows + hist, :]

    shift = lax.rem(LANES - 2 * n_pairs * pg, LANES)
    gt_ref[...] = pltpu.roll(gc_ref[...], shift, 1)

    n_quads = n_pairs // 2
    groups, incl, strict = _quad_masks(c, c)
    pair_row = lambda ci, p: ci * (n_heads // 2) + n_pairs * pg + p
    lane_wide = lax.broadcasted_iota(jnp.int32, (1, 2 * HEAD_DIM), 1)
    pairs = range(n_pairs)

    def matrices(chunks):
        grams = {}
        for ci in chunks:
            hd = lambda ref, p: ref[ci * c:(ci + 1) * c, p * HEAD_DIM:(p + 1) * HEAD_DIM]
            for qd in range(n_quads):
                grams[ci, qd] = _gram_quad(hd(kn_ref, 2 * qd), hd(qn_ref, 2 * qd), hd(kn_ref, 2 * qd + 1),
                                           hd(qn_ref, 2 * qd + 1))
        yield
        l_cats = []
        for ci in chunks:
            g_all = gt_ref[ci * c:(ci + 1) * c, :]
            for qd in range(n_quads):
                ab = grams[ci, qd]
                gcr = jnp.concatenate([gr_ref[0, 0, pl.ds(pair_row(ci, 2 * qd), 1), :],
                                       gr_ref[0, 0, pl.ds(pair_row(ci, 2 * qd + 1), 1), :]], axis=1)
                beta = _quad_select(g_all, QUAD * qd, groups)
                gc = _quad_select(g_all, n_heads + QUAD * qd, groups)
                dec = jnp.exp(jnp.where(incl, gc - gcr, NEG_BIG))
                pi_ref[ci, qd] = (ab[c:] * dec).astype(BF16)
                l_cats.append(jnp.where(strict, beta * ab[:c] * dec, 0.0))
        pms = []
        yield from _inverse_chain_stages(l_cats, c, groups, pms)
        for n, pm in enumerate(pms):
            p_ref[chunks[n // n_quads], n % n_quads] = pm.astype(BF16)

    def recurrence(chunks):
        for ci in chunks:
            r0 = ci * c
            g_all = gt_ref[r0:r0 + c, :]
            eg_all = jnp.exp(g_all)
            half = lambda ref, p: ref[ci, p // 2, :, (p % 2) * 2 * c:(p % 2 + 1) * 2 * c]
            kns = [kn_ref[r0:r0 + c, p * HEAD_DIM:(p + 1) * HEAD_DIM] for p in pairs]
            qns = [qn_ref[r0:r0 + c, p * HEAD_DIM:(p + 1) * HEAD_DIM] for p in pairs]
            s_cat = lambda p: jnp.concatenate([s_ref[0, 2 * p], s_ref[0, 2 * p + 1]], axis=1)
            kqs = [jnp.dot(jnp.concatenate([kns[p], qns[p]], axis=0), s_cat(p).astype(BF16),
                           preferred_element_type=F32) for p in pairs]
            yield
            beta_w = [_head_cols(g_all, 2 * p, HEAD_DIM) for p in pairs]
            gc_w = [_head_cols(g_all, n_heads + 2 * p, HEAD_DIM) for p in pairs]
            egc_w = [_head_cols(eg_all, n_heads + 2 * p, HEAD_DIM) for p in pairs]
            rs = [beta_w[p] * (vv_ref[r0:r0 + c, 2 * p * HEAD_DIM:(2 * p + 2) * HEAD_DIM]
                               - egc_w[p] * kqs[p][:c]) for p in pairs]
            us = [rs[p] + jnp.dot(half(p_ref, p), _blockdiag_wide(rs[p]), preferred_element_type=F32)
                  for p in pairs]
            yield
            for p in pairs:
                gcr = gr_ref[0, 0, pl.ds(pair_row(ci, p), 1), :]
                gl_w = jnp.where(lane_wide < HEAD_DIM, gcr[:, c - 1:c], gcr[:, 2 * c - 1:2 * c])
                w = (jnp.exp(gl_w - gc_w[p]) * us[p]).astype(BF16)
                s_new = s_cat(p) * jnp.exp(gl_w) + lax.dot_general(
                    kns[p], w, (((0,), (0,)), ((), ())), preferred_element_type=F32)
                s_ref[0, 2 * p] = s_new[:, :HEAD_DIM]
                s_ref[0, 2 * p + 1] = s_new[:, HEAD_DIM:]
            yield
            os_ = [egc_w[p] * kqs[p][c:] + jnp.dot(half(pi_ref, p), _blockdiag_wide(us[p]),
                                                   preferred_element_type=F32) for p in pairs]
            yield
            for p in pairs:
                for e in range(2):
                    hs = slice((2 * p + e) * HEAD_DIM, (2 * p + e + 1) * HEAD_DIM)
                    y_ref[r0:r0 + c, hs] = _gated_out(os_[p][:, e * HEAD_DIM:(e + 1) * HEAD_DIM],
                                                      z_ref[r0:r0 + c, hs], onw_ref[...])

    for c0 in range(0, n_chunks, batch_chunks):
        _interleave(matrices(list(range(c0, c0 + batch_chunks))))
    _interleave(recurrence(range(n_chunks)))


def _delta_prompt(proj, gates, gates_rows, conv_w, o_norm_w, layer, batch, seq, n_heads, d_inner,
                  n_pairs=8, n_chunks=CHUNKS_PER_STEP, batch_chunks=2):
    qk_dim = n_heads // 2 * HEAD_DIM
    rows = n_chunks * CHUNK
    nrb = seq // rows
    wq, wv = n_pairs * HEAD_DIM, 2 * n_pairs * HEAD_DIM
    n_pg = n_heads // (2 * n_pairs)
    kern = functools.partial(_delta_prompt_kernel, n_pairs=n_pairs, n_chunks=n_chunks, n_heads=n_heads,
                             batch_chunks=batch_chunks)
    row_map = lambda col0: (lambda b, g, r: (b * nrb + r, col0 + g))
    keep = CONV_WIDTH - 1
    out_shapes = (
        jax.ShapeDtypeStruct((batch * seq, d_inner), BF16),
        jax.ShapeDtypeStruct((batch, n_heads, HEAD_DIM, HEAD_DIM), F32),
        jax.ShapeDtypeStruct((batch, keep, qk_dim), F32),
        jax.ShapeDtypeStruct((batch, keep, qk_dim), F32),
        jax.ShapeDtypeStruct((batch, keep, d_inner), F32),
    )
    return pl.pallas_call(
        kern,
        grid=(batch, n_pg, nrb),
        in_specs=[
            pl.BlockSpec((rows, wq), row_map(0)),
            pl.BlockSpec((rows, wq), row_map(qk_dim // wq)),
            pl.BlockSpec((rows, wv), row_map(2 * qk_dim // wv)),
            pl.BlockSpec((rows, wv), row_map((2 * qk_dim + d_inner) // wv)),
            pl.BlockSpec((rows, LANES), lambda b, g, r: (b * nrb + r, 0)),
            pl.BlockSpec((1, 1, n_chunks * (n_heads // 2), 2 * CHUNK), lambda b, g, r: (b, r, 0, 0)),
            pl.BlockSpec((None, CONV_WIDTH, wq), lambda b, g, r: (layer, 0, g)),
            pl.BlockSpec((None, CONV_WIDTH, wq), lambda b, g, r: (layer, 0, qk_dim // wq + g)),
            pl.BlockSpec((None, CONV_WIDTH, wv), lambda b, g, r: (layer, 0, 2 * qk_dim // wv + g)),
            pl.BlockSpec((None, 1, HEAD_DIM), lambda b, g, r: (layer, 0, 0)),
        ],
        out_specs=(
            pl.BlockSpec((rows, wv), lambda b, g, r: (b * nrb + r, g)),
            pl.BlockSpec((1, 2 * n_pairs, HEAD_DIM, HEAD_DIM), lambda b, g, r: (b, g, 0, 0)),
            pl.BlockSpec((1, keep, wq), lambda b, g, r: (b, 0, g)),
            pl.BlockSpec((1, keep, wq), lambda b, g, r: (b, 0, g)),
            pl.BlockSpec((1, keep, wv), lambda b, g, r: (b, 0, g)),
        ),
        out_shape=out_shapes,
        scratch_shapes=[
            pltpu.VMEM((SUBLANES + rows, wq), F32),
            pltpu.VMEM((SUBLANES + rows, wq), F32),
            pltpu.VMEM((SUBLANES + rows, wv), F32),
            pltpu.VMEM((rows, wq), BF16),
            pltpu.VMEM((rows, wq), BF16),
            pltpu.VMEM((rows, wv), F32),
            pltpu.VMEM((rows, LANES), F32),
            pltpu.VMEM((n_chunks, n_pairs // 2, CHUNK, QUAD * CHUNK), BF16),
            pltpu.VMEM((n_chunks, n_pairs // 2, CHUNK, QUAD * CHUNK), BF16),
        ],
        compiler_params=_cparams("arbitrary", "arbitrary", "arbitrary"),
        name="delta_prompt",
    )(proj, proj, proj, proj, gates, gates_rows, conv_w, conv_w, conv_w, o_norm_w)


N_SAMPLE_INPUTS = 14


def _delta_sample_kernel(*refs, n_pairs, n_seqs, dec_seq, n_heads, aliased, layer):
    (q_ref, k_ref, v_ref, z_ref, cq_ref, ck_ref, cv_ref, gc_ref, gr_ref, wq_ref, wk_ref, wv_ref,
     onw_ref, s_in_ref) = refs[:N_SAMPLE_INPUTS]
    refs = refs[N_SAMPLE_INPUTS + 1:] if aliased else refs[N_SAMPLE_INPUTS:]
    y_ref, s_all_ref, ccq_ref, cck_ref, ccv_ref, eq_ref, ek_ref, ev_ref = refs
    if aliased:
        s_out_ref = s_all_ref
    else:
        s_out_ref = s_all_ref.at[layer]
        for other in range(s_all_ref.shape[0]):
            if other != layer:
                s_all_ref[other] = jnp.zeros(s_all_ref.shape[1:], F32)
    c = n_seqs * dec_seq
    pg = pl.program_id(1)
    hist = SUBLANES

    def conv(x_ref, cache_ref, ext_ref, w_ref, cc_ref):
        cols = ext_ref.shape[2]
        keep = CONV_WIDTH - 1
        ext_ref[:, 0:hist, :] = jnp.zeros((n_seqs, hist, cols), F32)
        ext_ref[:, hist - keep:hist, :] = cache_ref[...]
        ext_ref[:, hist:hist + dec_seq, :] = x_ref[...].reshape(n_seqs, dec_seq, cols)
        cc_ref[...] = ext_ref[:, hist + dec_seq - keep:hist + dec_seq, :]
        return _conv_taps(ext_ref, w_ref, hist).reshape(c, cols)

    xq = conv(q_ref, cq_ref, eq_ref, wq_ref, ccq_ref)
    xk = conv(k_ref, ck_ref, ek_ref, wk_ref, cck_ref)
    xv = conv(v_ref, cv_ref, ev_ref, wv_ref, ccv_ref)

    shift = lax.rem(LANES - 2 * n_pairs * pg, LANES)
    g_all = pltpu.roll(gc_ref[...], shift, 1)
    lane_wide = lax.broadcasted_iota(jnp.int32, (1, 2 * HEAD_DIM), 1)
    row_seq = lax.broadcasted_iota(jnp.int32, (c, 2 * HEAD_DIM), 0) // dec_seq
    pairs = range(n_pairs)
    seqs = range(n_seqs)
    hd = lambda p: slice(p * HEAD_DIM, (p + 1) * HEAD_DIM)

    qns = [_l2norm(xq[:, hd(p)]) * (HEAD_DIM ** -0.5) for p in pairs]
    kns = [_l2norm(xk[:, hd(p)]) for p in pairs]
    knb = [k.astype(BF16) for k in kns]

    groups, incl, strict = _quad_masks(c, dec_seq)
    qnb = [q.astype(BF16) for q in qns]
    grams = [_gram_quad(knb[2 * qd], qnb[2 * qd], knb[2 * qd + 1], qnb[2 * qd + 1])
             for qd in range(n_pairs // 2)]
    gcrs = [gr_ref[0, pl.ds(n_pairs * pg + p, 1), :] for p in pairs]
    l_cats, pis = [], []
    for qd in range(n_pairs // 2):
        ab = grams[qd]
        gcr = jnp.concatenate([gcrs[2 * qd], gcrs[2 * qd + 1]], axis=1)
        beta = _quad_select(g_all, QUAD * qd, groups)
        gc = _quad_select(g_all, n_heads + QUAD * qd, groups)
        dec = jnp.exp(jnp.where(incl, gc - gcr, NEG_BIG))
        pis.append((ab[c:] * dec).astype(BF16))
        l_cats.append(jnp.where(strict, beta * ab[:c] * dec, 0.0))
    pms = [pm.astype(BF16) for pm in _unit_lower_inverses_minus_identity(l_cats, dec_seq, groups)]
    half = lambda mats, p: mats[p // 2][:, (p % 2) * 2 * c:(p % 2 + 1) * 2 * c]

    s_cat = lambda p, s: jnp.concatenate([s_in_ref[s, 2 * p], s_in_ref[s, 2 * p + 1]], axis=1)
    kqs = [[jnp.dot(jnp.concatenate([kns[p][s * dec_seq:(s + 1) * dec_seq],
                                     qns[p][s * dec_seq:(s + 1) * dec_seq]], axis=0).astype(BF16),
                    s_cat(p, s).astype(BF16), preferred_element_type=F32) for s in seqs] for p in pairs]
    k_s = [jnp.concatenate([kqs[p][s][:dec_seq] for s in seqs], axis=0) for p in pairs]
    q_s = [jnp.concatenate([kqs[p][s][dec_seq:] for s in seqs], axis=0) for p in pairs]

    def last_gc(p, s):
        last = (s + 1) * dec_seq - 1
        return jnp.where(lane_wide < HEAD_DIM, gcrs[p][:, last:last + 1], gcrs[p][:, c + last:c + last + 1])

    gls = [[last_gc(p, s) for s in seqs] for p in pairs]
    beta_w = [_head_cols(g_all, 2 * p, HEAD_DIM) for p in pairs]
    gc_w = [_head_cols(g_all, n_heads + 2 * p, HEAD_DIM) for p in pairs]
    egc_w = [jnp.exp(g) for g in gc_w]
    rs = [beta_w[p] * (xv[:, 2 * p * HEAD_DIM:(2 * p + 2) * HEAD_DIM] - egc_w[p] * k_s[p]) for p in pairs]
    us = [rs[p] + jnp.dot(half(pms, p), _blockdiag_wide(rs[p]), preferred_element_type=F32) for p in pairs]
    os_ = [egc_w[p] * q_s[p] + jnp.dot(half(pis, p), _blockdiag_wide(us[p]), preferred_element_type=F32)
           for p in pairs]
    for p in pairs:
        gl_rows = jnp.concatenate([jnp.broadcast_to(g, (dec_seq, 2 * HEAD_DIM)) for g in gls[p]], axis=0)
        w = jnp.exp(gl_rows - gc_w[p]) * us[p]
        for s in seqs:
            w_s = jnp.where(row_seq == s, w, 0.0).astype(BF16)
            s_new = s_cat(p, s) * jnp.exp(gls[p][s]) + lax.dot_general(
                knb[p], w_s, (((0,), (0,)), ((), ())), preferred_element_type=F32)
            s_out_ref[s, 2 * p] = s_new[:, :HEAD_DIM]
            s_out_ref[s, 2 * p + 1] = s_new[:, HEAD_DIM:]
    for p in pairs:
        for e in range(2):
            hs = slice((2 * p + e) * HEAD_DIM, (2 * p + e + 1) * HEAD_DIM)
            y_ref[:, hs] = _gated_out(os_[p][:, e * HEAD_DIM:(e + 1) * HEAD_DIM], z_ref[:, hs], onw_ref[...])


def _delta_sample(proj, gates, gates_rows, conv_cache, state, prev_state_out, conv_w, o_norm_w, layer,
                  row_off, batch, dec_seq, n_heads, d_inner, n_pairs=4, n_seqs=8):
    qk_dim = n_heads // 2 * HEAD_DIM
    c = n_seqs * dec_seq
    wq, wv = n_pairs * HEAD_DIM, 2 * n_pairs * HEAD_DIM
    n_pg = n_heads // (2 * n_pairs)
    rb0 = row_off // c
    aliased = prev_state_out is not None
    kern = functools.partial(_delta_sample_kernel, n_pairs=n_pairs, n_seqs=n_seqs, dec_seq=dec_seq,
                             n_heads=n_heads, aliased=aliased, layer=layer)
    row_map = lambda col0: (lambda i, g: (rb0 + i, col0 + g))
    keep = CONV_WIDTH - 1
    in_specs = [
        pl.BlockSpec((c, wq), row_map(0)),
        pl.BlockSpec((c, wq), row_map(qk_dim // wq)),
        pl.BlockSpec((c, wv), row_map(2 * qk_dim // wv)),
        pl.BlockSpec((c, wv), row_map((2 * qk_dim + d_inner) // wv)),
        pl.BlockSpec((None, n_seqs, keep, wq), lambda i, g: (layer, i, 0, g)),
        pl.BlockSpec((None, n_seqs, keep, wq), lambda i, g: (layer, i, 0, qk_dim // wq + g)),
        pl.BlockSpec((None, n_seqs, keep, wv), lambda i, g: (layer, i, 0, 2 * qk_dim // wv + g)),
        pl.BlockSpec((c, LANES), lambda i, g: (rb0 + i, 0)),
        pl.BlockSpec((1, n_heads // 2, 2 * c), lambda i, g: (i, 0, 0)),
        pl.BlockSpec((None, CONV_WIDTH, wq), lambda i, g: (layer, 0, g)),
        pl.BlockSpec((None, CONV_WIDTH, wq), lambda i, g: (layer, 0, qk_dim // wq + g)),
        pl.BlockSpec((None, CONV_WIDTH, wv), lambda i, g: (layer, 0, 2 * qk_dim // wv + g)),
        pl.BlockSpec((None, 1, HEAD_DIM), lambda i, g: (layer, 0, 0)),
        pl.BlockSpec((None, n_seqs, 2 * n_pairs, HEAD_DIM, HEAD_DIM), lambda i, g: (layer, i, g, 0, 0)),
    ]
    args = [proj, proj, proj, proj, conv_cache, conv_cache, conv_cache, gates, gates_rows,
            conv_w, conv_w, conv_w, o_norm_w, state]
    aliases = {}
    if aliased:
        in_specs.append(pl.BlockSpec(memory_space=pl.ANY))
        args.append(prev_state_out)
        aliases = {N_SAMPLE_INPUTS: 1}
    state_blk = (n_seqs, 2 * n_pairs, HEAD_DIM, HEAD_DIM)
    if aliased:
        state_spec = pl.BlockSpec((None,) + state_blk, lambda i, g: (layer, i, g, 0, 0))
    else:
        state_spec = pl.BlockSpec((state.shape[0],) + state_blk, lambda i, g: (0, i, g, 0, 0))
    out_shapes = (
        jax.ShapeDtypeStruct((batch * dec_seq, d_inner), BF16),
        jax.ShapeDtypeStruct(state.shape, F32),
        jax.ShapeDtypeStruct((batch, keep, qk_dim), F32),
        jax.ShapeDtypeStruct((batch, keep, qk_dim), F32),
        jax.ShapeDtypeStruct((batch, keep, d_inner), F32),
    )
    return pl.pallas_call(
        kern,
        grid=(batch // n_seqs, n_pg),
        in_specs=in_specs,
        out_specs=(
            pl.BlockSpec((c, wv), lambda i, g: (i, g)),
            state_spec,
            pl.BlockSpec((n_seqs, keep, wq), lambda i, g: (i, 0, g)),
            pl.BlockSpec((n_seqs, keep, wq), lambda i, g: (i, 0, g)),
            pl.BlockSpec((n_seqs, keep, wv), lambda i, g: (i, 0, g)),
        ),
        out_shape=out_shapes,
        scratch_shapes=[
            pltpu.VMEM((n_seqs, SUBLANES + dec_seq, wq), F32),
            pltpu.VMEM((n_seqs, SUBLANES + dec_seq, wq), F32),
            pltpu.VMEM((n_seqs, SUBLANES + dec_seq, wv), F32),
        ],
        input_output_aliases=aliases,
        compiler_params=_cparams("arbitrary", "arbitrary"),
        name="delta_sample",
    )(*args)


def _window_sums(x, win, axis):
    s = x
    w = 1
    while w < win:
        s = s + pltpu.roll(s, w, axis)
        w *= 2
    return s


def _pool_finish(pooled, z, wg_ref, sc_ref):
    mixed = jnp.dot(pooled.astype(BF16), wg_ref[...], preferred_element_type=F32)
    return (mixed * sc_ref[...] * _silu(z)).astype(BF16)


def _pool_prompt_kernel(u_ref, prev_ref, z_ref, wg_ref, sc_ref, y_ref, cache_ref, ext_ref, *, tt, nt):
    g = pl.program_id(0)
    t = pl.program_id(2)
    hist = POOL_BUF + 1

    @pl.when(t == 0)
    def _():
        ext_ref[0:hist, :] = jnp.zeros((hist, ext_ref.shape[1]), F32)

    @pl.when(t > 0)
    def _():
        ext_ref[0:hist, :] = prev_ref[...]

    ext_ref[hist:hist + tt, :] = u_ref[...]
    pos = t * tt + lax.broadcasted_iota(jnp.int32, (tt, 1), 0)
    for gi, win in enumerate(POOL_WINDOWS):
        @pl.when(g == gi)
        def _(win=win):
            acc = _window_sums(ext_ref[...], win, 0)[hist:]
            cnt = jnp.minimum(pos + 1, win).astype(F32)
            y_ref[...] = _pool_finish(acc / cnt - u_ref[...], z_ref[...], wg_ref, sc_ref)

    @pl.when(t == nt - 1)
    def _():
        cache_ref[0] = ext_ref[hist + tt - POOL_BUF:hist + tt, :]


def _pool_prompt(proj, w_group, scale, layer, batch, seq, d_inner, tt=512):
    gd = d_inner // len(POOL_WINDOWS)
    nt = seq // tt
    hist = POOL_BUF + 1
    kern = functools.partial(_pool_prompt_kernel, tt=tt, nt=nt)
    n_g = len(POOL_WINDOWS)
    return pl.pallas_call(
        kern,
        grid=(n_g, batch, nt),
        in_specs=[
            pl.BlockSpec((tt, gd), lambda g, b, t: (b * nt + t, g)),
            pl.BlockSpec((hist, gd), lambda g, b, t: (jnp.maximum((b * nt + t) * (tt // hist) - 1, 0), g)),
            pl.BlockSpec((tt, gd), lambda g, b, t: (b * nt + t, n_g + g)),
            pl.BlockSpec((None, None, gd, gd), lambda g, b, t: (layer, g, 0, 0)),
            pl.BlockSpec((None, 1, gd), lambda g, b, t: (layer, 0, g)),
        ],
        out_specs=(
            pl.BlockSpec((tt, gd), lambda g, b, t: (b * nt + t, g)),
            pl.BlockSpec((1, POOL_BUF, gd), lambda g, b, t: (b, 0, g)),
        ),
        out_shape=(
            jax.ShapeDtypeStruct((batch * seq, d_inner), BF16),
            jax.ShapeDtypeStruct((batch, POOL_BUF, d_inner), F32),
        ),
        scratch_shapes=[pltpu.VMEM((hist + tt, gd), F32)],
        compiler_params=_cparams("arbitrary", "arbitrary", "arbitrary"),
        name="pool_prompt",
    )(proj, proj, proj, w_group, scale)


def _pool_sample_kernel(u_ref, c_ref, z_ref, wg_ref, sc_ref, y_ref, cache_ref, ext_ref, *, sb, dec_seq):
    g = pl.program_id(0)
    hist = POOL_BUF + 1
    gd = ext_ref.shape[2]
    ext_ref[:, 0:SUBLANES, :] = jnp.zeros((sb, SUBLANES, gd), F32)
    ext_ref[:, hist - POOL_BUF:hist, :] = c_ref[...]
    ext_ref[:, hist:hist + dec_seq, :] = u_ref[...].reshape(sb, dec_seq, gd)
    pos = PAST_LEN + lax.broadcasted_iota(jnp.int32, (1, dec_seq, 1), 1)
    for gi, win in enumerate(POOL_WINDOWS):
        @pl.when(g == gi)
        def _(win=win):
            x = ext_ref[:, hist:hist + dec_seq, :]
            acc = _window_sums(ext_ref[...], win, 1)[:, hist:, :]
            cnt = jnp.minimum(pos + 1, win).astype(F32)
            pooled = (acc / cnt - x).reshape(sb * dec_seq, gd)
            y_ref[...] = _pool_finish(pooled, z_ref[...], wg_ref, sc_ref)

    cache_ref[...] = ext_ref[:, hist + dec_seq - POOL_BUF:hist + dec_seq, :]


def _pool_sample(proj, cache, w_group, scale, layer, row_off, batch, dec_seq, d_inner, sb=16):
    gd = d_inner // len(POOL_WINDOWS)
    n_g = len(POOL_WINDOWS)
    rows = sb * dec_seq
    rb0 = row_off // rows
    hist = POOL_BUF + 1
    kern = functools.partial(_pool_sample_kernel, sb=sb, dec_seq=dec_seq)
    return pl.pallas_call(
        kern,
        grid=(n_g, batch // sb),
        in_specs=[
            pl.BlockSpec((rows, gd), lambda g, i: (rb0 + i, g)),
            pl.BlockSpec((None, sb, POOL_BUF, gd), lambda g, i: (layer, i, 0, g)),
            pl.BlockSpec((rows, gd), lambda g, i: (rb0 + i, n_g + g)),
            pl.BlockSpec((None, None, gd, gd), lambda g, i: (layer, g, 0, 0)),
            pl.BlockSpec((None, 1, gd), lambda g, i: (layer, 0, g)),
        ],
        out_specs=(
            pl.BlockSpec((rows, gd), lambda g, i: (i, g)),
            pl.BlockSpec((sb, POOL_BUF, gd), lambda g, i: (i, 0, g)),
        ),
        out_shape=(
            jax.ShapeDtypeStruct((batch * dec_seq, d_inner), BF16),
            jax.ShapeDtypeStruct((batch, POOL_BUF, d_inner), F32),
        ),
        scratch_shapes=[pltpu.VMEM((sb, hist + dec_seq, gd), F32)],
        compiler_params=_cparams("arbitrary", "arbitrary"),
        name="pool_sample",
    )(proj, cache, proj, w_group, scale)


def kernel(x_prompt, x_sample, state_delta, cache_conv, cache_pool, norm_w, final_norm_w, delta_w_in,
           delta_conv_w, delta_a_log, delta_dt_bias, delta_o_norm_w, delta_w_out, pool_w_in, pool_w_group,
           pool_scale, pool_w_out):
    batch, seq, d_model = x_prompt.shape
    dec_batch, dec_seq, _ = x_sample.shape
    depth = norm_w.shape[0]
    n_heads = state_delta.shape[2]
    d_inner = n_heads * HEAD_DIM
    conv_dim = cache_conv.shape[3]
    qk_dim = (conv_dim - d_inner) // 2
    mp, ms = batch * seq, dec_batch * dec_seq
    m = mp + ms
    n_main = conv_dim + d_inner
    n_seqs = CHUNK // dec_seq

    h = (x_prompt.reshape(mp, d_model), x_sample.reshape(ms, d_model))

    delta_w_out_b = delta_w_out.astype(BF16)
    pool_w_out_b = pool_w_out.astype(BF16)
    pool_w_group_b = pool_w_group.astype(BF16)
    delta_w_in_t = jnp.swapaxes(delta_w_in, 1, 2)
    w_tail_t = jnp.pad(delta_w_in_t[:, n_main:, :], ((0, 0), (0, LANES - 2 * n_heads), (0, 0)))
    onw = delta_o_norm_w.reshape(-1, 1, HEAD_DIM)
    pool_scale3 = pool_scale.reshape(pool_scale.shape[0], 1, d_inner)

    sd_p, cc_p, cp_p, cc_s, cp_s = [], [], [], [], []
    sd_s = None
    for i in range(depth):
        j = i // 2
        if isinstance(h, tuple):
            xn = _rmsnorm_pair(h[0], h[1], norm_w[i], BF16)
        else:
            xn = _rmsnorm(h, norm_w[i], BF16, m)
        if i % 2 == 0:
            proj = _matmul(xn, delta_w_in_t, j, n_main, tm=3072, tn=512, w_transposed=True)
            tail = _matmul(xn, w_tail_t, j, LANES, tm=3072, tn=LANES, w_transposed=True)
            gates = _gates(tail, delta_a_log[j], delta_dt_bias[j], mp, n_heads, dec_seq)
            gcum = gates[:, n_heads:2 * n_heads]
            rows_p = gcum[:mp].reshape(batch, seq // CHUNK, CHUNK, n_heads // 2, 2)
            rows_p = rows_p.transpose(0, 1, 3, 4, 2).reshape(batch, seq // (CHUNKS_PER_STEP * CHUNK), CHUNKS_PER_STEP * (n_heads // 2), 2 * CHUNK)
            rows_s = gcum[mp:].reshape(ms // CHUNK, CHUNK, n_heads // 2, 2)
            rows_s = rows_s.transpose(0, 2, 3, 1).reshape(ms // CHUNK, n_heads // 2, 2 * CHUNK)
            y_p, sp, cq, ck, cv = _delta_prompt(proj, gates, rows_p, delta_conv_w, onw, j, batch, seq,
                                                n_heads, d_inner)
            y_s, sd_s, cqs, cks, cvs = _delta_sample(proj, gates, rows_s, cache_conv, state_delta, sd_s,
                                                     delta_conv_w, onw, j, mp, dec_batch, dec_seq, n_heads,
                                                     d_inner, n_seqs=n_seqs)
            sd_p.append(sp)
            cc_p.append(jnp.concatenate([cq, ck, cv], axis=-1))
            cc_s.append(jnp.concatenate([cqs, cks, cvs], axis=-1))
            w_out, lw = delta_w_out_b, j
        else:
            proj = _matmul(xn, pool_w_in, j, 2 * d_inner, tm=3072, tn=512)
            y_p, pp = _pool_prompt(proj, pool_w_group_b, pool_scale3, j, batch, seq, d_inner)
            y_s, ps = _pool_sample(proj, cache_pool, pool_w_group_b, pool_scale3, j, mp, dec_batch, dec_seq,
                                   d_inner)
            cp_p.append(pp)
            cp_s.append(ps)
            w_out, lw = pool_w_out_b, j
        h = _matmul_residual(y_p, y_s, w_out, lw, h, tn=256 if isinstance(h, tuple) else 512)

    y_prompt = _rmsnorm(h, final_norm_w, F32, mp, 0).reshape(batch, seq, d_model)
    y_sample = _rmsnorm(h, final_norm_w, F32, ms, mp).reshape(dec_batch, dec_seq, d_model)
    return (y_prompt, y_sample, jnp.stack(sd_p), jnp.stack(cc_p), jnp.stack(cp_p),
            sd_s, jnp.stack(cc_s), jnp.stack(cp_s))
```

```python
import functools

import jax
import jax.numpy as jnp
from jax import lax
from jax.experimental import pallas as pl
from jax.experimental.pallas import tpu as pltpu

F32 = jnp.float32
BF16 = jnp.bfloat16

HEAD_DIM = 128
CONV_WIDTH = 4
CHUNK = 64
CHUNKS_PER_STEP = 4
POOL_WINDOWS = (2, 4, 8, 16)
POOL_BUF = max(POOL_WINDOWS) - 1
PAST_LEN = 16384
NORM_EPS = 1e-6
NEG_BIG = -1e30

LANES = 128
SUBLANES = 8
VMEM_LIMIT = 56 * 1024 * 1024


def _cparams(*sem):
    return pltpu.CompilerParams(dimension_semantics=sem, vmem_limit_bytes=VMEM_LIMIT)


def _silu(x):
    return x * jax.nn.sigmoid(x)


def _rmsnorm_kernel(x_ref, w_ref, o_ref):
    x = x_ref[...]
    y = x * lax.rsqrt(jnp.mean(x * x, axis=-1, keepdims=True) + NORM_EPS)
    o_ref[...] = (y * w_ref[...]).astype(o_ref.dtype)


def _rmsnorm_pair_kernel(xp_ref, xs_ref, w_ref, o_ref, *, n_prompt_blocks):
    i = pl.program_id(0)

    @pl.when(i < n_prompt_blocks)
    def _():
        _rmsnorm_kernel(xp_ref, w_ref, o_ref)

    @pl.when(i >= n_prompt_blocks)
    def _():
        _rmsnorm_kernel(xs_ref, w_ref, o_ref)


def _rmsnorm_pair(x_p, x_s, w, out_dtype, tm=512):
    d = x_p.shape[1]
    npb = x_p.shape[0] // tm
    m = x_p.shape[0] + x_s.shape[0]
    return pl.pallas_call(
        functools.partial(_rmsnorm_pair_kernel, n_prompt_blocks=npb),
        grid=(m // tm,),
        in_specs=[pl.BlockSpec((tm, d), lambda i: (jnp.minimum(i, npb - 1), 0)),
                  pl.BlockSpec((tm, d), lambda i: (jnp.maximum(i - npb, 0), 0)),
                  pl.BlockSpec((1, d), lambda i: (0, 0))],
        out_specs=pl.BlockSpec((tm, d), lambda i: (i, 0)),
        out_shape=jax.ShapeDtypeStruct((m, d), out_dtype),
        compiler_params=_cparams("arbitrary"),
        name="rmsnorm",
    )(x_p, x_s, w.reshape(1, d))


def _rmsnorm(x, w, out_dtype, rows, row_off=0, tm=512):
    d = x.shape[1]
    off = row_off // tm
    return pl.pallas_call(
        _rmsnorm_kernel,
        grid=(rows // tm,),
        in_specs=[pl.BlockSpec((tm, d), lambda i: (i + off, 0)),
                  pl.BlockSpec((1, d), lambda i: (0, 0))],
        out_specs=pl.BlockSpec((tm, d), lambda i: (i, 0)),
        out_shape=jax.ShapeDtypeStruct((rows, d), out_dtype),
        compiler_params=_cparams("arbitrary"),
        name="rmsnorm",
    )(x, w.reshape(1, d))


def _matmul_kernel(x_ref, w_ref, o_ref, *, w_transposed):
    w = w_ref[...].astype(BF16)
    contract = (((1,), (1 if w_transposed else 0,)), ((), ()))
    o_ref[...] = lax.dot_general(x_ref[...], w, contract, preferred_element_type=F32)


def _matmul(x, w, layer, n_cols, tm, tn, w_transposed=False):
    m, k = x.shape
    if w_transposed:
        w_spec = pl.BlockSpec((None, tn, k), lambda i, j: (layer, j, 0))
    else:
        w_spec = pl.BlockSpec((None, k, tn), lambda i, j: (layer, 0, j))
    return pl.pallas_call(
        functools.partial(_matmul_kernel, w_transposed=w_transposed),
        grid=(m // tm, n_cols // tn),
        in_specs=[pl.BlockSpec((tm, k), lambda i, j: (i, 0)), w_spec],
        out_specs=pl.BlockSpec((tm, tn), lambda i, j: (i, j)),
        out_shape=jax.ShapeDtypeStruct((m, n_cols), F32),
        compiler_params=_cparams("arbitrary", "arbitrary"),
        name="in_proj",
    )(x, w)


def _matmul_res_kernel(yp_ref, ys_ref, w_ref, *refs, n_prompt_blocks):
    hp_ref, hs_ref, o_ref = refs if len(refs) == 3 else (refs[0], refs[0], refs[1])
    i = pl.program_id(0)
    w = w_ref[...]

    @pl.when(i < n_prompt_blocks)
    def _():
        o_ref[...] = hp_ref[...] + jnp.dot(yp_ref[...], w, preferred_element_type=F32)

    @pl.when(i >= n_prompt_blocks)
    def _():
        o_ref[...] = hs_ref[...] + jnp.dot(ys_ref[...], w, preferred_element_type=F32)


def _matmul_residual(y_p, y_s, w, layer, h, tm=1024, tn=512):
    rows_p, k = y_p.shape
    npb = rows_p // tm
    prompt_rows = lambda i, j: (jnp.minimum(i, npb - 1), j)
    sample_rows = lambda i, j: (jnp.maximum(i - npb, 0), j)
    if isinstance(h, tuple):
        m, n = h[0].shape[0] + h[1].shape[0], h[0].shape[1]
        h_specs = [pl.BlockSpec((tm, tn), prompt_rows),
                   pl.BlockSpec((tm, tn), sample_rows, pipeline_mode=pl.Buffered(1))]
        aliases = {}
    else:
        m, n = h.shape
        h, h_specs, aliases = (h,), [pl.BlockSpec((tm, tn), lambda i, j: (i, j))], {3: 0}
    return pl.pallas_call(
        functools.partial(_matmul_res_kernel, n_prompt_blocks=npb),
        grid=(m // tm, n // tn),
        in_specs=[pl.BlockSpec((tm, k), lambda i, j: (jnp.minimum(i, npb - 1), 0)),
                  pl.BlockSpec((tm, k), lambda i, j: (jnp.maximum(i - npb, 0), 0),
                               pipeline_mode=pl.Buffered(1)),
                  pl.BlockSpec((None, k, tn), lambda i, j: (layer, 0, j))] + h_specs,
        out_specs=pl.BlockSpec((tm, tn), lambda i, j: (i, j)),
        out_shape=jax.ShapeDtypeStruct((m, n), F32),
        input_output_aliases=aliases,
        compiler_params=_cparams("arbitrary", "arbitrary"),
        name="out_proj",
    )(y_p, y_s, w, *h)


def _gates_kernel(t_ref, alog_ref, dt_ref, o_ref, *, n_prompt_blocks, n_heads, dec_seq):
    i = pl.program_id(0)
    x = t_ref[...]
    lane = lax.broadcasted_iota(jnp.int32, x.shape, 1)
    row = lax.broadcasted_iota(jnp.int32, x.shape, 0)
    beta = jax.nn.sigmoid(x)
    a = x + dt_ref[...]
    softplus = jnp.maximum(a, 0.0) + jnp.log(1.0 + jnp.exp(-jnp.abs(a)))
    g = -jnp.exp(alog_ref[...]) * softplus
    seg = jnp.where(i < n_prompt_blocks, CHUNK, dec_seq)
    pos = row & (seg - 1)
    cs = g
    s = 1
    while s < CHUNK:
        cs = cs + jnp.where(pos >= s, pltpu.roll(cs, s, 0), 0.0)
        s *= 2
    o_ref[...] = jnp.where(lane < n_heads, beta, jnp.where(lane < 2 * n_heads, cs, 0.0))


def _gates(tail, a_log, dt_bias, n_prompt_rows, n_heads, dec_seq, tm=1024):
    m = tail.shape[0]
    pad = lambda v: jnp.zeros((1, LANES), F32).at[0, n_heads:2 * n_heads].set(v.astype(F32))
    kern = functools.partial(_gates_kernel, n_prompt_blocks=n_prompt_rows // tm, n_heads=n_heads,
                             dec_seq=dec_seq)
    return pl.pallas_call(
        kern,
        grid=(m // tm,),
        in_specs=[pl.BlockSpec((tm, LANES), lambda i: (i, 0)),
                  pl.BlockSpec((1, LANES), lambda i: (0, 0)),
                  pl.BlockSpec((1, LANES), lambda i: (0, 0))],
        out_specs=pl.BlockSpec((tm, LANES), lambda i: (i, 0)),
        out_shape=jax.ShapeDtypeStruct((m, LANES), F32),
        compiler_params=_cparams("arbitrary"),
        name="delta_gates",
    )(tail, pad(a_log), pad(dt_bias))


QUAD = 4


def _blockdiag(y, masks):
    yb = y.astype(BF16)
    zero = jnp.zeros_like(yb)
    return jnp.concatenate([jnp.where(m, yb, zero) for m in masks], axis=0)


def _blockdiag_wide(r):
    rb = r.astype(BF16)
    z = jnp.zeros_like(rb[:, :HEAD_DIM])
    return jnp.concatenate([jnp.concatenate([rb[:, :HEAD_DIM], z], axis=1),
                            jnp.concatenate([z, rb[:, HEAD_DIM:]], axis=1)], axis=0)


def _inverse_chain_stages(l_cats, seg, masks, out):
    c = l_cats[0].shape[0]
    dot = lambda a, b: jnp.dot(a, _blockdiag(b, masks), preferred_element_type=F32)
    ps = [-l for l in l_cats]
    lbs = [l.astype(BF16) for l in l_cats]
    qs = [dot(lb, lb) for lb in lbs]
    yield
    n_steps = seg.bit_length() - 2
    for step in range(n_steps):
        qbs = [q.astype(BF16) for q in qs]
        if step < n_steps - 1:
            pqs = [dot(jnp.concatenate([p.astype(BF16), qb], axis=0), qb) for p, qb in zip(ps, qbs)]
            ps = [p + q + pq[:c] for p, q, pq in zip(ps, qs, pqs)]
            qs = [pq[c:] for pq in pqs]
        else:
            pqs = [dot(p.astype(BF16), qb) for p, qb in zip(ps, qbs)]
            ps = [p + q + pq for p, q, pq in zip(ps, qs, pqs)]
        yield
    out.extend(ps)


def _unit_lower_inverses_minus_identity(l_cats, seg, masks):
    out = []
    for _ in _inverse_chain_stages(l_cats, seg, masks, out):
        pass
    return out


def _interleave(*stage_generators):
    live = list(stage_generators)
    while live:
        for g in list(live):
            try:
                next(g)
            except StopIteration:
                live.remove(g)


def _gram_quad(kn_a, qn_a, kn_b, qn_b):
    zero = jnp.zeros_like(kn_a)
    lhs = jnp.concatenate([jnp.concatenate([kn_a, kn_b], axis=1),
                           jnp.concatenate([qn_a, qn_b], axis=1)], axis=0)
    ka = jnp.concatenate([kn_a, zero], axis=1)
    kb = jnp.concatenate([zero, kn_b], axis=1)
    rhs = jnp.concatenate([ka, ka, kb, kb], axis=0)
    return lax.dot_general(lhs, rhs, (((1,), (1,)), ((), ())), preferred_element_type=F32)


def _quad_masks(c, seg):
    lane = lax.broadcasted_iota(jnp.int32, (c, QUAD * c), 1)
    ii = lax.broadcasted_iota(jnp.int32, (c, QUAD * c), 0)
    jj = lane & (c - 1)
    groups = [(lane // c) == g for g in range(QUAD)]
    incl = (ii >= jj) if seg == c else ((ii >= jj) & ((ii // seg) == (jj // seg)))
    return groups, incl, ii > jj


def _quad_select(g_all, base, groups):
    out = g_all[:, base + QUAD - 1:base + QUAD]
    for g in range(QUAD - 2, -1, -1):
        out = jnp.where(groups[g], g_all[:, base + g:base + g + 1], out)
    return out


def _head_cols(g_all, idx, width):
    c = g_all.shape[0]
    return jnp.concatenate([jnp.broadcast_to(g_all[:, idx:idx + 1], (c, width)),
                            jnp.broadcast_to(g_all[:, idx + 1:idx + 2], (c, width))], axis=1)


def _gated_out(o, z, onw):
    on = o * lax.rsqrt(jnp.mean(o * o, axis=-1, keepdims=True) + NORM_EPS) * onw
    return (on * _silu(z)).astype(BF16)


def _conv_taps(ext_ref, w_ref, start):
    assert CONV_WIDTH == 4 and start >= CONV_WIDTH - 1 and start % SUBLANES == 0
    axis = len(ext_ref.shape) - 2
    x = ext_ref[...]
    x1 = pltpu.roll(x, 1, axis)
    a = w_ref[3:4, :] * x + w_ref[2:3, :] * x1
    b = w_ref[1:2, :] * x + w_ref[0:1, :] * x1
    y = a + pltpu.roll(b, 2, axis)
    y = y[start:] if axis == 0 else y[:, start:]
    return _silu(y)


def _l2norm(x):
    return x * lax.rsqrt(jnp.sum(x * x, axis=-1, keepdims=True) + NORM_EPS)


def _delta_prompt_kernel(q_ref, k_ref, v_ref, z_ref, gc_ref, gr_ref, wq_ref, wk_ref, wv_ref, onw_ref,
                         y_ref, s_ref, ccq_ref, cck_ref, ccv_ref,
                         eq_ref, ek_ref, ev_ref, qn_ref, kn_ref, vv_ref, gt_ref, p_ref, pi_ref,
                         *, n_pairs, n_chunks, n_heads, batch_chunks):
    c = CHUNK
    rows = n_chunks * c
    pg = pl.program_id(1)
    cb = pl.program_id(2)
    hist = SUBLANES

    @pl.when(cb == 0)
    def _():
        eq_ref[0:hist, :] = jnp.zeros((hist, eq_ref.shape[1]), F32)
        ek_ref[0:hist, :] = jnp.zeros((hist, ek_ref.shape[1]), F32)
        ev_ref[0:hist, :] = jnp.zeros((hist, ev_ref.shape[1]), F32)
        s_ref[...] = jnp.zeros(s_ref.shape, F32)

    eq_ref[hist:hist + rows, :] = q_ref[...]
    ek_ref[hist:hist + rows, :] = k_ref[...]
    ev_ref[hist:hist + rows, :] = v_ref[...]
    xq = _conv_taps(eq_ref, wq_ref, hist)
    xk = _conv_taps(ek_ref, wk_ref, hist)
    vv_ref[...] = _conv_taps(ev_ref, wv_ref, hist)
    for p in range(n_pairs):
        sl = slice(p * HEAD_DIM, (p + 1) * HEAD_DIM)
        qn_ref[:, sl] = (_l2norm(xq[:, sl]) * (HEAD_DIM ** -0.5)).astype(BF16)
        kn_ref[:, sl] = _l2norm(xk[:, sl]).astype(BF16)
    keep = CONV_WIDTH - 1
    ccq_ref[0] = eq_ref[hist + rows - keep:hist + rows, :]
    cck_ref[0] = ek_ref[hist + rows - keep:hist + rows, :]
    ccv_ref[0] = ev_ref[hist + rows - keep:hist + rows, :]
    eq_ref[0:hist, :] = eq_ref[rows:rows + hist, :]
    ek_ref[0:hist, :] = ek_ref[rows:rows + hist, :]
    ev_ref[0:hist, :] = ev_ref[rows:rows + hist, :]

    shift = lax.rem(LANES - 2 * n_pairs * pg, LANES)
    gt_ref[...] = pltpu.roll(gc_ref[...], shift, 1)

    n_quads = n_pairs // 2
    groups, incl, strict = _quad_masks(c, c)
    pair_row = lambda ci, p: ci * (n_heads // 2) + n_pairs * pg + p
    lane_wide = lax.broadcasted_iota(jnp.int32, (1, 2 * HEAD_DIM), 1)
    pairs = range(n_pairs)

    def matrices(chunks):
        grams = {}
        for ci in chunks:
            hd = lambda ref, p: ref[ci * c:(ci + 1) * c, p * HEAD_DIM:(p + 1) * HEAD_DIM]
            for qd in range(n_quads):
                grams[ci, qd] = _gram_quad(hd(kn_ref, 2 * qd), hd(qn_ref, 2 * qd), hd(kn_ref, 2 * qd + 1),
                                           hd(qn_ref, 2 * qd + 1))
        yield
        l_cats = []
        for ci in chunks:
            g_all = gt_ref[ci * c:(ci + 1) * c, :]
            for qd in range(n_quads):
                ab = grams[ci, qd]
                gcr = jnp.concatenate([gr_ref[0, 0, pl.ds(pair_row(ci, 2 * qd), 1), :],
                                       gr_ref[0, 0, pl.ds(pair_row(ci, 2 * qd + 1), 1), :]], axis=1)
                beta = _quad_select(g_all, QUAD * qd, groups)
                gc = _quad_select(g_all, n_heads + QUAD * qd, groups)
                dec = jnp.exp(jnp.where(incl, gc - gcr, NEG_BIG))
                pi_ref[ci, qd] = (ab[c:] * dec).astype(BF16)
                l_cats.append(jnp.where(strict, beta * ab[:c] * dec, 0.0))
        pms = []
        yield from _inverse_chain_stages(l_cats, c, groups, pms)
        for n, pm in enumerate(pms):
            p_ref[chunks[n // n_quads], n % n_quads] = pm.astype(BF16)

    def recurrence(chunks):
        for ci in chunks:
            r0 = ci * c
            g_all = gt_ref[r0:r0 + c, :]
            eg_all = jnp.exp(g_all)
            half = lambda ref, p: ref[ci, p // 2, :, (p % 2) * 2 * c:(p % 2 + 1) * 2 * c]
            kns = [kn_ref[r0:r0 + c, p * HEAD_DIM:(p + 1) * HEAD_DIM] for p in pairs]
            qns = [qn_ref[r0:r0 + c, p * HEAD_DIM:(p + 1) * HEAD_DIM] for p in pairs]
            s_cat = lambda p: jnp.concatenate([s_ref[0, 2 * p], s_ref[0, 2 * p + 1]], axis=1)
            kqs = [jnp.dot(jnp.concatenate([kns[p], qns[p]], axis=0), s_cat(p).astype(BF16),
                           preferred_element_type=F32) for p in pairs]
            yield
            beta_w = [_head_cols(g_all, 2 * p, HEAD_DIM) for p in pairs]
            gc_w = [_head_cols(g_all, n_heads + 2 * p, HEAD_DIM) for p in pairs]
            egc_w = [_head_cols(eg_all, n_heads + 2 * p, HEAD_DIM) for p in pairs]
            rs = [beta_w[p] * (vv_ref[r0:r0 + c, 2 * p * HEAD_DIM:(2 * p + 2) * HEAD_DIM]
                               - egc_w[p] * kqs[p][:c]) for p in pairs]
            us = [rs[p] + jnp.dot(half(p_ref, p), _blockdiag_wide(rs[p]), preferred_element_type=F32)
                  for p in pairs]
            yield
            for p in pairs:
                gcr = gr_ref[0, 0, pl.ds(pair_row(ci, p), 1), :]
                gl_w = jnp.where(lane_wide < HEAD_DIM, gcr[:, c - 1:c], gcr[:, 2 * c - 1:2 * c])
                w = (jnp.exp(gl_w - gc_w[p]) * us[p]).astype(BF16)
                s_new = s_cat(p) * jnp.exp(gl_w) + lax.dot_general(
                    kns[p], w, (((0,), (0,)), ((), ())), preferred_element_type=F32)
                s_ref[0, 2 * p] = s_new[:, :HEAD_DIM]
                s_ref[0, 2 * p + 1] = s_new[:, HEAD_DIM:]
            yield
            os_ = [egc_w[p] * kqs[p][c:] + jnp.dot(half(pi_ref, p), _blockdiag_wide(us[p]),
                                                   preferred_element_type=F32) for p in pairs]
            yield
            for p in pairs:
                for e in range(2):
                    hs = slice((2 * p + e) * HEAD_DIM, (2 * p + e + 1) * HEAD_DIM)
                    y_ref[r0:r0 + c, hs] = _gated_out(os_[p][:, e * HEAD_DIM:(e + 1) * HEAD_DIM],
                                                      z_ref[r0:r0 + c, hs], onw_ref[...])

    for c0 in range(0, n_chunks, batch_chunks):
        _interleave(matrices(list(range(c0, c0 + batch_chunks))))
    _interleave(recurrence(range(n_chunks)))


def _delta_prompt(proj, gates, gates_rows, conv_w, o_norm_w, layer, batch, seq, n_heads, d_inner,
                  n_pairs=8, n_chunks=CHUNKS_PER_STEP, batch_chunks=2):
    qk_dim = n_heads // 2 * HEAD_DIM
    rows = n_chunks * CHUNK
    nrb = seq // rows
    wq, wv = n_pairs * HEAD_DIM, 2 * n_pairs * HEAD_DIM
    n_pg = n_heads // (2 * n_pairs)
    kern = functools.partial(_delta_prompt_kernel, n_pairs=n_pairs, n_chunks=n_chunks, n_heads=n_heads,
                             batch_chunks=batch_chunks)
    row_map = lambda col0: (lambda b, g, r: (b * nrb + r, col0 + g))
    keep = CONV_WIDTH - 1
    out_shapes = (
        jax.ShapeDtypeStruct((batch * seq, d_inner), BF16),
        jax.ShapeDtypeStruct((batch, n_heads, HEAD_DIM, HEAD_DIM), F32),
        jax.ShapeDtypeStruct((batch, keep, qk_dim), F32),
        jax.ShapeDtypeStruct((batch, keep, qk_dim), F32),
        jax.ShapeDtypeStruct((batch, keep, d_inner), F32),
    )
    return pl.pallas_call(
        kern,
        grid=(batch, n_pg, nrb),
        in_specs=[
            pl.BlockSpec((rows, wq), row_map(0)),
            pl.BlockSpec((rows, wq), row_map(qk_dim // wq)),
            pl.BlockSpec((rows, wv), row_map(2 * qk_dim // wv)),
            pl.BlockSpec((rows, wv), row_map((2 * qk_dim + d_inner) // wv)),
            pl.BlockSpec((rows, LANES), lambda b, g, r: (b * nrb + r, 0)),
            pl.BlockSpec((1, 1, n_chunks * (n_heads // 2), 2 * CHUNK), lambda b, g, r: (b, r, 0, 0)),
            pl.BlockSpec((None, CONV_WIDTH, wq), lambda b, g, r: (layer, 0, g)),
            pl.BlockSpec((None, CONV_WIDTH, wq), lambda b, g, r: (layer, 0, qk_dim // wq + g)),
            pl.BlockSpec((None, CONV_WIDTH, wv), lambda b, g, r: (layer, 0, 2 * qk_dim // wv + g)),
            pl.BlockSpec((None, 1, HEAD_DIM), lambda b, g, r: (layer, 0, 0)),
        ],
        out_specs=(
            pl.BlockSpec((rows, wv), lambda b, g, r: (b * nrb + r, g)),
            pl.BlockSpec((1, 2 * n_pairs, HEAD_DIM, HEAD_DIM), lambda b, g, r: (b, g, 0, 0)),
            pl.BlockSpec((1, keep, wq), lambda b, g, r: (b, 0, g)),
            pl.BlockSpec((1, keep, wq), lambda b, g, r: (b, 0, g)),
            pl.BlockSpec((1, keep, wv), lambda b, g, r: (b, 0, g)),
        ),
        out_shape=out_shapes,
        scratch_shapes=[
            pltpu.VMEM((SUBLANES + rows, wq), F32),
            pltpu.VMEM((SUBLANES + rows, wq), F32),
            pltpu.VMEM((SUBLANES + rows, wv), F32),
            pltpu.VMEM((rows, wq), BF16),
            pltpu.VMEM((rows, wq), BF16),
            pltpu.VMEM((rows, wv), F32),
            pltpu.VMEM((rows, LANES), F32),
            pltpu.VMEM((n_chunks, n_pairs // 2, CHUNK, QUAD * CHUNK), BF16),
            pltpu.VMEM((n_chunks, n_pairs // 2, CHUNK, QUAD * CHUNK), BF16),
        ],
        compiler_params=_cparams("arbitrary", "arbitrary", "arbitrary"),
        name="delta_prompt",
    )(proj, proj, proj, proj, gates, gates_rows, conv_w, conv_w, conv_w, o_norm_w)


N_SAMPLE_INPUTS = 14


def _delta_sample_kernel(*refs, n_pairs, n_seqs, dec_seq, n_heads, aliased, layer):
    (q_ref, k_ref, v_ref, z_ref, cq_ref, ck_ref, cv_ref, gc_ref, gr_ref, wq_ref, wk_ref, wv_ref,
     onw_ref, s_in_ref) = refs[:N_SAMPLE_INPUTS]
    refs = refs[N_SAMPLE_INPUTS + 1:] if aliased else refs[N_SAMPLE_INPUTS:]
    y_ref, s_all_ref, ccq_ref, cck_ref, ccv_ref, eq_ref, ek_ref, ev_ref = refs
    if aliased:
        s_out_ref = s_all_ref
    else:
        s_out_ref = s_all_ref.at[layer]
        for other in range(s_all_ref.shape[0]):
            if other != layer:
                s_all_ref[other] = jnp.zeros(s_all_ref.shape[1:], F32)
    c = n_seqs * dec_seq
    pg = pl.program_id(1)
    hist = SUBLANES

    def conv(x_ref, cache_ref, ext_ref, w_ref, cc_ref):
        cols = ext_ref.shape[2]
        keep = CONV_WIDTH - 1
        ext_ref[:, 0:hist, :] = jnp.zeros((n_seqs, hist, cols), F32)
        for t in range(keep):
            ext_ref[:, hist - keep + t, :] = cache_ref[t]
        ext_ref[:, hist:hist + dec_seq, :] = x_ref[...].reshape(n_seqs, dec_seq, cols)
        for t in range(keep):
            cc_ref[t] = ext_ref[:, hist + dec_seq - keep + t, :]
        return _conv_taps(ext_ref, w_ref, hist).reshape(c, cols)

    xq = conv(q_ref, cq_ref, eq_ref, wq_ref, ccq_ref)
    xk = conv(k_ref, ck_ref, ek_ref, wk_ref, cck_ref)
    xv = conv(v_ref, cv_ref, ev_ref, wv_ref, ccv_ref)

    shift = lax.rem(LANES - 2 * n_pairs * pg, LANES)
    g_all = pltpu.roll(gc_ref[...], shift, 1)
    lane_wide = lax.broadcasted_iota(jnp.int32, (1, 2 * HEAD_DIM), 1)
    row_seq = lax.broadcasted_iota(jnp.int32, (c, 2 * HEAD_DIM), 0) // dec_seq
    pairs = range(n_pairs)
    seqs = range(n_seqs)
    hd = lambda p: slice(p * HEAD_DIM, (p + 1) * HEAD_DIM)

    qns = [_l2norm(xq[:, hd(p)]) * (HEAD_DIM ** -0.5) for p in pairs]
    kns = [_l2norm(xk[:, hd(p)]) for p in pairs]
    knb = [k.astype(BF16) for k in kns]

    groups, incl, strict = _quad_masks(c, dec_seq)
    qnb = [q.astype(BF16) for q in qns]
    grams = [_gram_quad(knb[2 * qd], qnb[2 * qd], knb[2 * qd + 1], qnb[2 * qd + 1])
             for qd in range(n_pairs // 2)]
    gcrs = [gr_ref[0, pl.ds(n_pairs * pg + p, 1), :] for p in pairs]
    l_cats, pis = [], []
    for qd in range(n_pairs // 2):
        ab = grams[qd]
        gcr = jnp.concatenate([gcrs[2 * qd], gcrs[2 * qd + 1]], axis=1)
        beta = _quad_select(g_all, QUAD * qd, groups)
        gc = _quad_select(g_all, n_heads + QUAD * qd, groups)
        dec = jnp.exp(jnp.where(incl, gc - gcr, NEG_BIG))
        pis.append((ab[c:] * dec).astype(BF16))
        l_cats.append(jnp.where(strict, beta * ab[:c] * dec, 0.0))
    pms = [pm.astype(BF16) for pm in _unit_lower_inverses_minus_identity(l_cats, dec_seq, groups)]
    half = lambda mats, p: mats[p // 2][:, (p % 2) * 2 * c:(p % 2 + 1) * 2 * c]

    s_cat = lambda p, s: jnp.concatenate([s_in_ref[s, 2 * p], s_in_ref[s, 2 * p + 1]], axis=1)
    kqs = [[jnp.dot(jnp.concatenate([kns[p][s * dec_seq:(s + 1) * dec_seq],
                                     qns[p][s * dec_seq:(s + 1) * dec_seq]], axis=0).astype(BF16),
                    s_cat(p, s).astype(BF16), preferred_element_type=F32) for s in seqs] for p in pairs]
    k_s = [jnp.concatenate([kqs[p][s][:dec_seq] for s in seqs], axis=0) for p in pairs]
    q_s = [jnp.concatenate([kqs[p][s][dec_seq:] for s in seqs], axis=0) for p in pairs]

    def last_gc(p, s):
        last = (s + 1) * dec_seq - 1
        return jnp.where(lane_wide < HEAD_DIM, gcrs[p][:, last:last + 1], gcrs[p][:, c + last:c + last + 1])

    gls = [[last_gc(p, s) for s in seqs] for p in pairs]
    beta_w = [_head_cols(g_all, 2 * p, HEAD_DIM) for p in pairs]
    gc_w = [_head_cols(g_all, n_heads + 2 * p, HEAD_DIM) for p in pairs]
    egc_w = [jnp.exp(g) for g in gc_w]
    rs = [beta_w[p] * (xv[:, 2 * p * HEAD_DIM:(2 * p + 2) * HEAD_DIM] - egc_w[p] * k_s[p]) for p in pairs]
    us = [rs[p] + jnp.dot(half(pms, p), _blockdiag_wide(rs[p]), preferred_element_type=F32) for p in pairs]
    os_ = [egc_w[p] * q_s[p] + jnp.dot(half(pis, p), _blockdiag_wide(us[p]), preferred_element_type=F32)
           for p in pairs]
    for p in pairs:
        gl_rows = jnp.concatenate([jnp.broadcast_to(g, (dec_seq, 2 * HEAD_DIM)) for g in gls[p]], axis=0)
        w = jnp.exp(gl_rows - gc_w[p]) * us[p]
        for s in seqs:
            w_s = jnp.where(row_seq == s, w, 0.0).astype(BF16)
            s_new = s_cat(p, s) * jnp.exp(gls[p][s]) + lax.dot_general(
                knb[p], w_s, (((0,), (0,)), ((), ())), preferred_element_type=F32)
            s_out_ref[s, 2 * p] = s_new[:, :HEAD_DIM]
            s_out_ref[s, 2 * p + 1] = s_new[:, HEAD_DIM:]
    for p in pairs:
        for e in range(2):
            hs = slice((2 * p + e) * HEAD_DIM, (2 * p + e + 1) * HEAD_DIM)
            y_ref[:, hs] = _gated_out(os_[p][:, e * HEAD_DIM:(e + 1) * HEAD_DIM], z_ref[:, hs], onw_ref[...])


def _delta_sample(proj, gates, gates_rows, conv_cache, state, prev_state_out, conv_w, o_norm_w, layer,
                  row_off, batch, dec_seq, n_heads, d_inner, n_pairs=4, n_seqs=8):
    qk_dim = n_heads // 2 * HEAD_DIM
    c = n_seqs * dec_seq
    wq, wv = n_pairs * HEAD_DIM, 2 * n_pairs * HEAD_DIM
    n_pg = n_heads // (2 * n_pairs)
    rb0 = row_off // c
    aliased = prev_state_out is not None
    kern = functools.partial(_delta_sample_kernel, n_pairs=n_pairs, n_seqs=n_seqs, dec_seq=dec_seq,
                             n_heads=n_heads, aliased=aliased, layer=layer)
    row_map = lambda col0: (lambda i, g: (rb0 + i, col0 + g))
    keep = CONV_WIDTH - 1
    in_specs = [
        pl.BlockSpec((c, wq), row_map(0)),
        pl.BlockSpec((c, wq), row_map(qk_dim // wq)),
        pl.BlockSpec((c, wv), row_map(2 * qk_dim // wv)),
        pl.BlockSpec((c, wv), row_map((2 * qk_dim + d_inner) // wv)),
        pl.BlockSpec((None, keep, n_seqs, wq), lambda i, g: (layer, 0, i, g)),
        pl.BlockSpec((None, keep, n_seqs, wq), lambda i, g: (layer, 0, i, qk_dim // wq + g)),
        pl.BlockSpec((None, keep, n_seqs, wv), lambda i, g: (layer, 0, i, 2 * qk_dim // wv + g)),
        pl.BlockSpec((c, LANES), lambda i, g: (rb0 + i, 0)),
        pl.BlockSpec((1, n_heads // 2, 2 * c), lambda i, g: (i, 0, 0)),
        pl.BlockSpec((None, CONV_WIDTH, wq), lambda i, g: (layer, 0, g)),
        pl.BlockSpec((None, CONV_WIDTH, wq), lambda i, g: (layer, 0, qk_dim // wq + g)),
        pl.BlockSpec((None, CONV_WIDTH, wv), lambda i, g: (layer, 0, 2 * qk_dim // wv + g)),
        pl.BlockSpec((None, 1, HEAD_DIM), lambda i, g: (layer, 0, 0)),
        pl.BlockSpec((None, n_seqs, 2 * n_pairs, HEAD_DIM, HEAD_DIM), lambda i, g: (layer, i, g, 0, 0)),
    ]
    args = [proj, proj, proj, proj, conv_cache, conv_cache, conv_cache, gates, gates_rows,
            conv_w, conv_w, conv_w, o_norm_w, state]
    aliases = {}
    if aliased:
        in_specs.append(pl.BlockSpec(memory_space=pl.ANY))
        args.append(prev_state_out)
        aliases = {N_SAMPLE_INPUTS: 1}
    state_blk = (n_seqs, 2 * n_pairs, HEAD_DIM, HEAD_DIM)
    if aliased:
        state_spec = pl.BlockSpec((None,) + state_blk, lambda i, g: (layer, i, g, 0, 0))
    else:
        state_spec = pl.BlockSpec((state.shape[0],) + state_blk, lambda i, g: (0, i, g, 0, 0))
    out_shapes = (
        jax.ShapeDtypeStruct((batch * dec_seq, d_inner), BF16),
        jax.ShapeDtypeStruct(state.shape, F32),
        jax.ShapeDtypeStruct((keep, batch, qk_dim), F32),
        jax.ShapeDtypeStruct((keep, batch, qk_dim), F32),
        jax.ShapeDtypeStruct((keep, batch, d_inner), F32),
    )
    return pl.pallas_call(
        kern,
        grid=(batch // n_seqs, n_pg),
        in_specs=in_specs,
        out_specs=(
            pl.BlockSpec((c, wv), lambda i, g: (i, g)),
            state_spec,
            pl.BlockSpec((keep, n_seqs, wq), lambda i, g: (0, i, g)),
            pl.BlockSpec((keep, n_seqs, wq), lambda i, g: (0, i, g)),
            pl.BlockSpec((keep, n_seqs, wv), lambda i, g: (0, i, g)),
        ),
        out_shape=out_shapes,
        scratch_shapes=[
            pltpu.VMEM((n_seqs, SUBLANES + dec_seq, wq), F32),
            pltpu.VMEM((n_seqs, SUBLANES + dec_seq, wq), F32),
            pltpu.VMEM((n_seqs, SUBLANES + dec_seq, wv), F32),
        ],
        input_output_aliases=aliases,
        compiler_params=_cparams("arbitrary", "arbitrary"),
        name="delta_sample",
    )(*args)


def _window_sums(x, win, axis):
    s = x
    w = 1
    while w < win:
        s = s + pltpu.roll(s, w, axis)
        w *= 2
    return s


def _pool_finish(pooled, z, wg_ref, sc_ref):
    mixed = jnp.dot(pooled.astype(BF16), wg_ref[...], preferred_element_type=F32)
    return (mixed * sc_ref[...] * _silu(z)).astype(BF16)


def _pool_prompt_kernel(u_ref, prev_ref, z_ref, wg_ref, sc_ref, y_ref, cache_ref, ext_ref, *, tt, nt):
    g = pl.program_id(0)
    t = pl.program_id(2)
    hist = POOL_BUF + 1

    @pl.when(t == 0)
    def _():
        ext_ref[0:hist, :] = jnp.zeros((hist, ext_ref.shape[1]), F32)

    @pl.when(t > 0)
    def _():
        ext_ref[0:hist, :] = prev_ref[...]

    ext_ref[hist:hist + tt, :] = u_ref[...]
    pos = t * tt + lax.broadcasted_iota(jnp.int32, (tt, 1), 0)
    for gi, win in enumerate(POOL_WINDOWS):
        @pl.when(g == gi)
        def _(win=win):
            acc = _window_sums(ext_ref[...], win, 0)[hist:]
            cnt = jnp.minimum(pos + 1, win).astype(F32)
            y_ref[...] = _pool_finish(acc / cnt - u_ref[...], z_ref[...], wg_ref, sc_ref)

    @pl.when(t == nt - 1)
    def _():
        cache_ref[0] = ext_ref[hist + tt - POOL_BUF:hist + tt, :]


def _pool_prompt(proj, w_group, scale, layer, batch, seq, d_inner, tt=512):
    gd = d_inner // len(POOL_WINDOWS)
    nt = seq // tt
    hist = POOL_BUF + 1
    kern = functools.partial(_pool_prompt_kernel, tt=tt, nt=nt)
    n_g = len(POOL_WINDOWS)
    return pl.pallas_call(
        kern,
        grid=(n_g, batch, nt),
        in_specs=[
            pl.BlockSpec((tt, gd), lambda g, b, t: (b * nt + t, g)),
            pl.BlockSpec((hist, gd), lambda g, b, t: (jnp.maximum((b * nt + t) * (tt // hist) - 1, 0), g)),
            pl.BlockSpec((tt, gd), lambda g, b, t: (b * nt + t, n_g + g)),
            pl.BlockSpec((None, None, gd, gd), lambda g, b, t: (layer, g, 0, 0)),
            pl.BlockSpec((None, 1, gd), lambda g, b, t: (layer, 0, g)),
        ],
        out_specs=(
            pl.BlockSpec((tt, gd), lambda g, b, t: (b * nt + t, g)),
            pl.BlockSpec((1, POOL_BUF, gd), lambda g, b, t: (b, 0, g)),
        ),
        out_shape=(
            jax.ShapeDtypeStruct((batch * seq, d_inner), BF16),
            jax.ShapeDtypeStruct((batch, POOL_BUF, d_inner), F32),
        ),
        scratch_shapes=[pltpu.VMEM((hist + tt, gd), F32)],
        compiler_params=_cparams("arbitrary", "arbitrary", "arbitrary"),
        name="pool_prompt",
    )(proj, proj, proj, w_group, scale)


def _pool_sample_kernel(u_ref, c_ref, z_ref, wg_ref, sc_ref, y_ref, cache_ref, ext_ref, *, sb, dec_seq):
    g = pl.program_id(0)
    hist = POOL_BUF + 1
    gd = ext_ref.shape[2]
    ext_ref[:, 0:SUBLANES, :] = jnp.zeros((sb, SUBLANES, gd), F32)
    for t in range(POOL_BUF):
        ext_ref[:, hist - POOL_BUF + t, :] = c_ref[t]
    ext_ref[:, hist:hist + dec_seq, :] = u_ref[...].reshape(sb, dec_seq, gd)
    pos = PAST_LEN + lax.broadcasted_iota(jnp.int32, (1, dec_seq, 1), 1)
    for gi, win in enumerate(POOL_WINDOWS):
        @pl.when(g == gi)
        def _(win=win):
            x = ext_ref[:, hist:hist + dec_seq, :]
            acc = _window_sums(ext_ref[...], win, 1)[:, hist:, :]
            cnt = jnp.minimum(pos + 1, win).astype(F32)
            pooled = (acc / cnt - x).reshape(sb * dec_seq, gd)
            y_ref[...] = _pool_finish(pooled, z_ref[...], wg_ref, sc_ref)

    for t in range(POOL_BUF):
        cache_ref[t] = ext_ref[:, hist + dec_seq - POOL_BUF + t, :]


def _pool_sample(proj, cache, w_group, scale, layer, row_off, batch, dec_seq, d_inner, sb=16):
    gd = d_inner // len(POOL_WINDOWS)
    n_g = len(POOL_WINDOWS)
    rows = sb * dec_seq
    rb0 = row_off // rows
    hist = POOL_BUF + 1
    kern = functools.partial(_pool_sample_kernel, sb=sb, dec_seq=dec_seq)
    return pl.pallas_call(
        kern,
        grid=(n_g, batch // sb),
        in_specs=[
            pl.BlockSpec((rows, gd), lambda g, i: (rb0 + i, g)),
            pl.BlockSpec((None, POOL_BUF, sb, gd), lambda g, i: (layer, 0, i, g)),
            pl.BlockSpec((rows, gd), lambda g, i: (rb0 + i, n_g + g)),
            pl.BlockSpec((None, None, gd, gd), lambda g, i: (layer, g, 0, 0)),
            pl.BlockSpec((None, 1, gd), lambda g, i: (layer, 0, g)),
        ],
        out_specs=(
            pl.BlockSpec((rows, gd), lambda g, i: (i, g)),
            pl.BlockSpec((POOL_BUF, sb, gd), lambda g, i: (0, i, g)),
        ),
        out_shape=(
            jax.ShapeDtypeStruct((batch * dec_seq, d_inner), BF16),
            jax.ShapeDtypeStruct((POOL_BUF, batch, d_inner), F32),
        ),
        scratch_shapes=[pltpu.VMEM((sb, hist + dec_seq, gd), F32)],
        compiler_params=_cparams("arbitrary", "arbitrary"),
        name="pool_sample",
    )(proj, cache, proj, w_group, scale)


def kernel(x_prompt, x_sample, state_delta, cache_conv, cache_pool, norm_w, final_norm_w, delta_w_in,
           delta_conv_w, delta_a_log, delta_dt_bias, delta_o_norm_w, delta_w_out, pool_w_in, pool_w_group,
           pool_scale, pool_w_out):
    batch, seq, d_model = x_prompt.shape
    dec_batch, dec_seq, _ = x_sample.shape
    depth = norm_w.shape[0]
    n_heads = state_delta.shape[2]
    d_inner = n_heads * HEAD_DIM
    conv_dim = cache_conv.shape[3]
    qk_dim = (conv_dim - d_inner) // 2
    mp, ms = batch * seq, dec_batch * dec_seq
    m = mp + ms
    n_main = conv_dim + d_inner
    n_seqs = CHUNK // dec_seq

    h = (x_prompt.reshape(mp, d_model), x_sample.reshape(ms, d_model))

    delta_w_out_b = delta_w_out.astype(BF16)
    pool_w_out_b = pool_w_out.astype(BF16)
    pool_w_group_b = pool_w_group.astype(BF16)
    delta_w_in_t = jnp.swapaxes(delta_w_in, 1, 2)
    w_tail_t = jnp.pad(delta_w_in_t[:, n_main:, :], ((0, 0), (0, LANES - 2 * n_heads), (0, 0)))
    onw = delta_o_norm_w.reshape(-1, 1, HEAD_DIM)
    pool_scale3 = pool_scale.reshape(pool_scale.shape[0], 1, d_inner)
    cache_conv_t = jnp.swapaxes(cache_conv, 1, 2)
    cache_pool_t = jnp.swapaxes(cache_pool, 1, 2)

    sd_p, cc_p, cp_p, cc_s, cp_s = [], [], [], [], []
    sd_s = None
    for i in range(depth):
        j = i // 2
        if isinstance(h, tuple):
            xn = _rmsnorm_pair(h[0], h[1], norm_w[i], BF16)
        else:
            xn = _rmsnorm(h, norm_w[i], BF16, m)
        if i % 2 == 0:
            proj = _matmul(xn, delta_w_in_t, j, n_main, tm=3072, tn=512, w_transposed=True)
            tail = _matmul(xn, w_tail_t, j, LANES, tm=3072, tn=LANES, w_transposed=True)
            gates = _gates(tail, delta_a_log[j], delta_dt_bias[j], mp, n_heads, dec_seq)
            gcum = gates[:, n_heads:2 * n_heads]
            rows_p = gcum[:mp].reshape(batch, seq // CHUNK, CHUNK, n_heads // 2, 2)
            rows_p = rows_p.transpose(0, 1, 3, 4, 2).reshape(batch, seq // (CHUNKS_PER_STEP * CHUNK), CHUNKS_PER_STEP * (n_heads // 2), 2 * CHUNK)
            rows_s = gcum[mp:].reshape(ms // CHUNK, CHUNK, n_heads // 2, 2)
            rows_s = rows_s.transpose(0, 2, 3, 1).reshape(ms // CHUNK, n_heads // 2, 2 * CHUNK)
            y_p, sp, cq, ck, cv = _delta_prompt(proj, gates, rows_p, delta_conv_w, onw, j, batch, seq,
                                                n_heads, d_inner)
            y_s, sd_s, cqs, cks, cvs = _delta_sample(proj, gates, rows_s, cache_conv_t, state_delta, sd_s,
                                                     delta_conv_w, onw, j, mp, dec_batch, dec_seq, n_heads,
                                                     d_inner, n_seqs=n_seqs)
            sd_p.append(sp)
            cc_p.append(jnp.concatenate([cq, ck, cv], axis=-1))
            cc_s.append(jnp.concatenate([cqs, cks, cvs], axis=-1))
            w_out, lw = delta_w_out_b, j
        else:
            proj = _matmul(xn, pool_w_in, j, 2 * d_inner, tm=3072, tn=512)
            y_p, pp = _pool_prompt(proj, pool_w_group_b, pool_scale3, j, batch, seq, d_inner)
            y_s, ps = _pool_sample(proj, cache_pool_t, pool_w_group_b, pool_scale3, j, mp, dec_batch, dec_seq,
                                   d_inner)
            cp_p.append(pp)
            cp_s.append(ps)
            w_out, lw = pool_w_out_b, j
        h = _matmul_residual(y_p, y_s, w_out, lw, h)

    y_prompt = _rmsnorm(h, final_norm_w, F32, mp, 0).reshape(batch, seq, d_model)
    y_sample = _rmsnorm(h, final_norm_w, F32, ms, mp).reshape(dec_batch, dec_seq, d_model)
    return (y_prompt, y_sample, jnp.stack(sd_p), jnp.stack(cc_p), jnp.stack(cp_p),
            sd_s, jnp.swapaxes(jnp.stack(cc_s), 1, 2), jnp.swapaxes(jnp.stack(cp_s), 1, 2))
```

```python
import functools

import jax
import jax.numpy as jnp
from jax import lax
from jax.experimental import pallas as pl
from jax.experimental.pallas import tpu as pltpu

F32 = jnp.float32
BF16 = jnp.bfloat16

HEAD_DIM = 128
CONV_WIDTH = 4
CHUNK = 64
CHUNKS_PER_STEP = 4
POOL_WINDOWS = (2, 4, 8, 16)
POOL_BUF = max(POOL_WINDOWS) - 1
PAST_LEN = 16384
NORM_EPS = 1e-6
NEG_BIG = -1e30

LANES = 128
SUBLANES = 8
VMEM_LIMIT = 56 * 1024 * 1024


def _cparams(*sem):
    return pltpu.CompilerParams(dimension_semantics=sem, vmem_limit_bytes=VMEM_LIMIT)


def _silu(x):
    return x * jax.nn.sigmoid(x)


def _rmsnorm_kernel(x_ref, w_ref, o_ref):
    x = x_ref[...]
    y = x * lax.rsqrt(jnp.mean(x * x, axis=-1, keepdims=True) + NORM_EPS)
    o_ref[...] = (y * w_ref[...]).astype(o_ref.dtype)


def _rmsnorm_pair_kernel(xp_ref, xs_ref, w_ref, o_ref, *, n_prompt_blocks):
    i = pl.program_id(0)

    @pl.when(i < n_prompt_blocks)
    def _():
        _rmsnorm_kernel(xp_ref, w_ref, o_ref)

    @pl.when(i >= n_prompt_blocks)
    def _():
        _rmsnorm_kernel(xs_ref, w_ref, o_ref)


def _rmsnorm_pair(x_p, x_s, w, out_dtype, tm=512):
    d = x_p.shape[1]
    npb = x_p.shape[0] // tm
    m = x_p.shape[0] + x_s.shape[0]
    return pl.pallas_call(
        functools.partial(_rmsnorm_pair_kernel, n_prompt_blocks=npb),
        grid=(m // tm,),
        in_specs=[pl.BlockSpec((tm, d), lambda i: (jnp.minimum(i, npb - 1), 0)),
                  pl.BlockSpec((tm, d), lambda i: (jnp.maximum(i - npb, 0), 0)),
                  pl.BlockSpec((1, d), lambda i: (0, 0))],
        out_specs=pl.BlockSpec((tm, d), lambda i: (i, 0)),
        out_shape=jax.ShapeDtypeStruct((m, d), out_dtype),
        compiler_params=_cparams("arbitrary"),
        name="rmsnorm",
    )(x_p, x_s, w.reshape(1, d))


def _rmsnorm(x, w, out_dtype, rows, row_off=0, tm=512):
    d = x.shape[1]
    off = row_off // tm
    return pl.pallas_call(
        _rmsnorm_kernel,
        grid=(rows // tm,),
        in_specs=[pl.BlockSpec((tm, d), lambda i: (i + off, 0)),
                  pl.BlockSpec((1, d), lambda i: (0, 0))],
        out_specs=pl.BlockSpec((tm, d), lambda i: (i, 0)),
        out_shape=jax.ShapeDtypeStruct((rows, d), out_dtype),
        compiler_params=_cparams("arbitrary"),
        name="rmsnorm",
    )(x, w.reshape(1, d))


def _matmul_kernel(x_ref, w_ref, o_ref, *, w_transposed):
    w = w_ref[...].astype(BF16)
    contract = (((1,), (1 if w_transposed else 0,)), ((), ()))
    o_ref[...] = lax.dot_general(x_ref[...], w, contract, preferred_element_type=F32)


def _matmul(x, w, layer, n_cols, tm, tn, w_transposed=False):
    m, k = x.shape
    if w_transposed:
        w_spec = pl.BlockSpec((None, tn, k), lambda i, j: (layer, j, 0))
    else:
        w_spec = pl.BlockSpec((None, k, tn), lambda i, j: (layer, 0, j))
    return pl.pallas_call(
        functools.partial(_matmul_kernel, w_transposed=w_transposed),
        grid=(m // tm, n_cols // tn),
        in_specs=[pl.BlockSpec((tm, k), lambda i, j: (i, 0)), w_spec],
        out_specs=pl.BlockSpec((tm, tn), lambda i, j: (i, j)),
        out_shape=jax.ShapeDtypeStruct((m, n_cols), F32),
        compiler_params=_cparams("arbitrary", "arbitrary"),
        name="in_proj",
    )(x, w)


def _matmul_res_kernel(yp_ref, ys_ref, w_ref, *refs, n_prompt_blocks):
    hp_ref, hs_ref, o_ref = refs if len(refs) == 3 else (refs[0], refs[0], refs[1])
    i = pl.program_id(0)
    w = w_ref[...]

    @pl.when(i < n_prompt_blocks)
    def _():
        o_ref[...] = hp_ref[...] + jnp.dot(yp_ref[...], w, preferred_element_type=F32)

    @pl.when(i >= n_prompt_blocks)
    def _():
        o_ref[...] = hs_ref[...] + jnp.dot(ys_ref[...], w, preferred_element_type=F32)


def _matmul_residual(y_p, y_s, w, layer, h, tm=1024, tn=512):
    rows_p, k = y_p.shape
    npb = rows_p // tm
    prompt_rows = lambda i, j: (jnp.minimum(i, npb - 1), j)
    sample_rows = lambda i, j: (jnp.maximum(i - npb, 0), jnp.where(i < npb, 0, j))
    if isinstance(h, tuple):
        m, n = h[0].shape[0] + h[1].shape[0], h[0].shape[1]
        h_specs = [pl.BlockSpec((tm, tn), prompt_rows), pl.BlockSpec((tm, tn), sample_rows)]
        aliases = {}
    else:
        m, n = h.shape
        h, h_specs, aliases = (h,), [pl.BlockSpec((tm, tn), lambda i, j: (i, j))], {3: 0}
    return pl.pallas_call(
        functools.partial(_matmul_res_kernel, n_prompt_blocks=npb),
        grid=(m // tm, n // tn),
        in_specs=[pl.BlockSpec((tm, k), lambda i, j: (jnp.minimum(i, npb - 1), 0)),
                  pl.BlockSpec((tm, k), lambda i, j: (jnp.maximum(i - npb, 0), 0),
                               pipeline_mode=pl.Buffered(1)),
                  pl.BlockSpec((None, k, tn), lambda i, j: (layer, 0, j))] + h_specs,
        out_specs=pl.BlockSpec((tm, tn), lambda i, j: (i, j)),
        out_shape=jax.ShapeDtypeStruct((m, n), F32),
        input_output_aliases=aliases,
        compiler_params=_cparams("arbitrary", "arbitrary"),
        name="out_proj",
    )(y_p, y_s, w, *h)


def _gates_kernel(xn_ref, wt_ref, alog_ref, dt_ref, o_ref, *, n_prompt_blocks, n_heads, dec_seq):
    i = pl.program_id(0)
    x = lax.dot_general(xn_ref[...], wt_ref[...].astype(BF16), (((1,), (1,)), ((), ())),
                        preferred_element_type=F32)
    lane = lax.broadcasted_iota(jnp.int32, x.shape, 1)
    row = lax.broadcasted_iota(jnp.int32, x.shape, 0)
    beta = jax.nn.sigmoid(x)
    a = x + dt_ref[...]
    softplus = jnp.maximum(a, 0.0) + jnp.log(1.0 + jnp.exp(-jnp.abs(a)))
    g = -jnp.exp(alog_ref[...]) * softplus
    seg = jnp.where(i < n_prompt_blocks, CHUNK, dec_seq)
    pos = row & (seg - 1)
    cs = g
    s = 1
    while s < CHUNK:
        cs = cs + jnp.where(pos >= s, pltpu.roll(cs, s, 0), 0.0)
        s *= 2
    o_ref[...] = jnp.where(lane < n_heads, beta, jnp.where(lane < 2 * n_heads, cs, 0.0))


def _gates(xn, w_tail_t, layer, a_log, dt_bias, n_prompt_rows, n_heads, dec_seq, tm=1024):
    m, k = xn.shape
    pad = lambda v: jnp.zeros((1, LANES), F32).at[0, n_heads:2 * n_heads].set(v.astype(F32))
    kern = functools.partial(_gates_kernel, n_prompt_blocks=n_prompt_rows // tm, n_heads=n_heads,
                             dec_seq=dec_seq)
    return pl.pallas_call(
        kern,
        grid=(m // tm,),
        in_specs=[pl.BlockSpec((tm, k), lambda i: (i, 0)),
                  pl.BlockSpec((None, LANES, k), lambda i: (layer, 0, 0)),
                  pl.BlockSpec((1, LANES), lambda i: (0, 0)),
                  pl.BlockSpec((1, LANES), lambda i: (0, 0))],
        out_specs=pl.BlockSpec((tm, LANES), lambda i: (i, 0)),
        out_shape=jax.ShapeDtypeStruct((m, LANES), F32),
        compiler_params=_cparams("arbitrary"),
        name="delta_gates",
    )(xn, w_tail_t, pad(a_log), pad(dt_bias))


QUAD = 4


def _blockdiag(y, masks):
    yb = y.astype(BF16)
    zero = jnp.zeros_like(yb)
    return jnp.concatenate([jnp.where(m, yb, zero) for m in masks], axis=0)


def _blockdiag_wide(r):
    rb = r.astype(BF16)
    z = jnp.zeros_like(rb[:, :HEAD_DIM])
    return jnp.concatenate([jnp.concatenate([rb[:, :HEAD_DIM], z], axis=1),
                            jnp.concatenate([z, rb[:, HEAD_DIM:]], axis=1)], axis=0)


def _inverse_chain_stages(l_cats, seg, masks, out):
    c = l_cats[0].shape[0]
    dot = lambda a, b: jnp.dot(a, _blockdiag(b, masks), preferred_element_type=F32)
    ps = [-l for l in l_cats]
    lbs = [l.astype(BF16) for l in l_cats]
    qs = [dot(lb, lb) for lb in lbs]
    yield
    n_steps = seg.bit_length() - 2
    for step in range(n_steps):
        qbs = [q.astype(BF16) for q in qs]
        if step < n_steps - 1:
            pqs = [dot(jnp.concatenate([p.astype(BF16), qb], axis=0), qb) for p, qb in zip(ps, qbs)]
            ps = [p + q + pq[:c] for p, q, pq in zip(ps, qs, pqs)]
            qs = [pq[c:] for pq in pqs]
        else:
            pqs = [dot(p.astype(BF16), qb) for p, qb in zip(ps, qbs)]
            ps = [p + q + pq for p, q, pq in zip(ps, qs, pqs)]
        yield
    out.extend(ps)


def _unit_lower_inverses_minus_identity(l_cats, seg, masks):
    out = []
    for _ in _inverse_chain_stages(l_cats, seg, masks, out):
        pass
    return out


def _interleave(*stage_generators):
    live = list(stage_generators)
    while live:
        for g in list(live):
            try:
                next(g)
            except StopIteration:
                live.remove(g)


def _gram_quad(kn_a, qn_a, kn_b, qn_b):
    zero = jnp.zeros_like(kn_a)
    lhs = jnp.concatenate([jnp.concatenate([kn_a, kn_b], axis=1),
                           jnp.concatenate([qn_a, qn_b], axis=1)], axis=0)
    ka = jnp.concatenate([kn_a, zero], axis=1)
    kb = jnp.concatenate([zero, kn_b], axis=1)
    rhs = jnp.concatenate([ka, ka, kb, kb], axis=0)
    return lax.dot_general(lhs, rhs, (((1,), (1,)), ((), ())), preferred_element_type=F32)


def _quad_masks(c, seg):
    lane = lax.broadcasted_iota(jnp.int32, (c, QUAD * c), 1)
    ii = lax.broadcasted_iota(jnp.int32, (c, QUAD * c), 0)
    jj = lane & (c - 1)
    groups = [(lane // c) == g for g in range(QUAD)]
    incl = (ii >= jj) if seg == c else ((ii >= jj) & ((ii // seg) == (jj // seg)))
    return groups, incl, ii > jj


def _quad_select(g_all, base, groups):
    out = g_all[:, base + QUAD - 1:base + QUAD]
    for g in range(QUAD - 2, -1, -1):
        out = jnp.where(groups[g], g_all[:, base + g:base + g + 1], out)
    return out


def _head_cols(g_all, idx, width):
    c = g_all.shape[0]
    return jnp.concatenate([jnp.broadcast_to(g_all[:, idx:idx + 1], (c, width)),
                            jnp.broadcast_to(g_all[:, idx + 1:idx + 2], (c, width))], axis=1)


def _gated_out(o, z, onw):
    on = o * lax.rsqrt(jnp.mean(o * o, axis=-1, keepdims=True) + NORM_EPS) * onw
    return (on * _silu(z)).astype(BF16)


def _conv_taps(ext_ref, w_ref, start):
    assert CONV_WIDTH == 4 and start >= CONV_WIDTH - 1 and start % SUBLANES == 0
    axis = len(ext_ref.shape) - 2
    x = ext_ref[...]
    x1 = pltpu.roll(x, 1, axis)
    a = w_ref[3:4, :] * x + w_ref[2:3, :] * x1
    b = w_ref[1:2, :] * x + w_ref[0:1, :] * x1
    y = a + pltpu.roll(b, 2, axis)
    y = y[start:] if axis == 0 else y[:, start:]
    return _silu(y)


def _l2norm(x):
    return x * lax.rsqrt(jnp.sum(x * x, axis=-1, keepdims=True) + NORM_EPS)


def _delta_prompt_kernel(q_ref, k_ref, v_ref, z_ref, gc_ref, gr_ref, wq_ref, wk_ref, wv_ref, onw_ref,
                         y_ref, s_ref, ccq_ref, cck_ref, ccv_ref,
                         eq_ref, ek_ref, ev_ref, qn_ref, kn_ref, vv_ref, gt_ref, p_ref, pi_ref,
                         *, n_pairs, n_chunks, n_heads, batch_chunks):
    c = CHUNK
    rows = n_chunks * c
    pg = pl.program_id(1)
    cb = pl.program_id(2)
    hist = SUBLANES

    @pl.when(cb == 0)
    def _():
        eq_ref[0:hist, :] = jnp.zeros((hist, eq_ref.shape[1]), F32)
        ek_ref[0:hist, :] = jnp.zeros((hist, ek_ref.shape[1]), F32)
        ev_ref[0:hist, :] = jnp.zeros((hist, ev_ref.shape[1]), F32)
        s_ref[...] = jnp.zeros(s_ref.shape, F32)

    eq_ref[hist:hist + rows, :] = q_ref[...]
    ek_ref[hist:hist + rows, :] = k_ref[...]
    ev_ref[hist:hist + rows, :] = v_ref[...]
    xq = _conv_taps(eq_ref, wq_ref, hist)
    xk = _conv_taps(ek_ref, wk_ref, hist)
    vv_ref[...] = _conv_taps(ev_ref, wv_ref, hist)
    for p in range(n_pairs):
        sl = slice(p * HEAD_DIM, (p + 1) * HEAD_DIM)
        qn_ref[:, sl] = (_l2norm(xq[:, sl]) * (HEAD_DIM ** -0.5)).astype(BF16)
        kn_ref[:, sl] = _l2norm(xk[:, sl]).astype(BF16)
    keep = CONV_WIDTH - 1
    ccq_ref[0] = eq_ref[hist + rows - keep:hist + rows, :]
    cck_ref[0] = ek_ref[hist + rows - keep:hist + rows, :]
    ccv_ref[0] = ev_ref[hist + rows - keep:hist + rows, :]
    eq_ref[0:hist, :] = eq_ref[rows:rows + hist, :]
    ek_ref[0:hist, :] = ek_ref[rows:rows + hist, :]
    ev_ref[0:hist, :] = ev_ref[rows:rows + hist, :]

    shift = lax.rem(LANES - 2 * n_pairs * pg, LANES)
    gt_ref[...] = pltpu.roll(gc_ref[...], shift, 1)

    n_quads = n_pairs // 2
    groups, incl, strict = _quad_masks(c, c)
    pair_row = lambda ci, p: ci * (n_heads // 2) + n_pairs * pg + p
    lane_wide = lax.broadcasted_iota(jnp.int32, (1, 2 * HEAD_DIM), 1)
    pairs = range(n_pairs)

    def matrices(chunks):
        grams = {}
        for ci in chunks:
            hd = lambda ref, p: ref[ci * c:(ci + 1) * c, p * HEAD_DIM:(p + 1) * HEAD_DIM]
            for qd in range(n_quads):
                grams[ci, qd] = _gram_quad(hd(kn_ref, 2 * qd), hd(qn_ref, 2 * qd), hd(kn_ref, 2 * qd + 1),
                                           hd(qn_ref, 2 * qd + 1))
        yield
        l_cats = []
        for ci in chunks:
            g_all = gt_ref[ci * c:(ci + 1) * c, :]
            for qd in range(n_quads):
                ab = grams[ci, qd]
                gcr = jnp.concatenate([gr_ref[0, 0, pl.ds(pair_row(ci, 2 * qd), 1), :],
                                       gr_ref[0, 0, pl.ds(pair_row(ci, 2 * qd + 1), 1), :]], axis=1)
                beta = _quad_select(g_all, QUAD * qd, groups)
                gc = _quad_select(g_all, n_heads + QUAD * qd, groups)
                dec = jnp.exp(jnp.where(incl, gc - gcr, NEG_BIG))
                pi_ref[ci, qd] = (ab[c:] * dec).astype(BF16)
                l_cats.append(jnp.where(strict, beta * ab[:c] * dec, 0.0))
        pms = []
        yield from _inverse_chain_stages(l_cats, c, groups, pms)
        for n, pm in enumerate(pms):
            p_ref[chunks[n // n_quads], n % n_quads] = pm.astype(BF16)

    def recurrence(chunks):
        for ci in chunks:
            r0 = ci * c
            g_all = gt_ref[r0:r0 + c, :]
            eg_all = jnp.exp(g_all)
            half = lambda ref, p: ref[ci, p // 2, :, (p % 2) * 2 * c:(p % 2 + 1) * 2 * c]
            kns = [kn_ref[r0:r0 + c, p * HEAD_DIM:(p + 1) * HEAD_DIM] for p in pairs]
            qns = [qn_ref[r0:r0 + c, p * HEAD_DIM:(p + 1) * HEAD_DIM] for p in pairs]
            s_cat = lambda p: jnp.concatenate([s_ref[0, 2 * p], s_ref[0, 2 * p + 1]], axis=1)
            kqs = [jnp.dot(jnp.concatenate([kns[p], qns[p]], axis=0), s_cat(p).astype(BF16),
                           preferred_element_type=F32) for p in pairs]
            yield
            beta_w = [_head_cols(g_all, 2 * p, HEAD_DIM) for p in pairs]
            gc_w = [_head_cols(g_all, n_heads + 2 * p, HEAD_DIM) for p in pairs]
            egc_w = [_head_cols(eg_all, n_heads + 2 * p, HEAD_DIM) for p in pairs]
            rs = [beta_w[p] * (vv_ref[r0:r0 + c, 2 * p * HEAD_DIM:(2 * p + 2) * HEAD_DIM]
                               - egc_w[p] * kqs[p][:c]) for p in pairs]
            us = [rs[p] + jnp.dot(half(p_ref, p), _blockdiag_wide(rs[p]), preferred_element_type=F32)
                  for p in pairs]
            yield
            for p in pairs:
                gcr = gr_ref[0, 0, pl.ds(pair_row(ci, p), 1), :]
                gl_w = jnp.where(lane_wide < HEAD_DIM, gcr[:, c - 1:c], gcr[:, 2 * c - 1:2 * c])
                w = (jnp.exp(gl_w - gc_w[p]) * us[p]).astype(BF16)
                s_new = s_cat(p) * jnp.exp(gl_w) + lax.dot_general(
                    kns[p], w, (((0,), (0,)), ((), ())), preferred_element_type=F32)
                s_ref[0, 2 * p] = s_new[:, :HEAD_DIM]
                s_ref[0, 2 * p + 1] = s_new[:, HEAD_DIM:]
            yield
            os_ = [egc_w[p] * kqs[p][c:] + jnp.dot(half(pi_ref, p), _blockdiag_wide(us[p]),
                                                   preferred_element_type=F32) for p in pairs]
            yield
            for p in pairs:
                for e in range(2):
                    hs = slice((2 * p + e) * HEAD_DIM, (2 * p + e + 1) * HEAD_DIM)
                    y_ref[r0:r0 + c, hs] = _gated_out(os_[p][:, e * HEAD_DIM:(e + 1) * HEAD_DIM],
                                                      z_ref[r0:r0 + c, hs], onw_ref[...])

    for c0 in range(0, n_chunks, batch_chunks):
        _interleave(matrices(list(range(c0, c0 + batch_chunks))))
    _interleave(recurrence(range(n_chunks)))


def _delta_prompt(proj, gates, gates_rows, conv_w, o_norm_w, layer, batch, seq, n_heads, d_inner,
                  n_pairs=8, n_chunks=CHUNKS_PER_STEP, batch_chunks=2):
    qk_dim = n_heads // 2 * HEAD_DIM
    rows = n_chunks * CHUNK
    nrb = seq // rows
    wq, wv = n_pairs * HEAD_DIM, 2 * n_pairs * HEAD_DIM
    n_pg = n_heads // (2 * n_pairs)
    kern = functools.partial(_delta_prompt_kernel, n_pairs=n_pairs, n_chunks=n_chunks, n_heads=n_heads,
                             batch_chunks=batch_chunks)
    row_map = lambda col0: (lambda b, g, r: (b * nrb + r, col0 + g))
    keep = CONV_WIDTH - 1
    out_shapes = (
        jax.ShapeDtypeStruct((batch * seq, d_inner), BF16),
        jax.ShapeDtypeStruct((batch, n_heads, HEAD_DIM, HEAD_DIM), F32),
        jax.ShapeDtypeStruct((batch, keep, qk_dim), F32),
        jax.ShapeDtypeStruct((batch, keep, qk_dim), F32),
        jax.ShapeDtypeStruct((batch, keep, d_inner), F32),
    )
    return pl.pallas_call(
        kern,
        grid=(batch, n_pg, nrb),
        in_specs=[
            pl.BlockSpec((rows, wq), row_map(0)),
            pl.BlockSpec((rows, wq), row_map(qk_dim // wq)),
            pl.BlockSpec((rows, wv), row_map(2 * qk_dim // wv)),
            pl.BlockSpec((rows, wv), row_map((2 * qk_dim + d_inner) // wv)),
            pl.BlockSpec((rows, LANES), lambda b, g, r: (b * nrb + r, 0)),
            pl.BlockSpec((1, 1, n_chunks * (n_heads // 2), 2 * CHUNK), lambda b, g, r: (b, r, 0, 0)),
            pl.BlockSpec((None, CONV_WIDTH, wq), lambda b, g, r: (layer, 0, g)),
            pl.BlockSpec((None, CONV_WIDTH, wq), lambda b, g, r: (layer, 0, qk_dim // wq + g)),
            pl.BlockSpec((None, CONV_WIDTH, wv), lambda b, g, r: (layer, 0, 2 * qk_dim // wv + g)),
            pl.BlockSpec((None, 1, HEAD_DIM), lambda b, g, r: (layer, 0, 0)),
        ],
        out_specs=(
            pl.BlockSpec((rows, wv), lambda b, g, r: (b * nrb + r, g)),
            pl.BlockSpec((1, 2 * n_pairs, HEAD_DIM, HEAD_DIM), lambda b, g, r: (b, g, 0, 0)),
            pl.BlockSpec((1, keep, wq), lambda b, g, r: (b, 0, g)),
            pl.BlockSpec((1, keep, wq), lambda b, g, r: (b, 0, g)),
            pl.BlockSpec((1, keep, wv), lambda b, g, r: (b, 0, g)),
        ),
        out_shape=out_shapes,
        scratch_shapes=[
            pltpu.VMEM((SUBLANES + rows, wq), F32),
            pltpu.VMEM((SUBLANES + rows, wq), F32),
            pltpu.VMEM((SUBLANES + rows, wv), F32),
            pltpu.VMEM((rows, wq), BF16),
            pltpu.VMEM((rows, wq), BF16),
            pltpu.VMEM((rows, wv), F32),
            pltpu.VMEM((rows, LANES), F32),
            pltpu.VMEM((n_chunks, n_pairs // 2, CHUNK, QUAD * CHUNK), BF16),
            pltpu.VMEM((n_chunks, n_pairs // 2, CHUNK, QUAD * CHUNK), BF16),
        ],
        compiler_params=_cparams("arbitrary", "arbitrary", "arbitrary"),
        name="delta_prompt",
    )(proj, proj, proj, proj, gates, gates_rows, conv_w, conv_w, conv_w, o_norm_w)


N_SAMPLE_INPUTS = 14


def _delta_sample_kernel(*refs, n_pairs, n_seqs, dec_seq, n_heads, aliased, layer):
    (q_ref, k_ref, v_ref, z_ref, cq_ref, ck_ref, cv_ref, gc_ref, gr_ref, wq_ref, wk_ref, wv_ref,
     onw_ref, s_in_ref) = refs[:N_SAMPLE_INPUTS]
    refs = refs[N_SAMPLE_INPUTS + 1:] if aliased else refs[N_SAMPLE_INPUTS:]
    y_ref, s_all_ref, ccq_ref, cck_ref, ccv_ref, eq_ref, ek_ref, ev_ref = refs
    if aliased:
        s_out_ref = s_all_ref
    else:
        s_out_ref = s_all_ref.at[layer]
        for other in range(s_all_ref.shape[0]):
            if other != layer:
                s_all_ref[other] = jnp.zeros(s_all_ref.shape[1:], F32)
    c = n_seqs * dec_seq
    pg = pl.program_id(1)
    hist = SUBLANES

    def conv(x_ref, cache_ref, ext_ref, w_ref, cc_ref):
        cols = ext_ref.shape[2]
        keep = CONV_WIDTH - 1
        ext_ref[:, 0:hist, :] = jnp.zeros((n_seqs, hist, cols), F32)
        for t in range(keep):
            ext_ref[:, hist - keep + t, :] = cache_ref[t]
        ext_ref[:, hist:hist + dec_seq, :] = x_ref[...].reshape(n_seqs, dec_seq, cols)
        for t in range(keep):
            cc_ref[t] = ext_ref[:, hist + dec_seq - keep + t, :]
        return _conv_taps(ext_ref, w_ref, hist).reshape(c, cols)

    xq = conv(q_ref, cq_ref, eq_ref, wq_ref, ccq_ref)
    xk = conv(k_ref, ck_ref, ek_ref, wk_ref, cck_ref)
    xv = conv(v_ref, cv_ref, ev_ref, wv_ref, ccv_ref)

    shift = lax.rem(LANES - 2 * n_pairs * pg, LANES)
    g_all = pltpu.roll(gc_ref[...], shift, 1)
    lane_wide = lax.broadcasted_iota(jnp.int32, (1, 2 * HEAD_DIM), 1)
    row_seq = lax.broadcasted_iota(jnp.int32, (c, 2 * HEAD_DIM), 0) // dec_seq
    pairs = range(n_pairs)
    seqs = range(n_seqs)
    hd = lambda p: slice(p * HEAD_DIM, (p + 1) * HEAD_DIM)

    qns = [_l2norm(xq[:, hd(p)]) * (HEAD_DIM ** -0.5) for p in pairs]
    kns = [_l2norm(xk[:, hd(p)]) for p in pairs]
    knb = [k.astype(BF16) for k in kns]

    groups, incl, strict = _quad_masks(c, dec_seq)
    qnb = [q.astype(BF16) for q in qns]
    grams = [_gram_quad(knb[2 * qd], qnb[2 * qd], knb[2 * qd + 1], qnb[2 * qd + 1])
             for qd in range(n_pairs // 2)]
    gcrs = [gr_ref[0, pl.ds(n_pairs * pg + p, 1), :] for p in pairs]
    l_cats, pis = [], []
    for qd in range(n_pairs // 2):
        ab = grams[qd]
        gcr = jnp.concatenate([gcrs[2 * qd], gcrs[2 * qd + 1]], axis=1)
        beta = _quad_select(g_all, QUAD * qd, groups)
        gc = _quad_select(g_all, n_heads + QUAD * qd, groups)
        dec = jnp.exp(jnp.where(incl, gc - gcr, NEG_BIG))
        pis.append((ab[c:] * dec).astype(BF16))
        l_cats.append(jnp.where(strict, beta * ab[:c] * dec, 0.0))
    pms = [pm.astype(BF16) for pm in _unit_lower_inverses_minus_identity(l_cats, dec_seq, groups)]
    half = lambda mats, p: mats[p // 2][:, (p % 2) * 2 * c:(p % 2 + 1) * 2 * c]

    s_cat = lambda p, s: jnp.concatenate([s_in_ref[s, 2 * p], s_in_ref[s, 2 * p + 1]], axis=1)
    kqs = [[jnp.dot(jnp.concatenate([kns[p][s * dec_seq:(s + 1) * dec_seq],
                                     qns[p][s * dec_seq:(s + 1) * dec_seq]], axis=0).astype(BF16),
                    s_cat(p, s).astype(BF16), preferred_element_type=F32) for s in seqs] for p in pairs]
    k_s = [jnp.concatenate([kqs[p][s][:dec_seq] for s in seqs], axis=0) for p in pairs]
    q_s = [jnp.concatenate([kqs[p][s][dec_seq:] for s in seqs], axis=0) for p in pairs]

    def last_gc(p, s):
        last = (s + 1) * dec_seq - 1
        return jnp.where(lane_wide < HEAD_DIM, gcrs[p][:, last:last + 1], gcrs[p][:, c + last:c + last + 1])

    gls = [[last_gc(p, s) for s in seqs] for p in pairs]
    beta_w = [_head_cols(g_all, 2 * p, HEAD_DIM) for p in pairs]
    gc_w = [_head_cols(g_all, n_heads + 2 * p, HEAD_DIM) for p in pairs]
    egc_w = [jnp.exp(g) for g in gc_w]
    rs = [beta_w[p] * (xv[:, 2 * p * HEAD_DIM:(2 * p + 2) * HEAD_DIM] - egc_w[p] * k_s[p]) for p in pairs]
    us = [rs[p] + jnp.dot(half(pms, p), _blockdiag_wide(rs[p]), preferred_element_type=F32) for p in pairs]
    os_ = [egc_w[p] * q_s[p] + jnp.dot(half(pis, p), _blockdiag_wide(us[p]), preferred_element_type=F32)
           for p in pairs]
    for p in pairs:
        gl_rows = jnp.concatenate([jnp.broadcast_to(g, (dec_seq, 2 * HEAD_DIM)) for g in gls[p]], axis=0)
        w = jnp.exp(gl_rows - gc_w[p]) * us[p]
        for s in seqs:
            w_s = jnp.where(row_seq == s, w, 0.0).astype(BF16)
            s_new = s_cat(p, s) * jnp.exp(gls[p][s]) + lax.dot_general(
                knb[p], w_s, (((0,), (0,)), ((), ())), preferred_element_type=F32)
            s_out_ref[s, 2 * p] = s_new[:, :HEAD_DIM]
            s_out_ref[s, 2 * p + 1] = s_new[:, HEAD_DIM:]
    for p in pairs:
        for e in range(2):
            hs = slice((2 * p + e) * HEAD_DIM, (2 * p + e + 1) * HEAD_DIM)
            y_ref[:, hs] = _gated_out(os_[p][:, e * HEAD_DIM:(e + 1) * HEAD_DIM], z_ref[:, hs], onw_ref[...])


def _delta_sample(proj, gates, gates_rows, conv_cache, state, prev_state_out, conv_w, o_norm_w, layer,
                  row_off, batch, dec_seq, n_heads, d_inner, n_pairs=4, n_seqs=8):
    qk_dim = n_heads // 2 * HEAD_DIM
    c = n_seqs * dec_seq
    wq, wv = n_pairs * HEAD_DIM, 2 * n_pairs * HEAD_DIM
    n_pg = n_heads // (2 * n_pairs)
    rb0 = row_off // c
    aliased = prev_state_out is not None
    kern = functools.partial(_delta_sample_kernel, n_pairs=n_pairs, n_seqs=n_seqs, dec_seq=dec_seq,
                             n_heads=n_heads, aliased=aliased, layer=layer)
    row_map = lambda col0: (lambda i, g: (rb0 + i, col0 + g))
    keep = CONV_WIDTH - 1
    in_specs = [
        pl.BlockSpec((c, wq), row_map(0)),
        pl.BlockSpec((c, wq), row_map(qk_dim // wq)),
        pl.BlockSpec((c, wv), row_map(2 * qk_dim // wv)),
        pl.BlockSpec((c, wv), row_map((2 * qk_dim + d_inner) // wv)),
        pl.BlockSpec((None, keep, n_seqs, wq), lambda i, g: (layer, 0, i, g)),
        pl.BlockSpec((None, keep, n_seqs, wq), lambda i, g: (layer, 0, i, qk_dim // wq + g)),
        pl.BlockSpec((None, keep, n_seqs, wv), lambda i, g: (layer, 0, i, 2 * qk_dim // wv + g)),
        pl.BlockSpec((c, LANES), lambda i, g: (rb0 + i, 0)),
        pl.BlockSpec((1, n_heads // 2, 2 * c), lambda i, g: (i, 0, 0)),
        pl.BlockSpec((None, CONV_WIDTH, wq), lambda i, g: (layer, 0, g)),
        pl.BlockSpec((None, CONV_WIDTH, wq), lambda i, g: (layer, 0, qk_dim // wq + g)),
        pl.BlockSpec((None, CONV_WIDTH, wv), lambda i, g: (layer, 0, 2 * qk_dim // wv + g)),
        pl.BlockSpec((None, 1, HEAD_DIM), lambda i, g: (layer, 0, 0)),
        pl.BlockSpec((None, n_seqs, 2 * n_pairs, HEAD_DIM, HEAD_DIM), lambda i, g: (layer, i, g, 0, 0)),
    ]
    args = [proj, proj, proj, proj, conv_cache, conv_cache, conv_cache, gates, gates_rows,
            conv_w, conv_w, conv_w, o_norm_w, state]
    aliases = {}
    if aliased:
        in_specs.append(pl.BlockSpec(memory_space=pl.ANY))
        args.append(prev_state_out)
        aliases = {N_SAMPLE_INPUTS: 1}
    state_blk = (n_seqs, 2 * n_pairs, HEAD_DIM, HEAD_DIM)
    if aliased:
        state_spec = pl.BlockSpec((None,) + state_blk, lambda i, g: (layer, i, g, 0, 0))
    else:
        state_spec = pl.BlockSpec((state.shape[0],) + state_blk, lambda i, g: (0, i, g, 0, 0))
    out_shapes = (
        jax.ShapeDtypeStruct((batch * dec_seq, d_inner), BF16),
        jax.ShapeDtypeStruct(state.shape, F32),
        jax.ShapeDtypeStruct((keep, batch, qk_dim), F32),
        jax.ShapeDtypeStruct((keep, batch, qk_dim), F32),
        jax.ShapeDtypeStruct((keep, batch, d_inner), F32),
    )
    return pl.pallas_call(
        kern,
        grid=(batch // n_seqs, n_pg),
        in_specs=in_specs,
        out_specs=(
            pl.BlockSpec((c, wv), lambda i, g: (i, g)),
            state_spec,
            pl.BlockSpec((keep, n_seqs, wq), lambda i, g: (0, i, g)),
            pl.BlockSpec((keep, n_seqs, wq), lambda i, g: (0, i, g)),
            pl.BlockSpec((keep, n_seqs, wv), lambda i, g: (0, i, g)),
        ),
        out_shape=out_shapes,
        scratch_shapes=[
            pltpu.VMEM((n_seqs, SUBLANES + dec_seq, wq), F32),
            pltpu.VMEM((n_seqs, SUBLANES + dec_seq, wq), F32),
            pltpu.VMEM((n_seqs, SUBLANES + dec_seq, wv), F32),
        ],
        input_output_aliases=aliases,
        compiler_params=_cparams("arbitrary", "arbitrary"),
        name="delta_sample",
    )(*args)


def _window_sums(x, win, axis):
    s = x
    w = 1
    while w < win:
        s = s + pltpu.roll(s, w, axis)
        w *= 2
    return s


def _pool_finish(pooled, z, wg_ref, sc_ref):
    mixed = jnp.dot(pooled.astype(BF16), wg_ref[...], preferred_element_type=F32)
    return (mixed * sc_ref[...] * _silu(z)).astype(BF16)


def _pool_prompt_kernel(u_ref, prev_ref, z_ref, wg_ref, sc_ref, y_ref, cache_ref, ext_ref, *, tt, nt):
    g = pl.program_id(0)
    t = pl.program_id(2)
    hist = POOL_BUF + 1

    @pl.when(t == 0)
    def _():
        ext_ref[0:hist, :] = jnp.zeros((hist, ext_ref.shape[1]), F32)

    @pl.when(t > 0)
    def _():
        ext_ref[0:hist, :] = prev_ref[...]

    ext_ref[hist:hist + tt, :] = u_ref[...]
    pos = t * tt + lax.broadcasted_iota(jnp.int32, (tt, 1), 0)
    for gi, win in enumerate(POOL_WINDOWS):
        @pl.when(g == gi)
        def _(win=win):
            acc = _window_sums(ext_ref[...], win, 0)[hist:]
            cnt = jnp.minimum(pos + 1, win).astype(F32)
            y_ref[...] = _pool_finish(acc / cnt - u_ref[...], z_ref[...], wg_ref, sc_ref)

    @pl.when(t == nt - 1)
    def _():
        cache_ref[0] = ext_ref[hist + tt - POOL_BUF:hist + tt, :]


def _pool_prompt(proj, w_group, scale, layer, batch, seq, d_inner, tt=512):
    gd = d_inner // len(POOL_WINDOWS)
    nt = seq // tt
    hist = POOL_BUF + 1
    kern = functools.partial(_pool_prompt_kernel, tt=tt, nt=nt)
    n_g = len(POOL_WINDOWS)
    return pl.pallas_call(
        kern,
        grid=(n_g, batch, nt),
        in_specs=[
            pl.BlockSpec((tt, gd), lambda g, b, t: (b * nt + t, g)),
            pl.BlockSpec((hist, gd), lambda g, b, t: (jnp.maximum((b * nt + t) * (tt // hist) - 1, 0), g)),
            pl.BlockSpec((tt, gd), lambda g, b, t: (b * nt + t, n_g + g)),
            pl.BlockSpec((None, None, gd, gd), lambda g, b, t: (layer, g, 0, 0)),
            pl.BlockSpec((None, 1, gd), lambda g, b, t: (layer, 0, g)),
        ],
        out_specs=(
            pl.BlockSpec((tt, gd), lambda g, b, t: (b * nt + t, g)),
            pl.BlockSpec((1, POOL_BUF, gd), lambda g, b, t: (b, 0, g)),
        ),
        out_shape=(
            jax.ShapeDtypeStruct((batch * seq, d_inner), BF16),
            jax.ShapeDtypeStruct((batch, POOL_BUF, d_inner), F32),
        ),
        scratch_shapes=[pltpu.VMEM((hist + tt, gd), F32)],
        compiler_params=_cparams("arbitrary", "arbitrary", "arbitrary"),
        name="pool_prompt",
    )(proj, proj, proj, w_group, scale)


def _pool_sample_kernel(u_ref, c_ref, z_ref, wg_ref, sc_ref, y_ref, cache_ref, ext_ref, *, sb, dec_seq):
    g = pl.program_id(0)
    hist = POOL_BUF + 1
    gd = ext_ref.shape[2]
    ext_ref[:, 0:SUBLANES, :] = jnp.zeros((sb, SUBLANES, gd), F32)
    for t in range(POOL_BUF):
        ext_ref[:, hist - POOL_BUF + t, :] = c_ref[t]
    ext_ref[:, hist:hist + dec_seq, :] = u_ref[...].reshape(sb, dec_seq, gd)
    pos = PAST_LEN + lax.broadcasted_iota(jnp.int32, (1, dec_seq, 1), 1)
    for gi, win in enumerate(POOL_WINDOWS):
        @pl.when(g == gi)
        def _(win=win):
            x = ext_ref[:, hist:hist + dec_seq, :]
            acc = _window_sums(ext_ref[...], win, 1)[:, hist:, :]
            cnt = jnp.minimum(pos + 1, win).astype(F32)
            pooled = (acc / cnt - x).reshape(sb * dec_seq, gd)
            y_ref[...] = _pool_finish(pooled, z_ref[...], wg_ref, sc_ref)

    for t in range(POOL_BUF):
        cache_ref[t] = ext_ref[:, hist + dec_seq - POOL_BUF + t, :]


def _pool_sample(proj, cache, w_group, scale, layer, row_off, batch, dec_seq, d_inner, sb=16):
    gd = d_inner // len(POOL_WINDOWS)
    n_g = len(POOL_WINDOWS)
    rows = sb * dec_seq
    rb0 = row_off // rows
    hist = POOL_BUF + 1
    kern = functools.partial(_pool_sample_kernel, sb=sb, dec_seq=dec_seq)
    return pl.pallas_call(
        kern,
        grid=(n_g, batch // sb),
        in_specs=[
            pl.BlockSpec((rows, gd), lambda g, i: (rb0 + i, g)),
            pl.BlockSpec((None, POOL_BUF, sb, gd), lambda g, i: (layer, 0, i, g)),
            pl.BlockSpec((rows, gd), lambda g, i: (rb0 + i, n_g + g)),
            pl.BlockSpec((None, None, gd, gd), lambda g, i: (layer, g, 0, 0)),
            pl.BlockSpec((None, 1, gd), lambda g, i: (layer, 0, g)),
        ],
        out_specs=(
            pl.BlockSpec((rows, gd), lambda g, i: (i, g)),
            pl.BlockSpec((POOL_BUF, sb, gd), lambda g, i: (0, i, g)),
        ),
        out_shape=(
            jax.ShapeDtypeStruct((batch * dec_seq, d_inner), BF16),
            jax.ShapeDtypeStruct((POOL_BUF, batch, d_inner), F32),
        ),
        scratch_shapes=[pltpu.VMEM((sb, hist + dec_seq, gd), F32)],
        compiler_params=_cparams("arbitrary", "arbitrary"),
        name="pool_sample",
    )(proj, cache, proj, w_group, scale)


def kernel(x_prompt, x_sample, state_delta, cache_conv, cache_pool, norm_w, final_norm_w, delta_w_in,
           delta_conv_w, delta_a_log, delta_dt_bias, delta_o_norm_w, delta_w_out, pool_w_in, pool_w_group,
           pool_scale, pool_w_out):
    batch, seq, d_model = x_prompt.shape
    dec_batch, dec_seq, _ = x_sample.shape
    depth = norm_w.shape[0]
    n_heads = state_delta.shape[2]
    d_inner = n_heads * HEAD_DIM
    conv_dim = cache_conv.shape[3]
    qk_dim = (conv_dim - d_inner) // 2
    mp, ms = batch * seq, dec_batch * dec_seq
    m = mp + ms
    n_main = conv_dim + d_inner
    n_seqs = CHUNK // dec_seq

    h = (x_prompt.reshape(mp, d_model), x_sample.reshape(ms, d_model))

    delta_w_out_b = delta_w_out.astype(BF16)
    pool_w_out_b = pool_w_out.astype(BF16)
    pool_w_group_b = pool_w_group.astype(BF16)
    delta_w_in_t = jnp.swapaxes(delta_w_in, 1, 2)
    w_tail_t = jnp.pad(delta_w_in_t[:, n_main:, :], ((0, 0), (0, LANES - 2 * n_heads), (0, 0)))
    onw = delta_o_norm_w.reshape(-1, 1, HEAD_DIM)
    pool_scale3 = pool_scale.reshape(pool_scale.shape[0], 1, d_inner)
    cache_conv_t = jnp.swapaxes(cache_conv, 1, 2)
    cache_pool_t = jnp.swapaxes(cache_pool, 1, 2)

    sd_p, cc_p, cp_p, cc_s, cp_s = [], [], [], [], []
    sd_s = None
    for i in range(depth):
        j = i // 2
        if isinstance(h, tuple):
            xn = _rmsnorm_pair(h[0], h[1], norm_w[i], BF16)
        else:
            xn = _rmsnorm(h, norm_w[i], BF16, m)
        if i % 2 == 0:
            proj = _matmul(xn, delta_w_in_t, j, n_main, tm=3072, tn=512, w_transposed=True)
            gates = _gates(xn, w_tail_t, j, delta_a_log[j], delta_dt_bias[j], mp, n_heads, dec_seq)
            gcum = gates[:, n_heads:2 * n_heads]
            rows_p = gcum[:mp].reshape(batch, seq // CHUNK, CHUNK, n_heads // 2, 2)
            rows_p = rows_p.transpose(0, 1, 3, 4, 2).reshape(batch, seq // (CHUNKS_PER_STEP * CHUNK), CHUNKS_PER_STEP * (n_heads // 2), 2 * CHUNK)
            rows_s = gcum[mp:].reshape(ms // CHUNK, CHUNK, n_heads // 2, 2)
            rows_s = rows_s.transpose(0, 2, 3, 1).reshape(ms // CHUNK, n_heads // 2, 2 * CHUNK)
            y_p, sp, cq, ck, cv = _delta_prompt(proj, gates, rows_p, delta_conv_w, onw, j, batch, seq,
                                                n_heads, d_inner)
            y_s, sd_s, cqs, cks, cvs = _delta_sample(proj, gates, rows_s, cache_conv_t, state_delta, sd_s,
                                                     delta_conv_w, onw, j, mp, dec_batch, dec_seq, n_heads,
                                                     d_inner, n_seqs=n_seqs)
            sd_p.append(sp)
            cc_p.append(jnp.concatenate([cq, ck, cv], axis=-1))
            cc_s.append(jnp.concatenate([cqs, cks, cvs], axis=-1))
            w_out, lw = delta_w_out_b, j
        else:
            proj = _matmul(xn, pool_w_in, j, 2 * d_inner, tm=3072, tn=512)
            y_p, pp = _pool_prompt(proj, pool_w_group_b, pool_scale3, j, batch, seq, d_inner)
            y_s, ps = _pool_sample(proj, cache_pool_t, pool_w_group_b, pool_scale3, j, mp, dec_batch, dec_seq,
                                   d_inner)
            cp_p.append(pp)
            cp_s.append(ps)
            w_out, lw = pool_w_out_b, j
        h = _matmul_residual(y_p, y_s, w_out, lw, h)

    y_prompt = _rmsnorm(h, final_norm_w, F32, mp, 0).reshape(batch, seq, d_model)
    y_sample = _rmsnorm(h, final_norm_w, F32, ms, mp).reshape(dec_batch, dec_seq, d_model)
    return (y_prompt, y_sample, jnp.stack(sd_p), jnp.stack(cc_p), jnp.stack(cp_p),
            sd_s, jnp.swapaxes(jnp.stack(cc_s), 1, 2), jnp.swapaxes(jnp.stack(cp_s), 1, 2))
```

```python
import functools

import jax
import jax.numpy as jnp
from jax import lax
from jax.experimental import pallas as pl
from jax.experimental.pallas import tpu as pltpu

F32 = jnp.float32
BF16 = jnp.bfloat16

HEAD_DIM = 128
CONV_WIDTH = 4
CHUNK = 64
CHUNKS_PER_STEP = 4
POOL_WINDOWS = (2, 4, 8, 16)
POOL_BUF = max(POOL_WINDOWS) - 1
PAST_LEN = 16384
NORM_EPS = 1e-6
NEG_BIG = -1e30

LANES = 128
SUBLANES = 8
VMEM_LIMIT = 56 * 1024 * 1024


def _cparams(*sem):
    return pltpu.CompilerParams(dimension_semantics=sem, vmem_limit_bytes=VMEM_LIMIT)


def _silu(x):
    return x * jax.nn.sigmoid(x)


def _rmsnorm_kernel(x_ref, w_ref, o_ref):
    x = x_ref[...]
    y = x * lax.rsqrt(jnp.mean(x * x, axis=-1, keepdims=True) + NORM_EPS)
    o_ref[...] = (y * w_ref[...]).astype(o_ref.dtype)


def _rmsnorm_pair_kernel(xp_ref, xs_ref, w_ref, o_ref, *, n_prompt_blocks):
    i = pl.program_id(0)

    @pl.when(i < n_prompt_blocks)
    def _():
        _rmsnorm_kernel(xp_ref, w_ref, o_ref)

    @pl.when(i >= n_prompt_blocks)
    def _():
        _rmsnorm_kernel(xs_ref, w_ref, o_ref)


def _rmsnorm_pair(x_p, x_s, w, out_dtype, tm=512):
    d = x_p.shape[1]
    npb = x_p.shape[0] // tm
    m = x_p.shape[0] + x_s.shape[0]
    return pl.pallas_call(
        functools.partial(_rmsnorm_pair_kernel, n_prompt_blocks=npb),
        grid=(m // tm,),
        in_specs=[pl.BlockSpec((tm, d), lambda i: (jnp.minimum(i, npb - 1), 0)),
                  pl.BlockSpec((tm, d), lambda i: (jnp.maximum(i - npb, 0), 0)),
                  pl.BlockSpec((1, d), lambda i: (0, 0))],
        out_specs=pl.BlockSpec((tm, d), lambda i: (i, 0)),
        out_shape=jax.ShapeDtypeStruct((m, d), out_dtype),
        compiler_params=_cparams("arbitrary"),
        name="rmsnorm",
    )(x_p, x_s, w.reshape(1, d))


def _rmsnorm(x, w, out_dtype, rows, row_off=0, tm=512):
    d = x.shape[1]
    off = row_off // tm
    return pl.pallas_call(
        _rmsnorm_kernel,
        grid=(rows // tm,),
        in_specs=[pl.BlockSpec((tm, d), lambda i: (i + off, 0)),
                  pl.BlockSpec((1, d), lambda i: (0, 0))],
        out_specs=pl.BlockSpec((tm, d), lambda i: (i, 0)),
        out_shape=jax.ShapeDtypeStruct((rows, d), out_dtype),
        compiler_params=_cparams("arbitrary"),
        name="rmsnorm",
    )(x, w.reshape(1, d))


def _matmul_kernel(x_ref, w_ref, o_ref, *, w_transposed):
    w = w_ref[...].astype(BF16)
    contract = (((1,), (1 if w_transposed else 0,)), ((), ()))
    o_ref[...] = lax.dot_general(x_ref[...], w, contract, preferred_element_type=F32)


def _matmul(x, w, layer, n_cols, tm, tn, w_transposed=False):
    m, k = x.shape
    if w_transposed:
        w_spec = pl.BlockSpec((None, tn, k), lambda i, j: (layer, j, 0))
    else:
        w_spec = pl.BlockSpec((None, k, tn), lambda i, j: (layer, 0, j))
    return pl.pallas_call(
        functools.partial(_matmul_kernel, w_transposed=w_transposed),
        grid=(m // tm, n_cols // tn),
        in_specs=[pl.BlockSpec((tm, k), lambda i, j: (i, 0)), w_spec],
        out_specs=pl.BlockSpec((tm, tn), lambda i, j: (i, j)),
        out_shape=jax.ShapeDtypeStruct((m, n_cols), F32),
        compiler_params=_cparams("arbitrary", "arbitrary"),
        name="in_proj",
    )(x, w)


def _matmul_res_kernel(yp_ref, ys_ref, w_ref, *refs, n_prompt_blocks):
    hp_ref, hs_ref, o_ref = refs if len(refs) == 3 else (refs[0], refs[0], refs[1])
    i = pl.program_id(0)
    w = w_ref[...]

    @pl.when(i < n_prompt_blocks)
    def _():
        o_ref[...] = hp_ref[...] + jnp.dot(yp_ref[...], w, preferred_element_type=F32)

    @pl.when(i >= n_prompt_blocks)
    def _():
        o_ref[...] = hs_ref[...] + jnp.dot(ys_ref[...], w, preferred_element_type=F32)


def _matmul_residual(y_p, y_s, w, layer, h, tm=1024, tn=512):
    rows_p, k = y_p.shape
    npb = rows_p // tm
    prompt_rows = lambda i, j: (jnp.minimum(i, npb - 1), j)
    sample_rows = lambda i, j: (jnp.maximum(i - npb, 0), jnp.where(i < npb, 0, j))
    if isinstance(h, tuple):
        m, n = h[0].shape[0] + h[1].shape[0], h[0].shape[1]
        h_specs = [pl.BlockSpec((tm, tn), prompt_rows), pl.BlockSpec((tm, tn), sample_rows)]
        aliases = {}
    else:
        m, n = h.shape
        h, h_specs, aliases = (h,), [pl.BlockSpec((tm, tn), lambda i, j: (i, j))], {3: 0}
    return pl.pallas_call(
        functools.partial(_matmul_res_kernel, n_prompt_blocks=npb),
        grid=(m // tm, n // tn),
        in_specs=[pl.BlockSpec((tm, k), lambda i, j: (jnp.minimum(i, npb - 1), 0)),
                  pl.BlockSpec((tm, k), lambda i, j: (jnp.maximum(i - npb, 0), 0),
                               pipeline_mode=pl.Buffered(1)),
                  pl.BlockSpec((None, k, tn), lambda i, j: (layer, 0, j))] + h_specs,
        out_specs=pl.BlockSpec((tm, tn), lambda i, j: (i, j)),
        out_shape=jax.ShapeDtypeStruct((m, n), F32),
        input_output_aliases=aliases,
        compiler_params=_cparams("arbitrary", "arbitrary"),
        name="out_proj",
    )(y_p, y_s, w, *h)


def _gates_kernel(xn_ref, wt_ref, alog_ref, dt_ref, o_ref, *, n_prompt_blocks, n_heads, dec_seq):
    i = pl.program_id(0)
    x = lax.dot_general(xn_ref[...], wt_ref[...].astype(BF16), (((1,), (1,)), ((), ())),
                        preferred_element_type=F32)
    lane = lax.broadcasted_iota(jnp.int32, x.shape, 1)
    row = lax.broadcasted_iota(jnp.int32, x.shape, 0)
    beta = jax.nn.sigmoid(x)
    a = x + dt_ref[...]
    softplus = jnp.maximum(a, 0.0) + jnp.log(1.0 + jnp.exp(-jnp.abs(a)))
    g = -jnp.exp(alog_ref[...]) * softplus
    seg = jnp.where(i < n_prompt_blocks, CHUNK, dec_seq)
    pos = row & (seg - 1)
    cs = g
    s = 1
    while s < CHUNK:
        cs = cs + jnp.where(pos >= s, pltpu.roll(cs, s, 0), 0.0)
        s *= 2
    o_ref[...] = jnp.where(lane < n_heads, beta, jnp.where(lane < 2 * n_heads, cs, 0.0))


def _gates(xn, w_tail_t, layer, a_log, dt_bias, n_prompt_rows, n_heads, dec_seq, tm=1024):
    m, k = xn.shape
    pad = lambda v: jnp.zeros((1, LANES), F32).at[0, n_heads:2 * n_heads].set(v.astype(F32))
    kern = functools.partial(_gates_kernel, n_prompt_blocks=n_prompt_rows // tm, n_heads=n_heads,
                             dec_seq=dec_seq)
    return pl.pallas_call(
        kern,
        grid=(m // tm,),
        in_specs=[pl.BlockSpec((tm, k), lambda i: (i, 0)),
                  pl.BlockSpec((None, LANES, k), lambda i: (layer, 0, 0)),
                  pl.BlockSpec((1, LANES), lambda i: (0, 0)),
                  pl.BlockSpec((1, LANES), lambda i: (0, 0))],
        out_specs=pl.BlockSpec((tm, LANES), lambda i: (i, 0)),
        out_shape=jax.ShapeDtypeStruct((m, LANES), F32),
        compiler_params=_cparams("arbitrary"),
        name="delta_gates",
    )(xn, w_tail_t, pad(a_log), pad(dt_bias))


QUAD = 4


def _blockdiag(y, masks):
    yb = y.astype(BF16)
    zero = jnp.zeros_like(yb)
    return jnp.concatenate([jnp.where(m, yb, zero) for m in masks], axis=0)


def _blockdiag_wide(r):
    rb = r.astype(BF16)
    z = jnp.zeros_like(rb[:, :HEAD_DIM])
    return jnp.concatenate([jnp.concatenate([rb[:, :HEAD_DIM], z], axis=1),
                            jnp.concatenate([z, rb[:, HEAD_DIM:]], axis=1)], axis=0)


def _inverse_chain_stages(l_cats, seg, masks, out):
    c = l_cats[0].shape[0]
    dot = lambda a, b: jnp.dot(a, _blockdiag(b, masks), preferred_element_type=F32)
    ps = [-l for l in l_cats]
    lbs = [l.astype(BF16) for l in l_cats]
    qs = [dot(lb, lb) for lb in lbs]
    yield
    n_steps = seg.bit_length() - 2
    for step in range(n_steps):
        qbs = [q.astype(BF16) for q in qs]
        if step < n_steps - 1:
            pqs = [dot(jnp.concatenate([p.astype(BF16), qb], axis=0), qb) for p, qb in zip(ps, qbs)]
            ps = [p + q + pq[:c] for p, q, pq in zip(ps, qs, pqs)]
            qs = [pq[c:] for pq in pqs]
        else:
            pqs = [dot(p.astype(BF16), qb) for p, qb in zip(ps, qbs)]
            ps = [p + q + pq for p, q, pq in zip(ps, qs, pqs)]
        yield
    out.extend(ps)


def _unit_lower_inverses_minus_identity(l_cats, seg, masks):
    out = []
    for _ in _inverse_chain_stages(l_cats, seg, masks, out):
        pass
    return out


def _interleave(*stage_generators):
    live = list(stage_generators)
    while live:
        for g in list(live):
            try:
                next(g)
            except StopIteration:
                live.remove(g)


def _gram_quad(kn_a, qn_a, kn_b, qn_b):
    zero = jnp.zeros_like(kn_a)
    lhs = jnp.concatenate([jnp.concatenate([kn_a, kn_b], axis=1),
                           jnp.concatenate([qn_a, qn_b], axis=1)], axis=0)
    ka = jnp.concatenate([kn_a, zero], axis=1)
    kb = jnp.concatenate([zero, kn_b], axis=1)
    rhs = jnp.concatenate([ka, ka, kb, kb], axis=0)
    return lax.dot_general(lhs, rhs, (((1,), (1,)), ((), ())), preferred_element_type=F32)


def _quad_masks(c, seg):
    lane = lax.broadcasted_iota(jnp.int32, (c, QUAD * c), 1)
    ii = lax.broadcasted_iota(jnp.int32, (c, QUAD * c), 0)
    jj = lane & (c - 1)
    groups = [(lane // c) == g for g in range(QUAD)]
    incl = (ii >= jj) if seg == c else ((ii >= jj) & ((ii // seg) == (jj // seg)))
    return groups, incl, ii > jj


def _quad_select(g_all, base, groups):
    c = g_all.shape[0]
    assert QUAD == 4 and 2 * c == LANES
    low = lax.broadcasted_iota(jnp.int32, (c, LANES), 1) < c
    col = lambda g: jnp.broadcast_to(g_all[:, base + g:base + g + 1], (c, LANES))
    return jnp.concatenate([jnp.where(low, col(0), col(1)), jnp.where(low, col(2), col(3))], axis=1)


def _head_cols(g_all, idx, width):
    c = g_all.shape[0]
    return jnp.concatenate([jnp.broadcast_to(g_all[:, idx:idx + 1], (c, width)),
                            jnp.broadcast_to(g_all[:, idx + 1:idx + 2], (c, width))], axis=1)


def _gated_out(o, z, onw):
    on = o * lax.rsqrt(jnp.mean(o * o, axis=-1, keepdims=True) + NORM_EPS) * onw
    return (on * _silu(z)).astype(BF16)


def _conv_taps(ext_ref, w_ref, start):
    assert CONV_WIDTH == 4 and start >= CONV_WIDTH - 1 and start % SUBLANES == 0
    axis = len(ext_ref.shape) - 2
    x = ext_ref[...]
    x1 = pltpu.roll(x, 1, axis)
    a = w_ref[3:4, :] * x + w_ref[2:3, :] * x1
    b = w_ref[1:2, :] * x + w_ref[0:1, :] * x1
    y = a + pltpu.roll(b, 2, axis)
    y = y[start:] if axis == 0 else y[:, start:]
    return _silu(y)


def _l2norm(x):
    return x * lax.rsqrt(jnp.sum(x * x, axis=-1, keepdims=True) + NORM_EPS)


def _delta_prompt_kernel(q_ref, k_ref, v_ref, z_ref, gc_ref, gr_ref, wq_ref, wk_ref, wv_ref, onw_ref,
                         y_ref, s_ref, ccq_ref, cck_ref, ccv_ref,
                         eq_ref, ek_ref, ev_ref, qn_ref, kn_ref, vv_ref, gt_ref, p_ref, pi_ref,
                         *, n_pairs, n_chunks, n_heads, batch_chunks):
    c = CHUNK
    rows = n_chunks * c
    pg = pl.program_id(1)
    cb = pl.program_id(2)
    hist = SUBLANES

    @pl.when(cb == 0)
    def _():
        eq_ref[0:hist, :] = jnp.zeros((hist, eq_ref.shape[1]), F32)
        ek_ref[0:hist, :] = jnp.zeros((hist, ek_ref.shape[1]), F32)
        ev_ref[0:hist, :] = jnp.zeros((hist, ev_ref.shape[1]), F32)
        s_ref[...] = jnp.zeros(s_ref.shape, F32)

    eq_ref[hist:hist + rows, :] = q_ref[...]
    ek_ref[hist:hist + rows, :] = k_ref[...]
    ev_ref[hist:hist + rows, :] = v_ref[...]
    xq = _conv_taps(eq_ref, wq_ref, hist)
    xk = _conv_taps(ek_ref, wk_ref, hist)
    vv_ref[...] = _conv_taps(ev_ref, wv_ref, hist)
    for p in range(n_pairs):
        sl = slice(p * HEAD_DIM, (p + 1) * HEAD_DIM)
        qn_ref[:, sl] = (_l2norm(xq[:, sl]) * (HEAD_DIM ** -0.5)).astype(BF16)
        kn_ref[:, sl] = _l2norm(xk[:, sl]).astype(BF16)
    keep = CONV_WIDTH - 1
    ccq_ref[0] = eq_ref[hist + rows - keep:hist + rows, :]
    cck_ref[0] = ek_ref[hist + rows - keep:hist + rows, :]
    ccv_ref[0] = ev_ref[hist + rows - keep:hist + rows, :]
    eq_ref[0:hist, :] = eq_ref[rows:rows + hist, :]
    ek_ref[0:hist, :] = ek_ref[rows:rows + hist, :]
    ev_ref[0:hist, :] = ev_ref[rows:rows + hist, :]

    shift = lax.rem(LANES - 2 * n_pairs * pg, LANES)
    gt_ref[...] = pltpu.roll(gc_ref[...], shift, 1)

    n_quads = n_pairs // 2
    groups, incl, strict = _quad_masks(c, c)
    pair_row = lambda ci, p: ci * (n_heads // 2) + n_pairs * pg + p
    lane_wide = lax.broadcasted_iota(jnp.int32, (1, 2 * HEAD_DIM), 1)
    pairs = range(n_pairs)

    def matrices(chunks):
        grams = {}
        for ci in chunks:
            hd = lambda ref, p: ref[ci * c:(ci + 1) * c, p * HEAD_DIM:(p + 1) * HEAD_DIM]
            for qd in range(n_quads):
                grams[ci, qd] = _gram_quad(hd(kn_ref, 2 * qd), hd(qn_ref, 2 * qd), hd(kn_ref, 2 * qd + 1),
                                           hd(qn_ref, 2 * qd + 1))
        yield
        l_cats = []
        for ci in chunks:
            g_all = gt_ref[ci * c:(ci + 1) * c, :]
            for qd in range(n_quads):
                ab = grams[ci, qd]
                gcr = jnp.concatenate([gr_ref[0, 0, pl.ds(pair_row(ci, 2 * qd), 1), :],
                                       gr_ref[0, 0, pl.ds(pair_row(ci, 2 * qd + 1), 1), :]], axis=1)
                beta = _quad_select(g_all, QUAD * qd, groups)
                gc = _quad_select(g_all, n_heads + QUAD * qd, groups)
                dec = jnp.exp(jnp.where(incl, gc - gcr, NEG_BIG))
                pi_ref[ci, qd] = (ab[c:] * dec).astype(BF16)
                l_cats.append(jnp.where(strict, beta * ab[:c] * dec, 0.0))
        pms = []
        yield from _inverse_chain_stages(l_cats, c, groups, pms)
        for n, pm in enumerate(pms):
            p_ref[chunks[n // n_quads], n % n_quads] = pm.astype(BF16)

    def recurrence(chunks):
        for ci in chunks:
            r0 = ci * c
            g_all = gt_ref[r0:r0 + c, :]
            eg_all = jnp.exp(g_all)
            half = lambda ref, p: ref[ci, p // 2, :, (p % 2) * 2 * c:(p % 2 + 1) * 2 * c]
            kns = [kn_ref[r0:r0 + c, p * HEAD_DIM:(p + 1) * HEAD_DIM] for p in pairs]
            qns = [qn_ref[r0:r0 + c, p * HEAD_DIM:(p + 1) * HEAD_DIM] for p in pairs]
            s_cat = lambda p: jnp.concatenate([s_ref[0, 2 * p], s_ref[0, 2 * p + 1]], axis=1)
            kqs = [jnp.dot(jnp.concatenate([kns[p], qns[p]], axis=0), s_cat(p).astype(BF16),
                           preferred_element_type=F32) for p in pairs]
            yield
            beta_w = [_head_cols(g_all, 2 * p, HEAD_DIM) for p in pairs]
            gc_w = [_head_cols(g_all, n_heads + 2 * p, HEAD_DIM) for p in pairs]
            egc_w = [_head_cols(eg_all, n_heads + 2 * p, HEAD_DIM) for p in pairs]
            rs = [beta_w[p] * (vv_ref[r0:r0 + c, 2 * p * HEAD_DIM:(2 * p + 2) * HEAD_DIM]
                               - egc_w[p] * kqs[p][:c]) for p in pairs]
            us = [rs[p] + jnp.dot(half(p_ref, p), _blockdiag_wide(rs[p]), preferred_element_type=F32)
                  for p in pairs]
            yield
            for p in pairs:
                gcr = gr_ref[0, 0, pl.ds(pair_row(ci, p), 1), :]
                gl_w = jnp.where(lane_wide < HEAD_DIM, gcr[:, c - 1:c], gcr[:, 2 * c - 1:2 * c])
                w = (jnp.exp(gl_w - gc_w[p]) * us[p]).astype(BF16)
                s_new = s_cat(p) * jnp.exp(gl_w) + lax.dot_general(
                    kns[p], w, (((0,), (0,)), ((), ())), preferred_element_type=F32)
                s_ref[0, 2 * p] = s_new[:, :HEAD_DIM]
                s_ref[0, 2 * p + 1] = s_new[:, HEAD_DIM:]
            yield
            os_ = [egc_w[p] * kqs[p][c:] + jnp.dot(half(pi_ref, p), _blockdiag_wide(us[p]),
                                                   preferred_element_type=F32) for p in pairs]
            yield
            for p in pairs:
                for e in range(2):
                    hs = slice((2 * p + e) * HEAD_DIM, (2 * p + e + 1) * HEAD_DIM)
                    y_ref[r0:r0 + c, hs] = _gated_out(os_[p][:, e * HEAD_DIM:(e + 1) * HEAD_DIM],
                                                      z_ref[r0:r0 + c, hs], onw_ref[...])

    for c0 in range(0, n_chunks, batch_chunks):
        _interleave(matrices(list(range(c0, c0 + batch_chunks))))
    _interleave(recurrence(range(n_chunks)))


def _delta_prompt(proj, gates, gates_rows, conv_w, o_norm_w, layer, batch, seq, n_heads, d_inner,
                  n_pairs=8, n_chunks=CHUNKS_PER_STEP, batch_chunks=2):
    qk_dim = n_heads // 2 * HEAD_DIM
    rows = n_chunks * CHUNK
    nrb = seq // rows
    wq, wv = n_pairs * HEAD_DIM, 2 * n_pairs * HEAD_DIM
    n_pg = n_heads // (2 * n_pairs)
    kern = functools.partial(_delta_prompt_kernel, n_pairs=n_pairs, n_chunks=n_chunks, n_heads=n_heads,
                             batch_chunks=batch_chunks)
    row_map = lambda col0: (lambda b, g, r: (b * nrb + r, col0 + g))
    keep = CONV_WIDTH - 1
    out_shapes = (
        jax.ShapeDtypeStruct((batch * seq, d_inner), BF16),
        jax.ShapeDtypeStruct((batch, n_heads, HEAD_DIM, HEAD_DIM), F32),
        jax.ShapeDtypeStruct((batch, keep, qk_dim), F32),
        jax.ShapeDtypeStruct((batch, keep, qk_dim), F32),
        jax.ShapeDtypeStruct((batch, keep, d_inner), F32),
    )
    return pl.pallas_call(
        kern,
        grid=(batch, n_pg, nrb),
        in_specs=[
            pl.BlockSpec((rows, wq), row_map(0)),
            pl.BlockSpec((rows, wq), row_map(qk_dim // wq)),
            pl.BlockSpec((rows, wv), row_map(2 * qk_dim // wv)),
            pl.BlockSpec((rows, wv), row_map((2 * qk_dim + d_inner) // wv)),
            pl.BlockSpec((rows, LANES), lambda b, g, r: (b * nrb + r, 0)),
            pl.BlockSpec((1, 1, n_chunks * (n_heads // 2), 2 * CHUNK), lambda b, g, r: (b, r, 0, 0)),
            pl.BlockSpec((None, CONV_WIDTH, wq), lambda b, g, r: (layer, 0, g)),
            pl.BlockSpec((None, CONV_WIDTH, wq), lambda b, g, r: (layer, 0, qk_dim // wq + g)),
            pl.BlockSpec((None, CONV_WIDTH, wv), lambda b, g, r: (layer, 0, 2 * qk_dim // wv + g)),
            pl.BlockSpec((None, 1, HEAD_DIM), lambda b, g, r: (layer, 0, 0)),
        ],
        out_specs=(
            pl.BlockSpec((rows, wv), lambda b, g, r: (b * nrb + r, g)),
            pl.BlockSpec((1, 2 * n_pairs, HEAD_DIM, HEAD_DIM), lambda b, g, r: (b, g, 0, 0)),
            pl.BlockSpec((1, keep, wq), lambda b, g, r: (b, 0, g)),
            pl.BlockSpec((1, keep, wq), lambda b, g, r: (b, 0, g)),
            pl.BlockSpec((1, keep, wv), lambda b, g, r: (b, 0, g)),
        ),
        out_shape=out_shapes,
        scratch_shapes=[
            pltpu.VMEM((SUBLANES + rows, wq), F32),
            pltpu.VMEM((SUBLANES + rows, wq), F32),
            pltpu.VMEM((SUBLANES + rows, wv), F32),
            pltpu.VMEM((rows, wq), BF16),
            pltpu.VMEM((rows, wq), BF16),
            pltpu.VMEM((rows, wv), F32),
            pltpu.VMEM((rows, LANES), F32),
            pltpu.VMEM((n_chunks, n_pairs // 2, CHUNK, QUAD * CHUNK), BF16),
            pltpu.VMEM((n_chunks, n_pairs // 2, CHUNK, QUAD * CHUNK), BF16),
        ],
        compiler_params=_cparams("arbitrary", "arbitrary", "arbitrary"),
        name="delta_prompt",
    )(proj, proj, proj, proj, gates, gates_rows, conv_w, conv_w, conv_w, o_norm_w)


N_SAMPLE_INPUTS = 14


def _delta_sample_kernel(*refs, n_pairs, n_seqs, dec_seq, n_heads, aliased, layer):
    (q_ref, k_ref, v_ref, z_ref, cq_ref, ck_ref, cv_ref, gc_ref, gr_ref, wq_ref, wk_ref, wv_ref,
     onw_ref, s_in_ref) = refs[:N_SAMPLE_INPUTS]
    refs = refs[N_SAMPLE_INPUTS + 1:] if aliased else refs[N_SAMPLE_INPUTS:]
    y_ref, s_all_ref, ccq_ref, cck_ref, ccv_ref, eq_ref, ek_ref, ev_ref = refs
    if aliased:
        s_out_ref = s_all_ref
    else:
        s_out_ref = s_all_ref.at[layer]
        for other in range(s_all_ref.shape[0]):
            if other != layer:
                s_all_ref[other] = jnp.zeros(s_all_ref.shape[1:], F32)
    c = n_seqs * dec_seq
    pg = pl.program_id(1)
    hist = SUBLANES

    def conv(x_ref, cache_ref, ext_ref, w_ref, cc_ref):
        cols = ext_ref.shape[2]
        keep = CONV_WIDTH - 1
        ext_ref[:, 0:hist, :] = jnp.zeros((n_seqs, hist, cols), F32)
        for t in range(keep):
            ext_ref[:, hist - keep + t, :] = cache_ref[t]
        ext_ref[:, hist:hist + dec_seq, :] = x_ref[...].reshape(n_seqs, dec_seq, cols)
        for t in range(keep):
            cc_ref[t] = ext_ref[:, hist + dec_seq - keep + t, :]
        return _conv_taps(ext_ref, w_ref, hist).reshape(c, cols)

    xq = conv(q_ref, cq_ref, eq_ref, wq_ref, ccq_ref)
    xk = conv(k_ref, ck_ref, ek_ref, wk_ref, cck_ref)
    xv = conv(v_ref, cv_ref, ev_ref, wv_ref, ccv_ref)

    shift = lax.rem(LANES - 2 * n_pairs * pg, LANES)
    g_all = pltpu.roll(gc_ref[...], shift, 1)
    lane_wide = lax.broadcasted_iota(jnp.int32, (1, 2 * HEAD_DIM), 1)
    row_seq = lax.broadcasted_iota(jnp.int32, (c, 2 * HEAD_DIM), 0) // dec_seq
    pairs = range(n_pairs)
    seqs = range(n_seqs)
    hd = lambda p: slice(p * HEAD_DIM, (p + 1) * HEAD_DIM)

    qns = [_l2norm(xq[:, hd(p)]) * (HEAD_DIM ** -0.5) for p in pairs]
    kns = [_l2norm(xk[:, hd(p)]) for p in pairs]
    knb = [k.astype(BF16) for k in kns]

    groups, incl, strict = _quad_masks(c, dec_seq)
    qnb = [q.astype(BF16) for q in qns]
    grams = [_gram_quad(knb[2 * qd], qnb[2 * qd], knb[2 * qd + 1], qnb[2 * qd + 1])
             for qd in range(n_pairs // 2)]
    gcrs = [gr_ref[0, pl.ds(n_pairs * pg + p, 1), :] for p in pairs]
    l_cats, pis = [], []
    for qd in range(n_pairs // 2):
        ab = grams[qd]
        gcr = jnp.concatenate([gcrs[2 * qd], gcrs[2 * qd + 1]], axis=1)
        beta = _quad_select(g_all, QUAD * qd, groups)
        gc = _quad_select(g_all, n_heads + QUAD * qd, groups)
        dec = jnp.exp(jnp.where(incl, gc - gcr, NEG_BIG))
        pis.append((ab[c:] * dec).astype(BF16))
        l_cats.append(jnp.where(strict, beta * ab[:c] * dec, 0.0))
    pms = [pm.astype(BF16) for pm in _unit_lower_inverses_minus_identity(l_cats, dec_seq, groups)]
    half = lambda mats, p: mats[p // 2][:, (p % 2) * 2 * c:(p % 2 + 1) * 2 * c]

    s_cat = lambda p, s: jnp.concatenate([s_in_ref[s, 2 * p], s_in_ref[s, 2 * p + 1]], axis=1)
    kqs = [[jnp.dot(jnp.concatenate([kns[p][s * dec_seq:(s + 1) * dec_seq],
                                     qns[p][s * dec_seq:(s + 1) * dec_seq]], axis=0).astype(BF16),
                    s_cat(p, s).astype(BF16), preferred_element_type=F32) for s in seqs] for p in pairs]
    k_s = [jnp.concatenate([kqs[p][s][:dec_seq] for s in seqs], axis=0) for p in pairs]
    q_s = [jnp.concatenate([kqs[p][s][dec_seq:] for s in seqs], axis=0) for p in pairs]

    def last_gc(p, s):
        last = (s + 1) * dec_seq - 1
        return jnp.where(lane_wide < HEAD_DIM, gcrs[p][:, last:last + 1], gcrs[p][:, c + last:c + last + 1])

    gls = [[last_gc(p, s) for s in seqs] for p in pairs]
    beta_w = [_head_cols(g_all, 2 * p, HEAD_DIM) for p in pairs]
    gc_w = [_head_cols(g_all, n_heads + 2 * p, HEAD_DIM) for p in pairs]
    egc_w = [jnp.exp(g) for g in gc_w]
    rs = [beta_w[p] * (xv[:, 2 * p * HEAD_DIM:(2 * p + 2) * HEAD_DIM] - egc_w[p] * k_s[p]) for p in pairs]
    us = [rs[p] + jnp.dot(half(pms, p), _blockdiag_wide(rs[p]), preferred_element_type=F32) for p in pairs]
    os_ = [egc_w[p] * q_s[p] + jnp.dot(half(pis, p), _blockdiag_wide(us[p]), preferred_element_type=F32)
           for p in pairs]
    for p in pairs:
        gl_rows = jnp.concatenate([jnp.broadcast_to(g, (dec_seq, 2 * HEAD_DIM)) for g in gls[p]], axis=0)
        w = jnp.exp(gl_rows - gc_w[p]) * us[p]
        for s in seqs:
            w_s = jnp.where(row_seq == s, w, 0.0).astype(BF16)
            s_new = s_cat(p, s) * jnp.exp(gls[p][s]) + lax.dot_general(
                knb[p], w_s, (((0,), (0,)), ((), ())), preferred_element_type=F32)
            s_out_ref[s, 2 * p] = s_new[:, :HEAD_DIM]
            s_out_ref[s, 2 * p + 1] = s_new[:, HEAD_DIM:]
    for p in pairs:
        for e in range(2):
            hs = slice((2 * p + e) * HEAD_DIM, (2 * p + e + 1) * HEAD_DIM)
            y_ref[:, hs] = _gated_out(os_[p][:, e * HEAD_DIM:(e + 1) * HEAD_DIM], z_ref[:, hs], onw_ref[...])


def _delta_sample(proj, gates, gates_rows, conv_cache, state, prev_state_out, conv_w, o_norm_w, layer,
                  row_off, batch, dec_seq, n_heads, d_inner, n_seqs=8):
    n_pairs = 8 if prev_state_out is not None else 4
    qk_dim = n_heads // 2 * HEAD_DIM
    c = n_seqs * dec_seq
    wq, wv = n_pairs * HEAD_DIM, 2 * n_pairs * HEAD_DIM
    n_pg = n_heads // (2 * n_pairs)
    rb0 = row_off // c
    aliased = prev_state_out is not None
    kern = functools.partial(_delta_sample_kernel, n_pairs=n_pairs, n_seqs=n_seqs, dec_seq=dec_seq,
                             n_heads=n_heads, aliased=aliased, layer=layer)
    row_map = lambda col0: (lambda i, g: (rb0 + i, col0 + g))
    keep = CONV_WIDTH - 1
    in_specs = [
        pl.BlockSpec((c, wq), row_map(0)),
        pl.BlockSpec((c, wq), row_map(qk_dim // wq)),
        pl.BlockSpec((c, wv), row_map(2 * qk_dim // wv)),
        pl.BlockSpec((c, wv), row_map((2 * qk_dim + d_inner) // wv)),
        pl.BlockSpec((None, keep, n_seqs, wq), lambda i, g: (layer, 0, i, g)),
        pl.BlockSpec((None, keep, n_seqs, wq), lambda i, g: (layer, 0, i, qk_dim // wq + g)),
        pl.BlockSpec((None, keep, n_seqs, wv), lambda i, g: (layer, 0, i, 2 * qk_dim // wv + g)),
        pl.BlockSpec((c, LANES), lambda i, g: (rb0 + i, 0)),
        pl.BlockSpec((1, n_heads // 2, 2 * c), lambda i, g: (i, 0, 0)),
        pl.BlockSpec((None, CONV_WIDTH, wq), lambda i, g: (layer, 0, g)),
        pl.BlockSpec((None, CONV_WIDTH, wq), lambda i, g: (layer, 0, qk_dim // wq + g)),
        pl.BlockSpec((None, CONV_WIDTH, wv), lambda i, g: (layer, 0, 2 * qk_dim // wv + g)),
        pl.BlockSpec((None, 1, HEAD_DIM), lambda i, g: (layer, 0, 0)),
        pl.BlockSpec((None, n_seqs, 2 * n_pairs, HEAD_DIM, HEAD_DIM), lambda i, g: (layer, i, g, 0, 0)),
    ]
    args = [proj, proj, proj, proj, conv_cache, conv_cache, conv_cache, gates, gates_rows,
            conv_w, conv_w, conv_w, o_norm_w, state]
    aliases = {}
    if aliased:
        in_specs.append(pl.BlockSpec(memory_space=pl.ANY))
        args.append(prev_state_out)
        aliases = {N_SAMPLE_INPUTS: 1}
    state_blk = (n_seqs, 2 * n_pairs, HEAD_DIM, HEAD_DIM)
    if aliased:
        state_spec = pl.BlockSpec((None,) + state_blk, lambda i, g: (layer, i, g, 0, 0))
    else:
        state_spec = pl.BlockSpec((state.shape[0],) + state_blk, lambda i, g: (0, i, g, 0, 0))
    out_shapes = (
        jax.ShapeDtypeStruct((batch * dec_seq, d_inner), BF16),
        jax.ShapeDtypeStruct(state.shape, F32),
        jax.ShapeDtypeStruct((keep, batch, qk_dim), F32),
        jax.ShapeDtypeStruct((keep, batch, qk_dim), F32),
        jax.ShapeDtypeStruct((keep, batch, d_inner), F32),
    )
    return pl.pallas_call(
        kern,
        grid=(batch // n_seqs, n_pg),
        in_specs=in_specs,
        out_specs=(
            pl.BlockSpec((c, wv), lambda i, g: (i, g)),
            state_spec,
            pl.BlockSpec((keep, n_seqs, wq), lambda i, g: (0, i, g)),
            pl.BlockSpec((keep, n_seqs, wq), lambda i, g: (0, i, g)),
            pl.BlockSpec((keep, n_seqs, wv), lambda i, g: (0, i, g)),
        ),
        out_shape=out_shapes,
        scratch_shapes=[
            pltpu.VMEM((n_seqs, SUBLANES + dec_seq, wq), F32),
            pltpu.VMEM((n_seqs, SUBLANES + dec_seq, wq), F32),
            pltpu.VMEM((n_seqs, SUBLANES + dec_seq, wv), F32),
        ],
        input_output_aliases=aliases,
        compiler_params=_cparams("arbitrary", "arbitrary"),
        name="delta_sample",
    )(*args)


def _window_sums(x, win, axis):
    s = x
    w = 1
    while w < win:
        s = s + pltpu.roll(s, w, axis)
        w *= 2
    return s


def _pool_finish(pooled, z, wg_ref, sc_ref):
    mixed = jnp.dot(pooled.astype(BF16), wg_ref[...], preferred_element_type=F32)
    return (mixed * sc_ref[...] * _silu(z)).astype(BF16)


def _pool_prompt_kernel(u_ref, prev_ref, z_ref, wg_ref, sc_ref, y_ref, cache_ref, ext_ref, *, tt, nt):
    g = pl.program_id(0)
    t = pl.program_id(2)
    hist = POOL_BUF + 1

    @pl.when(t == 0)
    def _():
        ext_ref[0:hist, :] = jnp.zeros((hist, ext_ref.shape[1]), F32)

    @pl.when(t > 0)
    def _():
        ext_ref[0:hist, :] = prev_ref[...]

    ext_ref[hist:hist + tt, :] = u_ref[...]
    pos = t * tt + lax.broadcasted_iota(jnp.int32, (tt, 1), 0)
    for gi, win in enumerate(POOL_WINDOWS):
        @pl.when(g == gi)
        def _(win=win):
            acc = _window_sums(ext_ref[...], win, 0)[hist:]
            cnt = jnp.minimum(pos + 1, win).astype(F32)
            y_ref[...] = _pool_finish(acc / cnt - u_ref[...], z_ref[...], wg_ref, sc_ref)

    @pl.when(t == nt - 1)
    def _():
        cache_ref[0] = ext_ref[hist + tt - POOL_BUF:hist + tt, :]


def _pool_prompt(proj, w_group, scale, layer, batch, seq, d_inner, tt=512):
    gd = d_inner // len(POOL_WINDOWS)
    nt = seq // tt
    hist = POOL_BUF + 1
    kern = functools.partial(_pool_prompt_kernel, tt=tt, nt=nt)
    n_g = len(POOL_WINDOWS)
    return pl.pallas_call(
        kern,
        grid=(n_g, batch, nt),
        in_specs=[
            pl.BlockSpec((tt, gd), lambda g, b, t: (b * nt + t, g)),
            pl.BlockSpec((hist, gd), lambda g, b, t: (jnp.maximum((b * nt + t) * (tt // hist) - 1, 0), g)),
            pl.BlockSpec((tt, gd), lambda g, b, t: (b * nt + t, n_g + g)),
            pl.BlockSpec((None, None, gd, gd), lambda g, b, t: (layer, g, 0, 0)),
            pl.BlockSpec((None, 1, gd), lambda g, b, t: (layer, 0, g)),
        ],
        out_specs=(
            pl.BlockSpec((tt, gd), lambda g, b, t: (b * nt + t, g)),
            pl.BlockSpec((1, POOL_BUF, gd), lambda g, b, t: (b, 0, g)),
        ),
        out_shape=(
            jax.ShapeDtypeStruct((batch * seq, d_inner), BF16),
            jax.ShapeDtypeStruct((batch, POOL_BUF, d_inner), F32),
        ),
        scratch_shapes=[pltpu.VMEM((hist + tt, gd), F32)],
        compiler_params=_cparams("arbitrary", "arbitrary", "arbitrary"),
        name="pool_prompt",
    )(proj, proj, proj, w_group, scale)


def _pool_sample_kernel(*refs, sb, dec_seq, aliased, layer):
    u_ref, c_ref, z_ref, wg_ref, sc_ref = refs[:5]
    y_ref, cache_all_ref, ext_ref = refs[6:] if aliased else refs[5:]
    g = pl.program_id(0)
    hist = POOL_BUF + 1
    gd = ext_ref.shape[2]
    if aliased:
        cache_ref = cache_all_ref
    else:
        cache_ref = cache_all_ref.at[layer]
        for other in range(cache_all_ref.shape[0]):
            if other != layer:
                cache_all_ref[other] = jnp.zeros(cache_all_ref.shape[1:], F32)
    ext_ref[:, 0:SUBLANES, :] = jnp.zeros((sb, SUBLANES, gd), F32)
    for t in range(POOL_BUF):
        ext_ref[:, hist - POOL_BUF + t, :] = c_ref[t]
    ext_ref[:, hist:hist + dec_seq, :] = u_ref[...].reshape(sb, dec_seq, gd)
    pos = PAST_LEN + lax.broadcasted_iota(jnp.int32, (1, dec_seq, 1), 1)
    for gi, win in enumerate(POOL_WINDOWS):
        @pl.when(g == gi)
        def _(win=win):
            x = ext_ref[:, hist:hist + dec_seq, :]
            acc = _window_sums(ext_ref[...], win, 1)[:, hist:, :]
            cnt = jnp.minimum(pos + 1, win).astype(F32)
            pooled = (acc / cnt - x).reshape(sb * dec_seq, gd)
            y_ref[...] = _pool_finish(pooled, z_ref[...], wg_ref, sc_ref)

    for t in range(POOL_BUF):
        cache_ref[t] = ext_ref[:, hist + dec_seq - POOL_BUF + t, :]


def _pool_sample(proj, cache, prev_cache_out, w_group, scale, layer, row_off, batch, dec_seq, d_inner, sb=16):
    gd = d_inner // len(POOL_WINDOWS)
    n_g = len(POOL_WINDOWS)
    rows = sb * dec_seq
    rb0 = row_off // rows
    hist = POOL_BUF + 1
    aliased = prev_cache_out is not None
    kern = functools.partial(_pool_sample_kernel, sb=sb, dec_seq=dec_seq, aliased=aliased, layer=layer)
    in_specs = [
        pl.BlockSpec((rows, gd), lambda g, i: (rb0 + i, g)),
        pl.BlockSpec((None, POOL_BUF, sb, gd), lambda g, i: (layer, 0, i, g)),
        pl.BlockSpec((rows, gd), lambda g, i: (rb0 + i, n_g + g)),
        pl.BlockSpec((None, None, gd, gd), lambda g, i: (layer, g, 0, 0)),
        pl.BlockSpec((None, 1, gd), lambda g, i: (layer, 0, g)),
    ]
    args = [proj, cache, proj, w_group, scale]
    if aliased:
        in_specs.append(pl.BlockSpec(memory_space=pl.ANY))
        args.append(prev_cache_out)
        cache_spec = pl.BlockSpec((None, POOL_BUF, sb, gd), lambda g, i: (layer, 0, i, g))
    else:
        cache_spec = pl.BlockSpec((cache.shape[0], POOL_BUF, sb, gd), lambda g, i: (0, 0, i, g))
    return pl.pallas_call(
        kern,
        grid=(n_g, batch // sb),
        in_specs=in_specs,
        out_specs=(pl.BlockSpec((rows, gd), lambda g, i: (i, g)), cache_spec),
        out_shape=(
            jax.ShapeDtypeStruct((batch * dec_seq, d_inner), BF16),
            jax.ShapeDtypeStruct(cache.shape, F32),
        ),
        scratch_shapes=[pltpu.VMEM((sb, hist + dec_seq, gd), F32)],
        input_output_aliases={len(args) - 1: 1} if aliased else {},
        compiler_params=_cparams("arbitrary", "arbitrary"),
        name="pool_sample",
    )(*args)


def kernel(x_prompt, x_sample, state_delta, cache_conv, cache_pool, norm_w, final_norm_w, delta_w_in,
           delta_conv_w, delta_a_log, delta_dt_bias, delta_o_norm_w, delta_w_out, pool_w_in, pool_w_group,
           pool_scale, pool_w_out):
    batch, seq, d_model = x_prompt.shape
    dec_batch, dec_seq, _ = x_sample.shape
    depth = norm_w.shape[0]
    n_heads = state_delta.shape[2]
    d_inner = n_heads * HEAD_DIM
    conv_dim = cache_conv.shape[3]
    qk_dim = (conv_dim - d_inner) // 2
    mp, ms = batch * seq, dec_batch * dec_seq
    m = mp + ms
    n_main = conv_dim + d_inner
    n_seqs = CHUNK // dec_seq

    h = (x_prompt.reshape(mp, d_model), x_sample.reshape(ms, d_model))

    delta_w_out_b = delta_w_out.astype(BF16)
    pool_w_out_b = pool_w_out.astype(BF16)
    pool_w_group_b = pool_w_group.astype(BF16)
    delta_w_in_t = jnp.swapaxes(delta_w_in, 1, 2)
    w_tail_t = jnp.pad(delta_w_in_t[:, n_main:, :], ((0, 0), (0, LANES - 2 * n_heads), (0, 0)))
    onw = delta_o_norm_w.reshape(-1, 1, HEAD_DIM)
    pool_scale3 = pool_scale.reshape(pool_scale.shape[0], 1, d_inner)
    cache_conv_t = jnp.swapaxes(cache_conv, 1, 2)
    cache_pool_t = jnp.swapaxes(cache_pool, 1, 2)

    sd_p, cc_p, cp_p, cc_s = [], [], [], []
    sd_s = cp_s = None
    for i in range(depth):
        j = i // 2
        if isinstance(h, tuple):
            xn = _rmsnorm_pair(h[0], h[1], norm_w[i], BF16)
        else:
            xn = _rmsnorm(h, norm_w[i], BF16, m)
        if i % 2 == 0:
            proj = _matmul(xn, delta_w_in_t, j, n_main, tm=3072, tn=512, w_transposed=True)
            gates = _gates(xn, w_tail_t, j, delta_a_log[j], delta_dt_bias[j], mp, n_heads, dec_seq)
            gcum = gates[:, n_heads:2 * n_heads]
            rows_p = gcum[:mp].reshape(batch, seq // CHUNK, CHUNK, n_heads // 2, 2)
            rows_p = rows_p.transpose(0, 1, 3, 4, 2).reshape(batch, seq // (CHUNKS_PER_STEP * CHUNK), CHUNKS_PER_STEP * (n_heads // 2), 2 * CHUNK)
            rows_s = gcum[mp:].reshape(ms // CHUNK, CHUNK, n_heads // 2, 2)
            rows_s = rows_s.transpose(0, 2, 3, 1).reshape(ms // CHUNK, n_heads // 2, 2 * CHUNK)
            y_p, sp, cq, ck, cv = _delta_prompt(proj, gates, rows_p, delta_conv_w, onw, j, batch, seq,
                                                n_heads, d_inner)
            y_s, sd_s, cqs, cks, cvs = _delta_sample(proj, gates, rows_s, cache_conv_t, state_delta, sd_s,
                                                     delta_conv_w, onw, j, mp, dec_batch, dec_seq, n_heads,
                                                     d_inner, n_seqs=n_seqs)
            sd_p.append(sp)
            cc_p.append(jnp.concatenate([cq, ck, cv], axis=-1))
            cc_s.append(jnp.concatenate([cqs, cks, cvs], axis=-1))
            w_out, lw = delta_w_out_b, j
        else:
            proj = _matmul(xn, pool_w_in, j, 2 * d_inner, tm=3072, tn=512)
            y_p, pp = _pool_prompt(proj, pool_w_group_b, pool_scale3, j, batch, seq, d_inner)
            y_s, cp_s = _pool_sample(proj, cache_pool_t, cp_s, pool_w_group_b, pool_scale3, j, mp, dec_batch,
                                     dec_seq, d_inner)
            cp_p.append(pp)
            w_out, lw = pool_w_out_b, j
        h = _matmul_residual(y_p, y_s, w_out, lw, h)

    y_prompt = _rmsnorm(h, final_norm_w, F32, mp, 0).reshape(batch, seq, d_model)
    y_sample = _rmsnorm(h, final_norm_w, F32, ms, mp).reshape(dec_batch, dec_seq, d_model)
    return (y_prompt, y_sample, jnp.stack(sd_p), jnp.stack(cc_p), jnp.stack(cp_p),
            sd_s, jnp.swapaxes(jnp.stack(cc_s), 1, 2), jnp.swapaxes(cp_s, 1, 2))
```

```python
import functools

import jax
import jax.numpy as jnp
from jax import lax
from jax.experimental import pallas as pl
from jax.experimental.pallas import tpu as pltpu

F32 = jnp.float32
BF16 = jnp.bfloat16

HEAD_DIM = 128
CONV_WIDTH = 4
CHUNK = 64
CHUNKS_PER_STEP = 4
POOL_WINDOWS = (2, 4, 8, 16)
POOL_BUF = max(POOL_WINDOWS) - 1
PAST_LEN = 16384
NORM_EPS = 1e-6
NEG_BIG = -1e30

LANES = 128
SUBLANES = 8
VMEM_LIMIT = 56 * 1024 * 1024


def _cparams(*sem):
    return pltpu.CompilerParams(dimension_semantics=sem, vmem_limit_bytes=VMEM_LIMIT)


def _silu(x):
    return x * jax.nn.sigmoid(x)


def _rmsnorm_kernel(x_ref, w_ref, o_ref):
    x = x_ref[...]
    y = x * lax.rsqrt(jnp.mean(x * x, axis=-1, keepdims=True) + NORM_EPS)
    o_ref[...] = (y * w_ref[...]).astype(o_ref.dtype)


def _rmsnorm_pair_kernel(xp_ref, xs_ref, w_ref, o_ref, *, n_prompt_blocks):
    i = pl.program_id(0)

    @pl.when(i < n_prompt_blocks)
    def _():
        _rmsnorm_kernel(xp_ref, w_ref, o_ref)

    @pl.when(i >= n_prompt_blocks)
    def _():
        _rmsnorm_kernel(xs_ref, w_ref, o_ref)


def _rmsnorm_pair(x_p, x_s, w, out_dtype, tm=512):
    d = x_p.shape[1]
    npb = x_p.shape[0] // tm
    m = x_p.shape[0] + x_s.shape[0]
    return pl.pallas_call(
        functools.partial(_rmsnorm_pair_kernel, n_prompt_blocks=npb),
        grid=(m // tm,),
        in_specs=[pl.BlockSpec((tm, d), lambda i: (jnp.minimum(i, npb - 1), 0)),
                  pl.BlockSpec((tm, d), lambda i: (jnp.maximum(i - npb, 0), 0)),
                  pl.BlockSpec((1, d), lambda i: (0, 0))],
        out_specs=pl.BlockSpec((tm, d), lambda i: (i, 0)),
        out_shape=jax.ShapeDtypeStruct((m, d), out_dtype),
        compiler_params=_cparams("arbitrary"),
        name="rmsnorm",
    )(x_p, x_s, w.reshape(1, d))


def _rmsnorm(x, w, out_dtype, rows, row_off=0, tm=512):
    d = x.shape[1]
    off = row_off // tm
    return pl.pallas_call(
        _rmsnorm_kernel,
        grid=(rows // tm,),
        in_specs=[pl.BlockSpec((tm, d), lambda i: (i + off, 0)),
                  pl.BlockSpec((1, d), lambda i: (0, 0))],
        out_specs=pl.BlockSpec((tm, d), lambda i: (i, 0)),
        out_shape=jax.ShapeDtypeStruct((rows, d), out_dtype),
        compiler_params=_cparams("arbitrary"),
        name="rmsnorm",
    )(x, w.reshape(1, d))


def _matmul_kernel(x_ref, w_ref, o_ref, *, w_transposed):
    w = w_ref[...].astype(BF16)
    contract = (((1,), (1 if w_transposed else 0,)), ((), ()))
    o_ref[...] = lax.dot_general(x_ref[...], w, contract, preferred_element_type=F32)


def _matmul(x, w, layer, n_cols, tm, tn, w_transposed=False):
    m, k = x.shape
    if w_transposed:
        w_spec = pl.BlockSpec((None, tn, k), lambda i, j: (layer, j, 0))
    else:
        w_spec = pl.BlockSpec((None, k, tn), lambda i, j: (layer, 0, j))
    return pl.pallas_call(
        functools.partial(_matmul_kernel, w_transposed=w_transposed),
        grid=(m // tm, n_cols // tn),
        in_specs=[pl.BlockSpec((tm, k), lambda i, j: (i, 0)), w_spec],
        out_specs=pl.BlockSpec((tm, tn), lambda i, j: (i, j)),
        out_shape=jax.ShapeDtypeStruct((m, n_cols), F32),
        compiler_params=_cparams("arbitrary", "arbitrary"),
        name="in_proj",
    )(x, w)


def _matmul_res_kernel(yp_ref, ys_ref, w_ref, *refs, n_prompt_blocks):
    hp_ref, hs_ref, o_ref = refs if len(refs) == 3 else (refs[0], refs[0], refs[1])
    i = pl.program_id(0)
    w = w_ref[...]

    @pl.when(i < n_prompt_blocks)
    def _():
        o_ref[...] = hp_ref[...] + jnp.dot(yp_ref[...], w, preferred_element_type=F32)

    @pl.when(i >= n_prompt_blocks)
    def _():
        o_ref[...] = hs_ref[...] + jnp.dot(ys_ref[...], w, preferred_element_type=F32)


def _matmul_residual(y_p, y_s, w, layer, h, tm=1024, tn=512):
    rows_p, k = y_p.shape
    npb = rows_p // tm
    prompt_rows = lambda i, j: (jnp.minimum(i, npb - 1), j)
    sample_rows = lambda i, j: (jnp.maximum(i - npb, 0), jnp.where(i < npb, 0, j))
    if isinstance(h, tuple):
        m, n = h[0].shape[0] + h[1].shape[0], h[0].shape[1]
        h_specs = [pl.BlockSpec((tm, tn), prompt_rows), pl.BlockSpec((tm, tn), sample_rows)]
        aliases = {}
    else:
        m, n = h.shape
        h, h_specs, aliases = (h,), [pl.BlockSpec((tm, tn), lambda i, j: (i, j))], {3: 0}
    return pl.pallas_call(
        functools.partial(_matmul_res_kernel, n_prompt_blocks=npb),
        grid=(m // tm, n // tn),
        in_specs=[pl.BlockSpec((tm, k), lambda i, j: (jnp.minimum(i, npb - 1), 0)),
                  pl.BlockSpec((tm, k), lambda i, j: (jnp.maximum(i - npb, 0), 0),
                               pipeline_mode=pl.Buffered(1)),
                  pl.BlockSpec((None, k, tn), lambda i, j: (layer, 0, j))] + h_specs,
        out_specs=pl.BlockSpec((tm, tn), lambda i, j: (i, j)),
        out_shape=jax.ShapeDtypeStruct((m, n), F32),
        input_output_aliases=aliases,
        compiler_params=_cparams("arbitrary", "arbitrary"),
        name="out_proj",
    )(y_p, y_s, w, *h)


def _gates_kernel(xn_ref, wt_ref, alog_ref, dt_ref, o_ref, *, n_prompt_blocks, n_heads, dec_seq):
    i = pl.program_id(0)
    x = lax.dot_general(xn_ref[...], wt_ref[...].astype(BF16), (((1,), (1,)), ((), ())),
                        preferred_element_type=F32)
    lane = lax.broadcasted_iota(jnp.int32, x.shape, 1)
    row = lax.broadcasted_iota(jnp.int32, x.shape, 0)
    beta = jax.nn.sigmoid(x)
    a = x + dt_ref[...]
    softplus = jnp.maximum(a, 0.0) + jnp.log(1.0 + jnp.exp(-jnp.abs(a)))
    g = -jnp.exp(alog_ref[...]) * softplus
    seg = jnp.where(i < n_prompt_blocks, CHUNK, dec_seq)
    pos = row & (seg - 1)
    cs = g
    s = 1
    while s < CHUNK:
        cs = cs + jnp.where(pos >= s, pltpu.roll(cs, s, 0), 0.0)
        s *= 2
    o_ref[...] = jnp.where(lane < n_heads, beta, jnp.where(lane < 2 * n_heads, cs, 0.0))


def _gates(xn, w_tail_t, layer, a_log, dt_bias, n_prompt_rows, n_heads, dec_seq, tm=1024):
    m, k = xn.shape
    pad = lambda v: jnp.zeros((1, LANES), F32).at[0, n_heads:2 * n_heads].set(v.astype(F32))
    kern = functools.partial(_gates_kernel, n_prompt_blocks=n_prompt_rows // tm, n_heads=n_heads,
                             dec_seq=dec_seq)
    return pl.pallas_call(
        kern,
        grid=(m // tm,),
        in_specs=[pl.BlockSpec((tm, k), lambda i: (i, 0)),
                  pl.BlockSpec((None, LANES, k), lambda i: (layer, 0, 0)),
                  pl.BlockSpec((1, LANES), lambda i: (0, 0)),
                  pl.BlockSpec((1, LANES), lambda i: (0, 0))],
        out_specs=pl.BlockSpec((tm, LANES), lambda i: (i, 0)),
        out_shape=jax.ShapeDtypeStruct((m, LANES), F32),
        compiler_params=_cparams("arbitrary"),
        name="delta_gates",
    )(xn, w_tail_t, pad(a_log), pad(dt_bias))


QUAD = 4


def _blockdiag(y, masks):
    yb = y.astype(BF16)
    zero = jnp.zeros_like(yb)
    return jnp.concatenate([jnp.where(m, yb, zero) for m in masks], axis=0)


def _blockdiag_wide(r):
    rb = r.astype(BF16)
    z = jnp.zeros_like(rb[:, :HEAD_DIM])
    return jnp.concatenate([jnp.concatenate([rb[:, :HEAD_DIM], z], axis=1),
                            jnp.concatenate([z, rb[:, HEAD_DIM:]], axis=1)], axis=0)


def _inverse_chain_stages(l_cats, seg, masks, out):
    c = l_cats[0].shape[0]
    dot = lambda a, b: jnp.dot(a, _blockdiag(b, masks), preferred_element_type=F32)
    ps = [-l for l in l_cats]
    lbs = [l.astype(BF16) for l in l_cats]
    qs = [dot(lb, lb) for lb in lbs]
    yield
    n_steps = seg.bit_length() - 2
    for step in range(n_steps):
        qbs = [q.astype(BF16) for q in qs]
        if step < n_steps - 1:
            pqs = [dot(jnp.concatenate([p.astype(BF16), qb], axis=0), qb) for p, qb in zip(ps, qbs)]
            ps = [p + q + pq[:c] for p, q, pq in zip(ps, qs, pqs)]
            qs = [pq[c:] for pq in pqs]
        else:
            pqs = [dot(p.astype(BF16), qb) for p, qb in zip(ps, qbs)]
            ps = [p + q + pq for p, q, pq in zip(ps, qs, pqs)]
        yield
    out.extend(ps)


def _unit_lower_inverses_minus_identity(l_cats, seg, masks):
    out = []
    for _ in _inverse_chain_stages(l_cats, seg, masks, out):
        pass
    return out


def _interleave(*stage_generators):
    live = list(stage_generators)
    while live:
        for g in list(live):
            try:
                next(g)
            except StopIteration:
                live.remove(g)


def _gram_quad(kn_a, qn_a, kn_b, qn_b):
    zero = jnp.zeros_like(kn_a)
    lhs = jnp.concatenate([jnp.concatenate([kn_a, kn_b], axis=1),
                           jnp.concatenate([qn_a, qn_b], axis=1)], axis=0)
    ka = jnp.concatenate([kn_a, zero], axis=1)
    kb = jnp.concatenate([zero, kn_b], axis=1)
    rhs = jnp.concatenate([ka, ka, kb, kb], axis=0)
    return lax.dot_general(lhs, rhs, (((1,), (1,)), ((), ())), preferred_element_type=F32)


def _quad_masks(c, seg):
    lane = lax.broadcasted_iota(jnp.int32, (c, QUAD * c), 1)
    ii = lax.broadcasted_iota(jnp.int32, (c, QUAD * c), 0)
    jj = lane & (c - 1)
    groups = [(lane // c) == g for g in range(QUAD)]
    incl = (ii >= jj) if seg == c else ((ii >= jj) & ((ii // seg) == (jj // seg)))
    return groups, incl, ii > jj


def _quad_select(g_all, base, groups):
    c = g_all.shape[0]
    assert QUAD == 4 and 2 * c == LANES
    low = lax.broadcasted_iota(jnp.int32, (c, LANES), 1) < c
    col = lambda g: jnp.broadcast_to(g_all[:, base + g:base + g + 1], (c, LANES))
    return jnp.concatenate([jnp.where(low, col(0), col(1)), jnp.where(low, col(2), col(3))], axis=1)


def _head_cols(g_all, idx, width):
    c = g_all.shape[0]
    return jnp.concatenate([jnp.broadcast_to(g_all[:, idx:idx + 1], (c, width)),
                            jnp.broadcast_to(g_all[:, idx + 1:idx + 2], (c, width))], axis=1)


def _gated_out(o, z, onw):
    on = o * lax.rsqrt(jnp.mean(o * o, axis=-1, keepdims=True) + NORM_EPS) * onw
    return (on * _silu(z)).astype(BF16)


def _conv_taps(ext_ref, w_ref, start):
    assert CONV_WIDTH == 4 and start >= CONV_WIDTH - 1 and start % SUBLANES == 0
    axis = len(ext_ref.shape) - 2
    x = ext_ref[...]
    x1 = pltpu.roll(x, 1, axis)
    a = w_ref[3:4, :] * x + w_ref[2:3, :] * x1
    b = w_ref[1:2, :] * x + w_ref[0:1, :] * x1
    y = a + pltpu.roll(b, 2, axis)
    y = y[start:] if axis == 0 else y[:, start:]
    return _silu(y)


def _l2norm(x):
    return x * lax.rsqrt(jnp.sum(x * x, axis=-1, keepdims=True) + NORM_EPS)


def _delta_prompt_kernel(*refs, n_pairs, n_chunks, n_heads, batch_chunks, make_sample_state):
    q_ref, k_ref, v_ref, z_ref, gc_ref, gr_ref, wq_ref, wk_ref, wv_ref, onw_ref = refs[:10]
    y_ref, s_ref, ccq_ref, cck_ref, ccv_ref = refs[10:15]
    if make_sample_state:
        refs[15][...] = jnp.zeros(refs[15].shape, F32)
        refs = refs[1:]
    eq_ref, ek_ref, ev_ref, qn_ref, kn_ref, vv_ref, gt_ref, p_ref, pi_ref = refs[15:]
    c = CHUNK
    rows = n_chunks * c
    pg = pl.program_id(1)
    cb = pl.program_id(2)
    hist = SUBLANES

    @pl.when(cb == 0)
    def _():
        eq_ref[0:hist, :] = jnp.zeros((hist, eq_ref.shape[1]), F32)
        ek_ref[0:hist, :] = jnp.zeros((hist, ek_ref.shape[1]), F32)
        ev_ref[0:hist, :] = jnp.zeros((hist, ev_ref.shape[1]), F32)
        s_ref[...] = jnp.zeros(s_ref.shape, F32)

    eq_ref[hist:hist + rows, :] = q_ref[...]
    ek_ref[hist:hist + rows, :] = k_ref[...]
    ev_ref[hist:hist + rows, :] = v_ref[...]
    xq = _conv_taps(eq_ref, wq_ref, hist)
    xk = _conv_taps(ek_ref, wk_ref, hist)
    vv_ref[...] = _conv_taps(ev_ref, wv_ref, hist)
    for p in range(n_pairs):
        sl = slice(p * HEAD_DIM, (p + 1) * HEAD_DIM)
        qn_ref[:, sl] = (_l2norm(xq[:, sl]) * (HEAD_DIM ** -0.5)).astype(BF16)
        kn_ref[:, sl] = _l2norm(xk[:, sl]).astype(BF16)
    keep = CONV_WIDTH - 1
    ccq_ref[0] = eq_ref[hist + rows - keep:hist + rows, :]
    cck_ref[0] = ek_ref[hist + rows - keep:hist + rows, :]
    ccv_ref[0] = ev_ref[hist + rows - keep:hist + rows, :]
    eq_ref[0:hist, :] = eq_ref[rows:rows + hist, :]
    ek_ref[0:hist, :] = ek_ref[rows:rows + hist, :]
    ev_ref[0:hist, :] = ev_ref[rows:rows + hist, :]

    shift = lax.rem(LANES - 2 * n_pairs * pg, LANES)
    gt_ref[...] = pltpu.roll(gc_ref[...], shift, 1)

    n_quads = n_pairs // 2
    groups, incl, strict = _quad_masks(c, c)
    pair_row = lambda ci, p: ci * (n_heads // 2) + n_pairs * pg + p
    lane_wide = lax.broadcasted_iota(jnp.int32, (1, 2 * HEAD_DIM), 1)
    pairs = range(n_pairs)

    def matrices(chunks):
        grams = {}
        for ci in chunks:
            hd = lambda ref, p: ref[ci * c:(ci + 1) * c, p * HEAD_DIM:(p + 1) * HEAD_DIM]
            for qd in range(n_quads):
                grams[ci, qd] = _gram_quad(hd(kn_ref, 2 * qd), hd(qn_ref, 2 * qd), hd(kn_ref, 2 * qd + 1),
                                           hd(qn_ref, 2 * qd + 1))
        yield
        l_cats = []
        for ci in chunks:
            g_all = gt_ref[ci * c:(ci + 1) * c, :]
            for qd in range(n_quads):
                ab = grams[ci, qd]
                gcr = jnp.concatenate([gr_ref[0, 0, pl.ds(pair_row(ci, 2 * qd), 1), :],
                                       gr_ref[0, 0, pl.ds(pair_row(ci, 2 * qd + 1), 1), :]], axis=1)
                beta = _quad_select(g_all, QUAD * qd, groups)
                gc = _quad_select(g_all, n_heads + QUAD * qd, groups)
                dec = jnp.exp(jnp.where(incl, gc - gcr, NEG_BIG))
                pi_ref[ci, qd] = (ab[c:] * dec).astype(BF16)
                l_cats.append(jnp.where(strict, beta * ab[:c] * dec, 0.0))
        pms = []
        yield from _inverse_chain_stages(l_cats, c, groups, pms)
        for n, pm in enumerate(pms):
            p_ref[chunks[n // n_quads], n % n_quads] = pm.astype(BF16)

    def recurrence(chunks):
        for ci in chunks:
            r0 = ci * c
            g_all = gt_ref[r0:r0 + c, :]
            eg_all = jnp.exp(g_all)
            half = lambda ref, p: ref[ci, p // 2, :, (p % 2) * 2 * c:(p % 2 + 1) * 2 * c]
            kns = [kn_ref[r0:r0 + c, p * HEAD_DIM:(p + 1) * HEAD_DIM] for p in pairs]
            qns = [qn_ref[r0:r0 + c, p * HEAD_DIM:(p + 1) * HEAD_DIM] for p in pairs]
            s_cat = lambda p: jnp.concatenate([s_ref[0, 2 * p], s_ref[0, 2 * p + 1]], axis=1)
            kqs = [jnp.dot(jnp.concatenate([kns[p], qns[p]], axis=0), s_cat(p).astype(BF16),
                           preferred_element_type=F32) for p in pairs]
            yield
            beta_w = [_head_cols(g_all, 2 * p, HEAD_DIM) for p in pairs]
            gc_w = [_head_cols(g_all, n_heads + 2 * p, HEAD_DIM) for p in pairs]
            egc_w = [_head_cols(eg_all, n_heads + 2 * p, HEAD_DIM) for p in pairs]
            rs = [beta_w[p] * (vv_ref[r0:r0 + c, 2 * p * HEAD_DIM:(2 * p + 2) * HEAD_DIM]
                               - egc_w[p] * kqs[p][:c]) for p in pairs]
            us = [rs[p] + jnp.dot(half(p_ref, p), _blockdiag_wide(rs[p]), preferred_element_type=F32)
                  for p in pairs]
            yield
            for p in pairs:
                gcr = gr_ref[0, 0, pl.ds(pair_row(ci, p), 1), :]
                gl_w = jnp.where(lane_wide < HEAD_DIM, gcr[:, c - 1:c], gcr[:, 2 * c - 1:2 * c])
                w = (jnp.exp(gl_w - gc_w[p]) * us[p]).astype(BF16)
                s_new = s_cat(p) * jnp.exp(gl_w) + lax.dot_general(
                    kns[p], w, (((0,), (0,)), ((), ())), preferred_element_type=F32)
                s_ref[0, 2 * p] = s_new[:, :HEAD_DIM]
                s_ref[0, 2 * p + 1] = s_new[:, HEAD_DIM:]
            yield
            os_ = [egc_w[p] * kqs[p][c:] + jnp.dot(half(pi_ref, p), _blockdiag_wide(us[p]),
                                                   preferred_element_type=F32) for p in pairs]
            yield
            for p in pairs:
                for e in range(2):
                    hs = slice((2 * p + e) * HEAD_DIM, (2 * p + e + 1) * HEAD_DIM)
                    y_ref[r0:r0 + c, hs] = _gated_out(os_[p][:, e * HEAD_DIM:(e + 1) * HEAD_DIM],
                                                      z_ref[r0:r0 + c, hs], onw_ref[...])

    for c0 in range(0, n_chunks, batch_chunks):
        _interleave(matrices(list(range(c0, c0 + batch_chunks))))
    _interleave(recurrence(range(n_chunks)))


def _delta_prompt(proj, gates, gates_rows, conv_w, o_norm_w, layer, batch, seq, n_heads, d_inner,
                  sample_state_shape=None, n_pairs=8, n_chunks=CHUNKS_PER_STEP, batch_chunks=2):
    qk_dim = n_heads // 2 * HEAD_DIM
    rows = n_chunks * CHUNK
    nrb = seq // rows
    wq, wv = n_pairs * HEAD_DIM, 2 * n_pairs * HEAD_DIM
    n_pg = n_heads // (2 * n_pairs)
    kern = functools.partial(_delta_prompt_kernel, n_pairs=n_pairs, n_chunks=n_chunks, n_heads=n_heads,
                             batch_chunks=batch_chunks, make_sample_state=sample_state_shape is not None)
    row_map = lambda col0: (lambda b, g, r: (b * nrb + r, col0 + g))
    keep = CONV_WIDTH - 1
    out_shapes = (
        jax.ShapeDtypeStruct((batch * seq, d_inner), BF16),
        jax.ShapeDtypeStruct((batch, n_heads, HEAD_DIM, HEAD_DIM), F32),
        jax.ShapeDtypeStruct((batch, keep, qk_dim), F32),
        jax.ShapeDtypeStruct((batch, keep, qk_dim), F32),
        jax.ShapeDtypeStruct((batch, keep, d_inner), F32),
    )
    extra_specs = ()
    if sample_state_shape is not None:
        n_layers, sample_batch = sample_state_shape[:2]
        seqs_per_step = sample_batch // (batch * nrb)
        assert seqs_per_step * batch * nrb == sample_batch
        out_shapes += (jax.ShapeDtypeStruct(sample_state_shape, F32),)
        extra_specs = (pl.BlockSpec((n_layers, seqs_per_step, 2 * n_pairs, HEAD_DIM, HEAD_DIM),
                                    lambda b, g, r: (0, b * nrb + r, g, 0, 0)),)
    return pl.pallas_call(
        kern,
        grid=(batch, n_pg, nrb),
        in_specs=[
            pl.BlockSpec((rows, wq), row_map(0)),
            pl.BlockSpec((rows, wq), row_map(qk_dim // wq)),
            pl.BlockSpec((rows, wv), row_map(2 * qk_dim // wv)),
            pl.BlockSpec((rows, wv), row_map((2 * qk_dim + d_inner) // wv)),
            pl.BlockSpec((rows, LANES), lambda b, g, r: (b * nrb + r, 0)),
            pl.BlockSpec((1, 1, n_chunks * (n_heads // 2), 2 * CHUNK), lambda b, g, r: (b, r, 0, 0)),
            pl.BlockSpec((None, CONV_WIDTH, wq), lambda b, g, r: (layer, 0, g)),
            pl.BlockSpec((None, CONV_WIDTH, wq), lambda b, g, r: (layer, 0, qk_dim // wq + g)),
            pl.BlockSpec((None, CONV_WIDTH, wv), lambda b, g, r: (layer, 0, 2 * qk_dim // wv + g)),
            pl.BlockSpec((None, 1, HEAD_DIM), lambda b, g, r: (layer, 0, 0)),
        ],
        out_specs=(
            pl.BlockSpec((rows, wv), lambda b, g, r: (b * nrb + r, g)),
            pl.BlockSpec((1, 2 * n_pairs, HEAD_DIM, HEAD_DIM), lambda b, g, r: (b, g, 0, 0)),
            pl.BlockSpec((1, keep, wq), lambda b, g, r: (b, 0, g)),
            pl.BlockSpec((1, keep, wq), lambda b, g, r: (b, 0, g)),
            pl.BlockSpec((1, keep, wv), lambda b, g, r: (b, 0, g)),
        ) + extra_specs,
        out_shape=out_shapes,
        scratch_shapes=[
            pltpu.VMEM((SUBLANES + rows, wq), F32),
            pltpu.VMEM((SUBLANES + rows, wq), F32),
            pltpu.VMEM((SUBLANES + rows, wv), F32),
            pltpu.VMEM((rows, wq), BF16),
            pltpu.VMEM((rows, wq), BF16),
            pltpu.VMEM((rows, wv), F32),
            pltpu.VMEM((rows, LANES), F32),
            pltpu.VMEM((n_chunks, n_pairs // 2, CHUNK, QUAD * CHUNK), BF16),
            pltpu.VMEM((n_chunks, n_pairs // 2, CHUNK, QUAD * CHUNK), BF16),
        ],
        compiler_params=_cparams("arbitrary", "arbitrary", "arbitrary"),
        name="delta_prompt",
    )(proj, proj, proj, proj, gates, gates_rows, conv_w, conv_w, conv_w, o_norm_w)


N_SAMPLE_INPUTS = 14


def _delta_sample_kernel(*refs, n_pairs, n_seqs, dec_seq, n_heads, aliased, layer):
    (q_ref, k_ref, v_ref, z_ref, cq_ref, ck_ref, cv_ref, gc_ref, gr_ref, wq_ref, wk_ref, wv_ref,
     onw_ref, s_in_ref) = refs[:N_SAMPLE_INPUTS]
    refs = refs[N_SAMPLE_INPUTS + 1:] if aliased else refs[N_SAMPLE_INPUTS:]
    y_ref, s_all_ref, ccq_ref, cck_ref, ccv_ref, eq_ref, ek_ref, ev_ref = refs
    if aliased:
        s_out_ref = s_all_ref
    else:
        s_out_ref = s_all_ref.at[layer]
        for other in range(s_all_ref.shape[0]):
            if other != layer:
                s_all_ref[other] = jnp.zeros(s_all_ref.shape[1:], F32)
    c = n_seqs * dec_seq
    pg = pl.program_id(1)
    hist = SUBLANES

    def conv(x_ref, cache_ref, ext_ref, w_ref, cc_ref):
        cols = ext_ref.shape[2]
        keep = CONV_WIDTH - 1
        ext_ref[:, 0:hist, :] = jnp.zeros((n_seqs, hist, cols), F32)
        for t in range(keep):
            ext_ref[:, hist - keep + t, :] = cache_ref[t]
        ext_ref[:, hist:hist + dec_seq, :] = x_ref[...].reshape(n_seqs, dec_seq, cols)
        for t in range(keep):
            cc_ref[t] = ext_ref[:, hist + dec_seq - keep + t, :]
        return _conv_taps(ext_ref, w_ref, hist).reshape(c, cols)

    xq = conv(q_ref, cq_ref, eq_ref, wq_ref, ccq_ref)
    xk = conv(k_ref, ck_ref, ek_ref, wk_ref, cck_ref)
    xv = conv(v_ref, cv_ref, ev_ref, wv_ref, ccv_ref)

    shift = lax.rem(LANES - 2 * n_pairs * pg, LANES)
    g_all = pltpu.roll(gc_ref[...], shift, 1)
    lane_wide = lax.broadcasted_iota(jnp.int32, (1, 2 * HEAD_DIM), 1)
    row_seq = lax.broadcasted_iota(jnp.int32, (c, 2 * HEAD_DIM), 0) // dec_seq
    pairs = range(n_pairs)
    seqs = range(n_seqs)
    hd = lambda p: slice(p * HEAD_DIM, (p + 1) * HEAD_DIM)

    qns = [_l2norm(xq[:, hd(p)]) * (HEAD_DIM ** -0.5) for p in pairs]
    kns = [_l2norm(xk[:, hd(p)]) for p in pairs]
    knb = [k.astype(BF16) for k in kns]

    groups, incl, strict = _quad_masks(c, dec_seq)
    qnb = [q.astype(BF16) for q in qns]
    grams = [_gram_quad(knb[2 * qd], qnb[2 * qd], knb[2 * qd + 1], qnb[2 * qd + 1])
             for qd in range(n_pairs // 2)]
    gcrs = [gr_ref[0, pl.ds(n_pairs * pg + p, 1), :] for p in pairs]
    l_cats, pis = [], []
    for qd in range(n_pairs // 2):
        ab = grams[qd]
        gcr = jnp.concatenate([gcrs[2 * qd], gcrs[2 * qd + 1]], axis=1)
        beta = _quad_select(g_all, QUAD * qd, groups)
        gc = _quad_select(g_all, n_heads + QUAD * qd, groups)
        dec = jnp.exp(jnp.where(incl, gc - gcr, NEG_BIG))
        pis.append((ab[c:] * dec).astype(BF16))
        l_cats.append(jnp.where(strict, beta * ab[:c] * dec, 0.0))
    pms = [pm.astype(BF16) for pm in _unit_lower_inverses_minus_identity(l_cats, dec_seq, groups)]
    half = lambda mats, p: mats[p // 2][:, (p % 2) * 2 * c:(p % 2 + 1) * 2 * c]

    s_cat = lambda p, s: jnp.concatenate([s_in_ref[s, 2 * p], s_in_ref[s, 2 * p + 1]], axis=1)
    kqs = [[jnp.dot(jnp.concatenate([kns[p][s * dec_seq:(s + 1) * dec_seq],
                                     qns[p][s * dec_seq:(s + 1) * dec_seq]], axis=0).astype(BF16),
                    s_cat(p, s).astype(BF16), preferred_element_type=F32) for s in seqs] for p in pairs]
    k_s = [jnp.concatenate([kqs[p][s][:dec_seq] for s in seqs], axis=0) for p in pairs]
    q_s = [jnp.concatenate([kqs[p][s][dec_seq:] for s in seqs], axis=0) for p in pairs]

    def last_gc(p, s):
        last = (s + 1) * dec_seq - 1
        return jnp.where(lane_wide < HEAD_DIM, gcrs[p][:, last:last + 1], gcrs[p][:, c + last:c + last + 1])

    gls = [[last_gc(p, s) for s in seqs] for p in pairs]
    beta_w = [_head_cols(g_all, 2 * p, HEAD_DIM) for p in pairs]
    gc_w = [_head_cols(g_all, n_heads + 2 * p, HEAD_DIM) for p in pairs]
    egc_w = [jnp.exp(g) for g in gc_w]
    rs = [beta_w[p] * (xv[:, 2 * p * HEAD_DIM:(2 * p + 2) * HEAD_DIM] - egc_w[p] * k_s[p]) for p in pairs]
    us = [rs[p] + jnp.dot(half(pms, p), _blockdiag_wide(rs[p]), preferred_element_type=F32) for p in pairs]
    os_ = [egc_w[p] * q_s[p] + jnp.dot(half(pis, p), _blockdiag_wide(us[p]), preferred_element_type=F32)
           for p in pairs]
    for p in pairs:
        gl_rows = jnp.concatenate([jnp.broadcast_to(g, (dec_seq, 2 * HEAD_DIM)) for g in gls[p]], axis=0)
        w = jnp.exp(gl_rows - gc_w[p]) * us[p]
        for s in seqs:
            w_s = jnp.where(row_seq == s, w, 0.0).astype(BF16)
            s_new = s_cat(p, s) * jnp.exp(gls[p][s]) + lax.dot_general(
                knb[p], w_s, (((0,), (0,)), ((), ())), preferred_element_type=F32)
            s_out_ref[s, 2 * p] = s_new[:, :HEAD_DIM]
            s_out_ref[s, 2 * p + 1] = s_new[:, HEAD_DIM:]
    for p in pairs:
        for e in range(2):
            hs = slice((2 * p + e) * HEAD_DIM, (2 * p + e + 1) * HEAD_DIM)
            y_ref[:, hs] = _gated_out(os_[p][:, e * HEAD_DIM:(e + 1) * HEAD_DIM], z_ref[:, hs], onw_ref[...])


def _delta_sample(proj, gates, gates_rows, conv_cache, state, prev_state_out, conv_w, o_norm_w, layer,
                  row_off, batch, dec_seq, n_heads, d_inner, n_seqs=8):
    n_pairs = 8 if prev_state_out is not None else 4
    qk_dim = n_heads // 2 * HEAD_DIM
    c = n_seqs * dec_seq
    wq, wv = n_pairs * HEAD_DIM, 2 * n_pairs * HEAD_DIM
    n_pg = n_heads // (2 * n_pairs)
    rb0 = row_off // c
    aliased = prev_state_out is not None
    kern = functools.partial(_delta_sample_kernel, n_pairs=n_pairs, n_seqs=n_seqs, dec_seq=dec_seq,
                             n_heads=n_heads, aliased=aliased, layer=layer)
    row_map = lambda col0: (lambda i, g: (rb0 + i, col0 + g))
    keep = CONV_WIDTH - 1
    in_specs = [
        pl.BlockSpec((c, wq), row_map(0)),
        pl.BlockSpec((c, wq), row_map(qk_dim // wq)),
        pl.BlockSpec((c, wv), row_map(2 * qk_dim // wv)),
        pl.BlockSpec((c, wv), row_map((2 * qk_dim + d_inner) // wv)),
        pl.BlockSpec((None, keep, n_seqs, wq), lambda i, g: (layer, 0, i, g)),
        pl.BlockSpec((None, keep, n_seqs, wq), lambda i, g: (layer, 0, i, qk_dim // wq + g)),
        pl.BlockSpec((None, keep, n_seqs, wv), lambda i, g: (layer, 0, i, 2 * qk_dim // wv + g)),
        pl.BlockSpec((c, LANES), lambda i, g: (rb0 + i, 0)),
        pl.BlockSpec((1, n_heads // 2, 2 * c), lambda i, g: (i, 0, 0)),
        pl.BlockSpec((None, CONV_WIDTH, wq), lambda i, g: (layer, 0, g)),
        pl.BlockSpec((None, CONV_WIDTH, wq), lambda i, g: (layer, 0, qk_dim // wq + g)),
        pl.BlockSpec((None, CONV_WIDTH, wv), lambda i, g: (layer, 0, 2 * qk_dim // wv + g)),
        pl.BlockSpec((None, 1, HEAD_DIM), lambda i, g: (layer, 0, 0)),
        pl.BlockSpec((None, n_seqs, 2 * n_pairs, HEAD_DIM, HEAD_DIM), lambda i, g: (layer, i, g, 0, 0)),
    ]
    args = [proj, proj, proj, proj, conv_cache, conv_cache, conv_cache, gates, gates_rows,
            conv_w, conv_w, conv_w, o_norm_w, state]
    aliases = {}
    if aliased:
        in_specs.append(pl.BlockSpec(memory_space=pl.ANY))
        args.append(prev_state_out)
        aliases = {N_SAMPLE_INPUTS: 1}
    state_blk = (n_seqs, 2 * n_pairs, HEAD_DIM, HEAD_DIM)
    if aliased:
        state_spec = pl.BlockSpec((None,) + state_blk, lambda i, g: (layer, i, g, 0, 0))
    else:
        state_spec = pl.BlockSpec((state.shape[0],) + state_blk, lambda i, g: (0, i, g, 0, 0))
    out_shapes = (
        jax.ShapeDtypeStruct((batch * dec_seq, d_inner), BF16),
        jax.ShapeDtypeStruct(state.shape, F32),
        jax.ShapeDtypeStruct((keep, batch, qk_dim), F32),
        jax.ShapeDtypeStruct((keep, batch, qk_dim), F32),
        jax.ShapeDtypeStruct((keep, batch, d_inner), F32),
    )
    return pl.pallas_call(
        kern,
        grid=(batch // n_seqs, n_pg),
        in_specs=in_specs,
        out_specs=(
            pl.BlockSpec((c, wv), lambda i, g: (i, g)),
            state_spec,
            pl.BlockSpec((keep, n_seqs, wq), lambda i, g: (0, i, g)),
            pl.BlockSpec((keep, n_seqs, wq), lambda i, g: (0, i, g)),
            pl.BlockSpec((keep, n_seqs, wv), lambda i, g: (0, i, g)),
        ),
        out_shape=out_shapes,
        scratch_shapes=[
            pltpu.VMEM((n_seqs, SUBLANES + dec_seq, wq), F32),
            pltpu.VMEM((n_seqs, SUBLANES + dec_seq, wq), F32),
            pltpu.VMEM((n_seqs, SUBLANES + dec_seq, wv), F32),
        ],
        input_output_aliases=aliases,
        compiler_params=_cparams("arbitrary", "arbitrary"),
        name="delta_sample",
    )(*args)


def _window_sums(x, win, axis):
    s = x
    w = 1
    while w < win:
        s = s + pltpu.roll(s, w, axis)
        w *= 2
    return s


def _pool_finish(pooled, z, wg_ref, sc_ref):
    mixed = jnp.dot(pooled.astype(BF16), wg_ref[...], preferred_element_type=F32)
    return (mixed * sc_ref[...] * _silu(z)).astype(BF16)


def _pool_prompt_kernel(u_ref, prev_ref, z_ref, wg_ref, sc_ref, y_ref, cache_ref, ext_ref, *, tt, nt):
    g = pl.program_id(0)
    t = pl.program_id(2)
    hist = POOL_BUF + 1

    @pl.when(t == 0)
    def _():
        ext_ref[0:hist, :] = jnp.zeros((hist, ext_ref.shape[1]), F32)

    @pl.when(t > 0)
    def _():
        ext_ref[0:hist, :] = prev_ref[...]

    ext_ref[hist:hist + tt, :] = u_ref[...]
    pos = t * tt + lax.broadcasted_iota(jnp.int32, (tt, 1), 0)
    for gi, win in enumerate(POOL_WINDOWS):
        @pl.when(g == gi)
        def _(win=win):
            acc = _window_sums(ext_ref[...], win, 0)[hist:]
            cnt = jnp.minimum(pos + 1, win).astype(F32)
            y_ref[...] = _pool_finish(acc / cnt - u_ref[...], z_ref[...], wg_ref, sc_ref)

    @pl.when(t == nt - 1)
    def _():
        cache_ref[0] = ext_ref[hist + tt - POOL_BUF:hist + tt, :]


def _pool_prompt(proj, w_group, scale, layer, batch, seq, d_inner, tt=512):
    gd = d_inner // len(POOL_WINDOWS)
    nt = seq // tt
    hist = POOL_BUF + 1
    kern = functools.partial(_pool_prompt_kernel, tt=tt, nt=nt)
    n_g = len(POOL_WINDOWS)
    return pl.pallas_call(
        kern,
        grid=(n_g, batch, nt),
        in_specs=[
            pl.BlockSpec((tt, gd), lambda g, b, t: (b * nt + t, g)),
            pl.BlockSpec((hist, gd), lambda g, b, t: (jnp.maximum((b * nt + t) * (tt // hist) - 1, 0), g)),
            pl.BlockSpec((tt, gd), lambda g, b, t: (b * nt + t, n_g + g)),
            pl.BlockSpec((None, None, gd, gd), lambda g, b, t: (layer, g, 0, 0)),
            pl.BlockSpec((None, 1, gd), lambda g, b, t: (layer, 0, g)),
        ],
        out_specs=(
            pl.BlockSpec((tt, gd), lambda g, b, t: (b * nt + t, g)),
            pl.BlockSpec((1, POOL_BUF, gd), lambda g, b, t: (b, 0, g)),
        ),
        out_shape=(
            jax.ShapeDtypeStruct((batch * seq, d_inner), BF16),
            jax.ShapeDtypeStruct((batch, POOL_BUF, d_inner), F32),
        ),
        scratch_shapes=[pltpu.VMEM((hist + tt, gd), F32)],
        compiler_params=_cparams("arbitrary", "arbitrary", "arbitrary"),
        name="pool_prompt",
    )(proj, proj, proj, w_group, scale)


def _pool_sample_kernel(*refs, sb, dec_seq, aliased, layer):
    u_ref, c_ref, z_ref, wg_ref, sc_ref = refs[:5]
    y_ref, cache_all_ref, ext_ref = refs[6:] if aliased else refs[5:]
    g = pl.program_id(0)
    hist = POOL_BUF + 1
    gd = ext_ref.shape[2]
    if aliased:
        cache_ref = cache_all_ref
    else:
        cache_ref = cache_all_ref.at[layer]
        for other in range(cache_all_ref.shape[0]):
            if other != layer:
                cache_all_ref[other] = jnp.zeros(cache_all_ref.shape[1:], F32)
    ext_ref[:, 0:SUBLANES, :] = jnp.zeros((sb, SUBLANES, gd), F32)
    for t in range(POOL_BUF):
        ext_ref[:, hist - POOL_BUF + t, :] = c_ref[t]
    ext_ref[:, hist:hist + dec_seq, :] = u_ref[...].reshape(sb, dec_seq, gd)
    pos = PAST_LEN + lax.broadcasted_iota(jnp.int32, (1, dec_seq, 1), 1)
    for gi, win in enumerate(POOL_WINDOWS):
        @pl.when(g == gi)
        def _(win=win):
            x = ext_ref[:, hist:hist + dec_seq, :]
            acc = _window_sums(ext_ref[...], win, 1)[:, hist:, :]
            cnt = jnp.minimum(pos + 1, win).astype(F32)
            pooled = (acc / cnt - x).reshape(sb * dec_seq, gd)
            y_ref[...] = _pool_finish(pooled, z_ref[...], wg_ref, sc_ref)

    for t in range(POOL_BUF):
        cache_ref[t] = ext_ref[:, hist + dec_seq - POOL_BUF + t, :]


def _pool_sample(proj, cache, prev_cache_out, w_group, scale, layer, row_off, batch, dec_seq, d_inner, sb=16):
    gd = d_inner // len(POOL_WINDOWS)
    n_g = len(POOL_WINDOWS)
    rows = sb * dec_seq
    rb0 = row_off // rows
    hist = POOL_BUF + 1
    aliased = prev_cache_out is not None
    kern = functools.partial(_pool_sample_kernel, sb=sb, dec_seq=dec_seq, aliased=aliased, layer=layer)
    in_specs = [
        pl.BlockSpec((rows, gd), lambda g, i: (rb0 + i, g)),
        pl.BlockSpec((None, POOL_BUF, sb, gd), lambda g, i: (layer, 0, i, g)),
        pl.BlockSpec((rows, gd), lambda g, i: (rb0 + i, n_g + g)),
        pl.BlockSpec((None, None, gd, gd), lambda g, i: (layer, g, 0, 0)),
        pl.BlockSpec((None, 1, gd), lambda g, i: (layer, 0, g)),
    ]
    args = [proj, cache, proj, w_group, scale]
    if aliased:
        in_specs.append(pl.BlockSpec(memory_space=pl.ANY))
        args.append(prev_cache_out)
        cache_spec = pl.BlockSpec((None, POOL_BUF, sb, gd), lambda g, i: (layer, 0, i, g))
    else:
        cache_spec = pl.BlockSpec((cache.shape[0], POOL_BUF, sb, gd), lambda g, i: (0, 0, i, g))
    return pl.pallas_call(
        kern,
        grid=(n_g, batch // sb),
        in_specs=in_specs,
        out_specs=(pl.BlockSpec((rows, gd), lambda g, i: (i, g)), cache_spec),
        out_shape=(
            jax.ShapeDtypeStruct((batch * dec_seq, d_inner), BF16),
            jax.ShapeDtypeStruct(cache.shape, F32),
        ),
        scratch_shapes=[pltpu.VMEM((sb, hist + dec_seq, gd), F32)],
        input_output_aliases={len(args) - 1: 1} if aliased else {},
        compiler_params=_cparams("arbitrary", "arbitrary"),
        name="pool_sample",
    )(*args)


def kernel(x_prompt, x_sample, state_delta, cache_conv, cache_pool, norm_w, final_norm_w, delta_w_in,
           delta_conv_w, delta_a_log, delta_dt_bias, delta_o_norm_w, delta_w_out, pool_w_in, pool_w_group,
           pool_scale, pool_w_out):
    batch, seq, d_model = x_prompt.shape
    dec_batch, dec_seq, _ = x_sample.shape
    depth = norm_w.shape[0]
    n_heads = state_delta.shape[2]
    d_inner = n_heads * HEAD_DIM
    conv_dim = cache_conv.shape[3]
    qk_dim = (conv_dim - d_inner) // 2
    mp, ms = batch * seq, dec_batch * dec_seq
    m = mp + ms
    n_main = conv_dim + d_inner
    n_seqs = CHUNK // dec_seq

    h = (x_prompt.reshape(mp, d_model), x_sample.reshape(ms, d_model))

    delta_w_out_b = delta_w_out.astype(BF16)
    pool_w_out_b = pool_w_out.astype(BF16)
    pool_w_group_b = pool_w_group.astype(BF16)
    delta_w_in_t = jnp.swapaxes(delta_w_in, 1, 2)
    w_tail_t = jnp.pad(delta_w_in_t[:, n_main:, :], ((0, 0), (0, LANES - 2 * n_heads), (0, 0)))
    onw = delta_o_norm_w.reshape(-1, 1, HEAD_DIM)
    pool_scale3 = pool_scale.reshape(pool_scale.shape[0], 1, d_inner)
    cache_conv_t = jnp.swapaxes(cache_conv, 1, 2)
    cache_pool_t = jnp.swapaxes(cache_pool, 1, 2)

    sd_p, cc_p, cp_p, cc_s = [], [], [], []
    sd_s = cp_s = None
    for i in range(depth):
        j = i // 2
        if isinstance(h, tuple):
            xn = _rmsnorm_pair(h[0], h[1], norm_w[i], BF16)
        else:
            xn = _rmsnorm(h, norm_w[i], BF16, m)
        if i % 2 == 0:
            proj = _matmul(xn, delta_w_in_t, j, n_main, tm=3072, tn=512, w_transposed=True)
            gates = _gates(xn, w_tail_t, j, delta_a_log[j], delta_dt_bias[j], mp, n_heads, dec_seq)
            gcum = gates[:, n_heads:2 * n_heads]
            rows_p = gcum[:mp].reshape(batch, seq // CHUNK, CHUNK, n_heads // 2, 2)
            rows_p = rows_p.transpose(0, 1, 3, 4, 2).reshape(batch, seq // (CHUNKS_PER_STEP * CHUNK), CHUNKS_PER_STEP * (n_heads // 2), 2 * CHUNK)
            rows_s = gcum[mp:].reshape(ms // CHUNK, CHUNK, n_heads // 2, 2)
            rows_s = rows_s.transpose(0, 2, 3, 1).reshape(ms // CHUNK, n_heads // 2, 2 * CHUNK)
            outs = _delta_prompt(proj, gates, rows_p, delta_conv_w, onw, j, batch, seq, n_heads, d_inner,
                                 sample_state_shape=state_delta.shape if sd_s is None else None)
            y_p, sp, cq, ck, cv = outs[:5]
            if sd_s is None:
                sd_s = outs[5]
            y_s, sd_s, cqs, cks, cvs = _delta_sample(proj, gates, rows_s, cache_conv_t, state_delta, sd_s,
                                                     delta_conv_w, onw, j, mp, dec_batch, dec_seq, n_heads,
                                                     d_inner, n_seqs=n_seqs)
            sd_p.append(sp)
            cc_p.append(jnp.concatenate([cq, ck, cv], axis=-1))
            cc_s.append(jnp.concatenate([cqs, cks, cvs], axis=-1))
            w_out, lw = delta_w_out_b, j
        else:
            proj = _matmul(xn, pool_w_in, j, 2 * d_inner, tm=3072, tn=512)
            y_p, pp = _pool_prompt(proj, pool_w_group_b, pool_scale3, j, batch, seq, d_inner)
            y_s, cp_s = _pool_sample(proj, cache_pool_t, cp_s, pool_w_group_b, pool_scale3, j, mp, dec_batch,
                                     dec_seq, d_inner)
            cp_p.append(pp)
            w_out, lw = pool_w_out_b, j
        h = _matmul_residual(y_p, y_s, w_out, lw, h)

    y_prompt = _rmsnorm(h, final_norm_w, F32, mp, 0).reshape(batch, seq, d_model)
    y_sample = _rmsnorm(h, final_norm_w, F32, ms, mp).reshape(dec_batch, dec_seq, d_model)
    return (y_prompt, y_sample, jnp.stack(sd_p), jnp.stack(cc_p), jnp.stack(cp_p),
            sd_s, jnp.swapaxes(jnp.stack(cc_s), 1, 2), jnp.swapaxes(cp_s, 1, 2))
```
